```python
import jax
import jax.numpy as jnp
from jax import lax
import numpy as np

D_MODEL = 4096
BATCH = 1
SEQ = 16384
DEPTH = 4

GRID_W = 64
CTX_LEN = 256
N_MIXERS = 4

MOD_RANK = 256
N_MOD = 6
MOD_SCALE = 0.5

DEEPNORM_ALPHA = (2 * DEPTH) ** 0.25
DEEPNORM_BETA = (8 * DEPTH) ** -0.25
LN_EPS = 1e-5
RMS_EPS = 1e-6

GM_WIDTH = D_MODEL
GM_CHUNK = 128
GM_GROUPS = 16

N_HEADS = 32
HEAD_DIM = D_MODEL // N_HEADS
N_KV_HEADS = 8
Q_BLOCK = 128
ROPE_THETA = 10000.0
ROPE_AXIS_DIM = HEAD_DIM // 2

NA_WIN_H = 8
NA_WIN_W = 16
RPB_SCALE = 0.1

FT_GROUPS = 8

N_EXPERTS = 32
N_EXPERT_GROUPS = 4
TOP_K_GROUPS = 2
TOP_K = 4
D_EXPERT = 384
D_SHARED = 384
ROUTED_SCALE = 2.5
ROUTER_BIAS_SCALE = 0.01
MOE_BLOCK = 128

N_GM = len(range(0, DEPTH, N_MIXERS))
N_NA = len(range(1, DEPTH, N_MIXERS))
N_FT = len(range(2, DEPTH, N_MIXERS))
N_GA = len(range(3, DEPTH, N_MIXERS))

kernel_name = 'hybrid_interleaved_dit_moe'


def layer_norm(x, g, b):
    xf = x.astype(jnp.float32)
    mu = jnp.mean(xf, axis=-1, keepdims=True)
    var = jnp.mean(jnp.square(xf - mu), axis=-1, keepdims=True)
    return ((xf - mu) * lax.rsqrt(var + LN_EPS) * g + b).astype(x.dtype)


def rms_norm(x, g):
    xf = x.astype(jnp.float32)
    return (xf * lax.rsqrt(jnp.mean(xf * xf, axis=-1, keepdims=True) + RMS_EPS) * g).astype(x.dtype)


def rope_1d(x, pos):
    d = x.shape[-1]
    inv = ROPE_THETA ** (-jnp.arange(0, d, 2, dtype=jnp.float32) / d)
    ang = pos.astype(jnp.float32)[:, None] * inv[None, :]
    cos, sin = jnp.cos(ang)[:, None, :], jnp.sin(ang)[:, None, :]
    x1, x2 = jnp.split(x.astype(jnp.float32), 2, axis=-1)
    return jnp.concatenate([x1 * cos - x2 * sin, x2 * cos + x1 * sin], axis=-1).astype(x.dtype)


def rope_2d(x, n):
    t = jnp.arange(n)
    return jnp.concatenate([rope_1d(x[..., :ROPE_AXIS_DIM], t // GRID_W),
                            rope_1d(x[..., ROPE_AXIS_DIM:], t % GRID_W)], axis=-1)


def ada_mod(cond, w_down, w_up, b):
    m = (jax.nn.silu(cond) @ w_down) @ w_up + b
    return m.reshape(cond.shape[0], N_MOD, 1, -1)


def attend(q, k, v):
    s = jnp.einsum('bqgrd,bkgd->bgrqk', q, k).astype(jnp.float32) * (q.shape[-1] ** -0.5)
    p = jax.nn.softmax(s, axis=-1).astype(v.dtype)
    return jnp.einsum('bgrqk,bkgd->bqgrd', p, v)


def chunk_gmlp(h, w_in, v_g, v_b, w_s, b_s, w_out):
    b, l, _ = h.shape
    u, v = jnp.split(jax.nn.gelu(h @ w_in), 2, axis=-1)
    v = layer_norm(v, v_g, v_b)
    v = v.reshape(b, l // GM_CHUNK, GM_CHUNK, GM_GROUPS, GM_WIDTH // GM_GROUPS)
    sv = jnp.einsum('gpq,bnqgc->bnpgc', w_s, v) + b_s.T[:, :, None]
    return (u * sv.reshape(b, l, GM_WIDTH)) @ w_out


def mixer_gmlp(hc, hl, with_ctx, w_in, v_g, v_b, w_s, b_s, w_out):
    nc = hc.shape[1]
    if with_ctx:
        o = chunk_gmlp(jnp.concatenate([hc, hl], axis=1), w_in, v_g, v_b, w_s, b_s, w_out)
        return o[:, :nc], o[:, nc:]
    return None, chunk_gmlp(hl, w_in, v_g, v_b, w_s, b_s, w_out)


def mixer_na(hc, hl, with_ctx, w_qkv, rpb, w_out):
    b, n, d = hl.shape
    nc = hc.shape[1]
    rows = n // GRID_W
    kh = min(NA_WIN_H, rows)
    scale = HEAD_DIM ** -0.5
    qkv = (jnp.concatenate([hc, hl], axis=1) @ w_qkv).reshape(b, nc + n, 3, N_HEADS, HEAD_DIM)
    q, k, v = qkv[:, :, 0], qkv[:, :, 1], qkv[:, :, 2]
    qc, kc, vc = q[:, :nc], k[:, :nc], v[:, :nc]
    ql = q[:, nc:].reshape(b, rows, GRID_W, N_HEADS, HEAD_DIM)
    kl = k[:, nc:].reshape(b, rows, GRID_W, N_HEADS, HEAD_DIM)
    vl = v[:, nc:].reshape(b, rows, GRID_W, N_HEADS, HEAD_DIM)

    row_start = jnp.clip(jnp.arange(rows) - kh // 2, 0, rows - kh)
    col_start = jnp.clip(jnp.arange(GRID_W) - NA_WIN_W // 2, 0, GRID_W - NA_WIN_W)
    col_idx = col_start[:, None] + jnp.arange(NA_WIN_W)[None, :]
    dc = col_idx - jnp.arange(GRID_W)[:, None] + NA_WIN_W - 1
    rpb_c = rpb[:, :, dc]

    def row_block(r):
        rs = row_start[r]
        qr = lax.dynamic_index_in_dim(ql, r, axis=1, keepdims=False)
        kr = jnp.take(lax.dynamic_slice_in_dim(kl, rs, kh, axis=1), col_idx, axis=2)
        vr = jnp.take(lax.dynamic_slice_in_dim(vl, rs, kh, axis=1), col_idx, axis=2)
        dr = rs + jnp.arange(kh) - r + NA_WIN_H - 1
        bias = jnp.take(rpb_c, dr, axis=1).transpose(0, 2, 1, 3)
        s_loc = jnp.einsum('bqhd,biqjhd->bhqij', qr, kr).astype(jnp.float32) * scale + bias[None]
        s_ctx = jnp.einsum('bqhd,bchd->bhqc', qr, kc).astype(jnp.float32) * scale
        s = jnp.concatenate([s_loc.reshape(b, N_HEADS, GRID_W, kh * NA_WIN_W), s_ctx], axis=-1)
        p = jax.nn.softmax(s, axis=-1).astype(qr.dtype)
        p_loc = p[..., :kh * NA_WIN_W].reshape(b, N_HEADS, GRID_W, kh, NA_WIN_W)
        return (jnp.einsum('bhqij,biqjhd->bqhd', p_loc, vr)
                + jnp.einsum('bhqc,bchd->bqhd', p[..., kh * NA_WIN_W:], vc))

    ol = lax.map(row_block, jnp.arange(rows))
    ol = ol.transpose(1, 0, 2, 3, 4).reshape(b, n, d) @ w_out
    oc = None
    if with_ctx:
        oc = attend(qc.reshape(b, nc, N_HEADS, 1, HEAD_DIM), kc, vc).reshape(b, nc, d) @ w_out
    return oc, ol


def fourier_mix(h, w_out):
    b, l, d = h.shape
    hg = h.astype(jnp.float32).reshape(b, l, FT_GROUPS, d // FT_GROUPS)
    f = jnp.fft.fft2(hg, axes=(1, 3), norm='ortho').real
    return f.reshape(b, l, d).astype(h.dtype) @ w_out


def mixer_ft(hc, hl, with_ctx, w_out):
    oc = fourier_mix(hc, w_out) if with_ctx else None
    return oc, fourier_mix(hl, w_out)


def mixer_gqa(hc, hl, with_ctx, w_qkv, q_g, k_g, w_out):
    b, n, d = hl.shape
    nc = hc.shape[1]
    kvd = N_KV_HEADS * HEAD_DIM
    rep = N_HEADS // N_KV_HEADS
    q, k, v = jnp.split(jnp.concatenate([hc, hl], axis=1) @ w_qkv, [d, d + kvd], axis=-1)
    q = rms_norm(q.reshape(b, nc + n, N_HEADS, HEAD_DIM), q_g)
    k = rms_norm(k.reshape(b, nc + n, N_KV_HEADS, HEAD_DIM), k_g)
    v = v.reshape(b, nc + n, N_KV_HEADS, HEAD_DIM)
    ql = rope_2d(q[:, nc:], n).reshape(b, n // Q_BLOCK, Q_BLOCK, N_KV_HEADS, rep, HEAD_DIM)
    k_all = jnp.concatenate([k[:, :nc], rope_2d(k[:, nc:], n)], axis=1)
    ol = lax.map(lambda qb: attend(qb, k_all, v), jnp.moveaxis(ql, 1, 0))
    ol = jnp.moveaxis(ol, 0, 1).reshape(b, n, d) @ w_out
    oc = None
    if with_ctx:
        qc = q[:, :nc].reshape(b, nc, N_KV_HEADS, rep, HEAD_DIM)
        oc = attend(qc, k[:, :nc], v[:, :nc]).reshape(b, nc, d) @ w_out
    return oc, ol


def moe(t, router_w, router_bias, w_gate, w_up, w_down, s_gate, s_up, s_down):
    n_tok, d = t.shape
    scores = jax.nn.sigmoid(t.astype(jnp.float32) @ router_w.astype(jnp.float32))
    biased = scores + router_bias.astype(jnp.float32)
    per_g = N_EXPERTS // N_EXPERT_GROUPS
    g_score = lax.top_k(biased.reshape(n_tok, N_EXPERT_GROUPS, per_g), 2)[0].sum(-1)
    _, g_idx = lax.top_k(g_score, TOP_K_GROUPS)
    g_mask = jnp.sum(jax.nn.one_hot(g_idx, N_EXPERT_GROUPS, dtype=jnp.float32), axis=1)
    e_mask = jnp.repeat(g_mask, per_g, axis=1) > 0
    _, e_idx = lax.top_k(jnp.where(e_mask, biased, -jnp.inf), TOP_K)
    w = jnp.take_along_axis(scores, e_idx, axis=1)
    w = w / jnp.sum(w, axis=-1, keepdims=True) * ROUTED_SCALE

    n_asg = n_tok * TOP_K
    flat_e = e_idx.reshape(-1)
    order = jnp.argsort(flat_e)
    sorted_e = flat_e[order]
    counts = jnp.bincount(flat_e, length=N_EXPERTS)
    padded = (counts + MOE_BLOCK - 1) // MOE_BLOCK * MOE_BLOCK
    pad_end = jnp.cumsum(padded)
    rank = jnp.arange(n_asg) - (jnp.cumsum(counts) - counts)[sorted_e]
    dest = (pad_end - padded)[sorted_e] + rank
    n_blocks = -(-(n_asg + N_EXPERTS * (MOE_BLOCK - 1)) // MOE_BLOCK)
    n_rows = n_blocks * MOE_BLOCK
    row_tok = jnp.zeros((n_rows,), jnp.int32).at[dest].set((order // TOP_K).astype(jnp.int32))
    row_w = jnp.zeros((n_rows,), jnp.float32).at[dest].set(w.reshape(-1)[order])
    block_e = jnp.minimum(jnp.searchsorted(pad_end, jnp.arange(n_blocks) * MOE_BLOCK, side='right'),
                          N_EXPERTS - 1)

    def expert_block(args):
        e, tok = args
        xb = t[tok]
        hb = jax.nn.silu(xb @ w_gate[e]) * (xb @ w_up[e])
        return hb @ w_down[e]

    y = lax.map(expert_block, (block_e, row_tok.reshape(n_blocks, MOE_BLOCK)))
    y = y.reshape(n_rows, d) * row_w[:, None].astype(t.dtype)
    routed = jnp.zeros_like(t).at[row_tok].add(y)
    shared = (jax.nn.silu(t @ s_gate) * (t @ s_up)) @ s_down
    return shared + routed


def setup_inputs(seed: int = 0) -> dict:
    key = jax.random.key(seed)
    ks = iter(jax.random.split(key, 32))
    f32 = jnp.float32
    D = D_MODEL

    def nrm(shape, scale):
        return jax.random.normal(next(ks), shape, f32) * scale

    def gain(shape):
        return 1.0 + nrm(shape, 0.02)

    kvd = N_KV_HEADS * HEAD_DIM
    return {
        'x': nrm((BATCH, SEQ, D), 1.0),
        'c': nrm((BATCH, D), 1.0),
        'ctx': nrm((BATCH, CTX_LEN, D), 1.0),
        'c_ctx': nrm((D,), 1.0),
        'w_mod_down': nrm((DEPTH, D, MOD_RANK), D ** -0.5),
        'w_mod_up': nrm((DEPTH, MOD_RANK, N_MOD * D), MOD_SCALE * MOD_RANK ** -0.5),
        'b_mod': nrm((DEPTH, N_MOD * D), 0.02),
        'ln_g': gain((DEPTH, 2, D)),
        'ln_b': nrm((DEPTH, 2, D), 0.02),
        'gm_w_in': nrm((N_GM, D, 2 * GM_WIDTH), D ** -0.5),
        'gm_v_g': gain((N_GM, GM_WIDTH)),
        'gm_v_b': nrm((N_GM, GM_WIDTH), 0.02),
        'gm_w_s': nrm((N_GM, GM_GROUPS, GM_CHUNK, GM_CHUNK), GM_CHUNK ** -0.5),
        'gm_b_s': gain((N_GM, GM_GROUPS, GM_CHUNK)),
        'gm_w_out': nrm((N_GM, GM_WIDTH, D), GM_WIDTH ** -0.5 * DEEPNORM_BETA),
        'na_w_qkv': nrm((N_NA, D, 3 * D), D ** -0.5),
        'na_rpb': nrm((N_NA, N_HEADS, 2 * NA_WIN_H - 1, 2 * NA_WIN_W - 1), RPB_SCALE),
        'na_w_out': nrm((N_NA, D, D), D ** -0.5 * DEEPNORM_BETA),
        'ft_w_out': nrm((N_FT, D, D), D ** -0.5 * DEEPNORM_BETA),
        'ga_w_qkv': nrm((N_GA, D, D + 2 * kvd), D ** -0.5),
        'ga_q_g': gain((N_GA, HEAD_DIM)),
        'ga_k_g': gain((N_GA, HEAD_DIM)),
        'ga_w_out': nrm((N_GA, D, D), D ** -0.5 * DEEPNORM_BETA),
        'router_w': nrm((DEPTH, D, N_EXPERTS), D ** -0.5),
        'router_bias': nrm((DEPTH, N_EXPERTS), ROUTER_BIAS_SCALE),
        'exp_w_gate': nrm((DEPTH, N_EXPERTS, D, D_EXPERT), D ** -0.5),
        'exp_w_up': nrm((DEPTH, N_EXPERTS, D, D_EXPERT), D ** -0.5),
        'exp_w_down': nrm((DEPTH, N_EXPERTS, D_EXPERT, D), D_EXPERT ** -0.5 * DEEPNORM_BETA),
        'sh_w_gate': nrm((DEPTH, D, D_SHARED), D ** -0.5),
        'sh_w_up': nrm((DEPTH, D, D_SHARED), D ** -0.5),
        'sh_w_down': nrm((DEPTH, D_SHARED, D), D_SHARED ** -0.5 * DEEPNORM_BETA),
    }


def reference(x, c, ctx, c_ctx, w_mod_down, w_mod_up, b_mod, ln_g, ln_b,
              gm_w_in, gm_v_g, gm_v_b, gm_w_s, gm_b_s, gm_w_out,
              na_w_qkv, na_rpb, na_w_out,
              ft_w_out,
              ga_w_qkv, ga_q_g, ga_k_g, ga_w_out,
              router_w, router_bias, exp_w_gate, exp_w_up, exp_w_down,
              sh_w_gate, sh_w_up, sh_w_down):
    b, n, d = x.shape
    nc = ctx.shape[1]
    cond_ctx = c_ctx[None, :]
    for i in range(DEPTH):
        with_ctx = i < DEPTH - 1
        kind, j = i % N_MIXERS, i // N_MIXERS
        m_lat = ada_mod(c, w_mod_down[i], w_mod_up[i], b_mod[i])
        m_ctx = ada_mod(cond_ctx, w_mod_down[i], w_mod_up[i], b_mod[i])

        hl = x * (1 + m_lat[:, 1]) + m_lat[:, 0]
        hc = ctx * (1 + m_ctx[:, 1]) + m_ctx[:, 0]
        if kind == 0:
            oc, ol = mixer_gmlp(hc, hl, with_ctx, gm_w_in[j], gm_v_g[j], gm_v_b[j], gm_w_s[j], gm_b_s[j], gm_w_out[j])
        elif kind == 1:
            oc, ol = mixer_na(hc, hl, with_ctx, na_w_qkv[j], na_rpb[j], na_w_out[j])
        elif kind == 2:
            oc, ol = mixer_ft(hc, hl, with_ctx, ft_w_out[j])
        else:
            oc, ol = mixer_gqa(hc, hl, with_ctx, ga_w_qkv[j], ga_q_g[j], ga_k_g[j], ga_w_out[j])
        x = layer_norm(DEEPNORM_ALPHA * x + m_lat[:, 2] * ol, ln_g[i, 0], ln_b[i, 0])
        if with_ctx:
            ctx = layer_norm(DEEPNORM_ALPHA * ctx + m_ctx[:, 2] * oc, ln_g[i, 0], ln_b[i, 0])

        hl = x * (1 + m_lat[:, 4]) + m_lat[:, 3]
        if with_ctx:
            hc = ctx * (1 + m_ctx[:, 4]) + m_ctx[:, 3]
            tok = jnp.concatenate([hc.reshape(-1, d), hl.reshape(-1, d)], axis=0)
        else:
            tok = hl.reshape(-1, d)
        y = moe(tok, router_w[i], router_bias[i], exp_w_gate[i], exp_w_up[i], exp_w_down[i],
                sh_w_gate[i], sh_w_up[i], sh_w_down[i])
        if with_ctx:
            yc = y[:b * nc].reshape(b, nc, d)
            yl = y[b * nc:].reshape(b, n, d)
            ctx = layer_norm(DEEPNORM_ALPHA * ctx + m_ctx[:, 5] * yc, ln_g[i, 1], ln_b[i, 1])
        else:
            yl = y.reshape(b, n, d)
        x = layer_norm(DEEPNORM_ALPHA * x + m_lat[:, 5] * yl, ln_g[i, 1], ln_b[i, 1])
    return x
```

```python
import functools
import math

import jax
import jax.numpy as jnp
import numpy as np
from jax import lax
from jax.experimental import pallas as pl
from jax.experimental.pallas import tpu as pltpu

F32 = jnp.float32
BF16 = jnp.bfloat16

D_MODEL = 4096
DEPTH = 4
GRID_W = 64
N_MOD = 6
LN_EPS = 1e-5
RMS_EPS = 1e-6
DEEPNORM_ALPHA = (2 * DEPTH) ** 0.25
GM_CHUNK = 128
GM_GROUPS = 16
N_HEADS = 32
HEAD_DIM = 128
N_KV_HEADS = 8
ROPE_THETA = 10000.0
NA_WIN_H = 8
NA_WIN_W = 16
FT_GROUPS = 8
N_EXPERTS = 32
N_EXPERT_GROUPS = 4
TOP_K_GROUPS = 2
TOP_K = 4
D_EXPERT = 384
ROUTED_SCALE = 2.5

LANES = 128
VMEM_LIMIT_BYTES = 56 * 1024 * 1024
ROW_TILE = 256
MM_TILE_M = 1280
MM_TILE_N = 1024
MOE_ROWS = 256
COMBINE_ROWS = 128
NA_Q_ROWS = 4
NA_KEY_ROWS = 12
GQA_TQ = 256
GQA_TK = 640
MASK_VALUE = -1e30


def _params(n_axes):
    return pltpu.CompilerParams(
        dimension_semantics=("arbitrary",) * n_axes,
        vmem_limit_bytes=VMEM_LIMIT_BYTES,
    )


def _dot(a, b, precision=None):
    return jnp.dot(a, b, preferred_element_type=F32, precision=precision)


def _dot_nt(a, b):
    return lax.dot_general(a, b, (((1,), (1,)), ((), ())), preferred_element_type=F32)


def _mm_kernel(a_ref, b_ref, o_ref, *, act):
    acc = _dot(a_ref[...], b_ref[...])
    if act == "gelu":
        acc = jax.nn.gelu(acc)
    o_ref[...] = acc.astype(o_ref.dtype)


def _matmul(a, b, *, out_dtype, act=None, tm=MM_TILE_M, tn=MM_TILE_N):
    m, k = a.shape
    n = b.shape[1]
    tm = min(tm, m)
    tn = min(tn, n)
    if jnp.dtype(out_dtype).itemsize == 4:
        tn = min(tn, MM_TILE_N // 2)
    while n % tn:
        tn //= 2
    assert m % tm == 0 and tn % LANES == 0
    return pl.pallas_call(
        functools.partial(_mm_kernel, act=act),
        grid=(m // tm, n // tn),
        in_specs=[
            pl.BlockSpec((tm, k), lambda i, j: (i, 0)),
            pl.BlockSpec((k, tn), lambda i, j: (0, j)),
        ],
        out_specs=pl.BlockSpec((tm, tn), lambda i, j: (i, j)),
        out_shape=jax.ShapeDtypeStruct((m, n), out_dtype),
        compiler_params=_params(2),
        name="matmul",
    )(a, b)


def _lmm_kernel(m_ref, x_ref, o_ref):
    o_ref[...] = _dot(m_ref[...], x_ref[...]).astype(o_ref.dtype)


def _left_matmul(mat, x, *, out_dtype, tn):
    r, k = mat.shape
    n = x.shape[1]
    assert n % tn == 0
    return pl.pallas_call(
        _lmm_kernel,
        grid=(n // tn,),
        in_specs=[
            pl.BlockSpec((r, k), lambda j: (0, 0)),
            pl.BlockSpec((k, tn), lambda j: (0, j)),
        ],
        out_specs=pl.BlockSpec((r, tn), lambda j: (0, j)),
        out_shape=jax.ShapeDtypeStruct((r, n), out_dtype),
        compiler_params=_params(1),
        name="left_matmul",
    )(mat, x)


def _mod_kernel(cond_ref, wd_ref, wu_ref, b_ref, o_ref):
    cond = cond_ref[...]
    t = _dot(cond * jax.nn.sigmoid(cond), wd_ref[...], precision=lax.Precision.HIGHEST)
    o_ref[...] = _dot(t, wu_ref[...], precision=lax.Precision.HIGHEST) + b_ref[...]


def _ada_mod_all(cond8, w_mod_down, w_mod_up, b_mod):
    depth, d, rank = w_mod_down.shape
    out = pl.pallas_call(
        _mod_kernel,
        grid=(depth, N_MOD),
        in_specs=[
            pl.BlockSpec((8, d), lambda l, j: (0, 0)),
            pl.BlockSpec((None, d, rank), lambda l, j: (l, 0, 0)),
            pl.BlockSpec((None, rank, d), lambda l, j: (l, 0, j)),
            pl.BlockSpec((None, None, 1, d), lambda l, j: (l, j, 0, 0)),
        ],
        out_specs=pl.BlockSpec((None, None, 8, d), lambda l, j: (l, j, 0, 0)),
        out_shape=jax.ShapeDtypeStruct((depth, N_MOD, 8, d), F32),
        compiler_params=_params(2),
        name="ada_mod",
    )(cond8, w_mod_down, w_mod_up, b_mod.reshape(depth, N_MOD, 1, d))
    return jnp.transpose(out[:, :, :2], (0, 2, 1, 3))[:, :, :, None, :]


def _mod_spec(slot, n_lat_blocks, d):
    return pl.BlockSpec((None, None, 1, d), lambda i: (i // n_lat_blocks, slot, 0, 0))


def _modulate_kernel(x_ref, sc_ref, sh_ref, h_ref):
    h_ref[...] = (x_ref[...] * (1.0 + sc_ref[...]) + sh_ref[...]).astype(h_ref.dtype)


def _modulate(x, mod, scale_slot, shift_slot, n_lat, out_dtype):
    t, d = x.shape
    tm = ROW_TILE
    nlb = n_lat // tm
    return pl.pallas_call(
        _modulate_kernel,
        grid=(t // tm,),
        in_specs=[
            pl.BlockSpec((tm, d), lambda i: (i, 0)),
            _mod_spec(scale_slot, nlb, d),
            _mod_spec(shift_slot, nlb, d),
        ],
        out_specs=pl.BlockSpec((tm, d), lambda i: (i, 0)),
        out_shape=jax.ShapeDtypeStruct((t, d), out_dtype),
        compiler_params=_params(1),
        name="modulate",
    )(x, mod, mod)


def _layer_norm_rows(z, g, b):
    mu = jnp.mean(z, axis=-1, keepdims=True)
    zc = z - mu
    var = jnp.mean(zc * zc, axis=-1, keepdims=True)
    return zc * lax.rsqrt(var + LN_EPS) * g + b


def _resln_kernel(*refs, with_h, with_router):
    x_ref, o_ref, gate_ref, g_ref, b_ref = refs[:5]
    pos = 5
    if with_h:
        sc_ref, sh_ref = refs[pos:pos + 2]
        pos += 2
    if with_router:
        rw_ref = refs[pos]
        pos += 1
    xo_ref = refs[pos]
    pos += 1
    z = DEEPNORM_ALPHA * x_ref[...] + gate_ref[...] * o_ref[...].astype(F32)
    xn = _layer_norm_rows(z, g_ref[...], b_ref[...])
    xo_ref[...] = xn
    if with_h:
        h_ref = refs[pos]
        pos += 1
        h = xn * (1.0 + sc_ref[...]) + sh_ref[...]
        h_ref[...] = h.astype(h_ref.dtype)
        if with_router:
            lg_ref = refs[pos]
            lg_ref[...] = _dot(h, rw_ref[...], precision=lax.Precision.HIGHEST)


def _resln(x, o, mod, gate_slot, ln_g, ln_b, n_lat, *, next_mod=None, scale_slot=None,
           shift_slot=None, h_dtype=None, router_w=None, n_out_rows=None):
    t, d = x.shape
    n_out_rows = t if n_out_rows is None else n_out_rows
    tm = ROW_TILE
    nlb = n_lat // tm
    with_h = next_mod is not None
    with_router = router_w is not None
    row = pl.BlockSpec((tm, d), lambda i: (i, 0))
    vec = pl.BlockSpec((1, d), lambda i: (0, 0))
    in_specs = [row, row, _mod_spec(gate_slot, nlb, d), vec, vec]
    args = [x, o, mod, ln_g.reshape(1, d), ln_b.reshape(1, d)]
    out_specs = [row]
    out_shape = [jax.ShapeDtypeStruct((n_out_rows, d), F32)]
    if with_h:
        in_specs += [_mod_spec(scale_slot, nlb, d), _mod_spec(shift_slot, nlb, d)]
        args += [next_mod, next_mod]
        out_specs.append(row)
        out_shape.append(jax.ShapeDtypeStruct((n_out_rows, d), h_dtype))
    if with_router:
        ne = router_w.shape[1]
        in_specs.append(pl.BlockSpec((d, ne), lambda i: (0, 0)))
        args.append(router_w)
        out_specs.append(pl.BlockSpec((tm, ne), lambda i: (i, 0)))
        out_shape.append(jax.ShapeDtypeStruct((n_out_rows, ne), F32))
    return pl.pallas_call(
        functools.partial(_resln_kernel, with_h=with_h, with_router=with_router),
        grid=(n_out_rows // tm,),
        in_specs=in_specs,
        out_specs=out_specs,
        out_shape=out_shape,
        compiler_params=_params(1),
        name="residual_layernorm",
    )(*args)


def _gmlp_gate_kernel(u_ref, v_ref, vg_ref, vb_ref, ws_ref, bs_ref, o_ref, *, n_groups, gw):
    vn = _layer_norm_rows(v_ref[...].astype(F32), vg_ref[...], vb_ref[...]).astype(BF16)
    for g in range(n_groups):
        cols = slice(g * gw, (g + 1) * gw)
        sv = _dot(ws_ref[g], vn[:, cols]) + bs_ref[:, cols]
        o_ref[:, cols] = (u_ref[:, cols].astype(F32) * sv).astype(o_ref.dtype)


def _gmlp_gate(uv, v_g, v_b, w_s, b_s):
    t, two_w = uv.shape
    width = two_w // 2
    n_groups, chunk, _ = w_s.shape
    gw = width // n_groups
    bias_full = jnp.repeat(b_s.T, gw, axis=1).astype(F32)
    return pl.pallas_call(
        functools.partial(_gmlp_gate_kernel, n_groups=n_groups, gw=gw),
        grid=(t // chunk,),
        in_specs=[
            pl.BlockSpec((chunk, width), lambda i: (i, 0)),
            pl.BlockSpec((chunk, width), lambda i: (i, 1)),
            pl.BlockSpec((1, width), lambda i: (0, 0)),
            pl.BlockSpec((1, width), lambda i: (0, 0)),
            pl.BlockSpec((n_groups, chunk, chunk), lambda i: (0, 0, 0)),
            pl.BlockSpec((chunk, width), lambda i: (0, 0)),
        ],
        out_specs=pl.BlockSpec((chunk, width), lambda i: (i, 0)),
        out_shape=jax.ShapeDtypeStruct((t, width), BF16),
        compiler_params=_params(1),
        name="gmlp_gate",
    )(uv, uv, v_g.reshape(1, width), v_b.reshape(1, width), w_s.astype(BF16), bias_full)


def _na_bias_table(rpb, rows):
    nb = rows // NA_Q_ROWS
    tables = []
    for b in (0, 1, nb - 1):
        r0 = b * NA_Q_ROWS
        kw0 = int(np.clip(r0 - NA_WIN_H // 2, 0, rows - NA_KEY_ROWS))
        r = r0 + np.arange(NA_Q_ROWS)[:, None, None, None]
        c = np.arange(GRID_W)[None, :, None, None]
        kr = kw0 + np.arange(NA_KEY_ROWS)[None, None, :, None]
        kc = np.arange(GRID_W)[None, None, None, :]
        rs = np.clip(r - NA_WIN_H // 2, 0, rows - NA_WIN_H)
        cs = np.clip(c - NA_WIN_W // 2, 0, GRID_W - NA_WIN_W)
        ok = (kr >= rs) & (kr < rs + NA_WIN_H) & (kc >= cs) & (kc < cs + NA_WIN_W)
        dr = np.clip(kr - r + NA_WIN_H - 1, 0, 2 * NA_WIN_H - 2)
        dc = np.clip(kc - c + NA_WIN_W - 1, 0, 2 * NA_WIN_W - 2)
        shape = (NA_Q_ROWS, GRID_W, NA_KEY_ROWS, GRID_W)
        dr = np.broadcast_to(dr, shape).reshape(NA_Q_ROWS * GRID_W, NA_KEY_ROWS * GRID_W)
        dc = np.broadcast_to(dc, shape).reshape(NA_Q_ROWS * GRID_W, NA_KEY_ROWS * GRID_W)
        ok = np.broadcast_to(ok, shape).reshape(NA_Q_ROWS * GRID_W, NA_KEY_ROWS * GRID_W)
        bias = rpb[:, dr, dc]
        tables.append(jnp.where(ok[None], bias, MASK_VALUE))
    return jnp.stack(tables, axis=0).astype(F32)


def _na_kernel(q_ref, k_ref, v_ref, bias_ref, o_ref, *, rows, n_lat, n_ctx):
    b = pl.program_id(1)
    kw0 = jnp.clip(b * NA_Q_ROWS - NA_WIN_H // 2, 0, rows - NA_KEY_ROWS)
    start = pl.multiple_of(kw0 * GRID_W, GRID_W)
    nk = NA_KEY_ROWS * GRID_W
    scale = HEAD_DIM ** -0.5
    q = q_ref[...]
    kw = k_ref[pl.ds(start, nk), :]
    vw = v_ref[pl.ds(start, nk), :]
    kc = k_ref[pl.ds(n_lat, n_ctx), :]
    vc = v_ref[pl.ds(n_lat, n_ctx), :]
    s_loc = _dot_nt(q, kw) * scale + bias_ref[...]
    s_ctx = _dot_nt(q, kc) * scale
    m = jnp.maximum(jnp.max(s_loc, axis=-1, keepdims=True), jnp.max(s_ctx, axis=-1, keepdims=True))
    p_loc = jnp.exp(s_loc - m)
    p_ctx = jnp.exp(s_ctx - m)
    l = jnp.sum(p_loc, axis=-1, keepdims=True) + jnp.sum(p_ctx, axis=-1, keepdims=True)
    o = _dot(p_loc.astype(BF16), vw) + _dot(p_ctx.astype(BF16), vc)
    o_ref[...] = (o / l).astype(o_ref.dtype)


def _na_attention(qkv, bias_table, n_lat, n_ctx):
    t = qkv.shape[0]
    rows = n_lat // GRID_W
    nb = rows // NA_Q_ROWS
    tq = NA_Q_ROWS * GRID_W
    nk = NA_KEY_ROWS * GRID_W
    h = N_HEADS
    return pl.pallas_call(
        functools.partial(_na_kernel, rows=rows, n_lat=n_lat, n_ctx=n_ctx),
        grid=(h, nb),
        in_specs=[
            pl.BlockSpec((tq, HEAD_DIM), lambda hh, b: (b, hh)),
            pl.BlockSpec((t, HEAD_DIM), lambda hh, b: (0, h + hh)),
            pl.BlockSpec((t, HEAD_DIM), lambda hh, b: (0, 2 * h + hh)),
            pl.BlockSpec((None, None, tq, nk),
                         lambda hh, b: (jnp.where(b == 0, 0, jnp.where(b == nb - 1, 2, 1)), hh, 0, 0)),
        ],
        out_specs=pl.BlockSpec((tq, HEAD_DIM), lambda hh, b: (b, hh)),
        out_shape=jax.ShapeDtypeStruct((n_lat, h * HEAD_DIM), BF16),
        compiler_params=_params(2),
        name="neighbourhood_attention",
    )(qkv, qkv, qkv, bias_table)


def _ctx_attn_kernel(q_ref, k_ref, v_ref, o_ref):
    s = _dot_nt(q_ref[...], k_ref[...]) * (HEAD_DIM ** -0.5)
    m = jnp.max(s, axis=-1, keepdims=True)
    p = jnp.exp(s - m)
    l = jnp.sum(p, axis=-1, keepdims=True)
    o_ref[...] = (_dot(p.astype(BF16), v_ref[...]) / l).astype(o_ref.dtype)


def _ctx_attention(qkv, n_lat, n_ctx):
    h = N_HEADS
    rb = n_lat // n_ctx
    return pl.pallas_call(
        _ctx_attn_kernel,
        grid=(h,),
        in_specs=[
            pl.BlockSpec((n_ctx, HEAD_DIM), lambda hh: (rb, hh)),
            pl.BlockSpec((n_ctx, HEAD_DIM), lambda hh: (rb, h + hh)),
            pl.BlockSpec((n_ctx, HEAD_DIM), lambda hh: (rb, 2 * h + hh)),
        ],
        out_specs=pl.BlockSpec((n_ctx, HEAD_DIM), lambda hh: (0, hh)),
        out_shape=jax.ShapeDtypeStruct((n_ctx, h * HEAD_DIM), BF16),
        compiler_params=_params(1),
        name="context_attention",
    )(qkv, qkv, qkv)


def _dft_cos_sin(k_times_n, period):
    ang = (2.0 * math.pi / period) * (k_times_n % period).astype(F32)
    return jnp.cos(ang), -jnp.sin(ang)


def _ft_stage2_kernel(g_ref, y_ref, o_ref, *, nb):
    for i in range(nb):
        rhs = jnp.concatenate([y_ref[0, i], y_ref[1, i]], axis=0)
        res = _dot(g_ref[i], rhs)
        half = res.shape[0] // 2
        o_ref[0, :, i, :] = res[:half].astype(o_ref.dtype)
        o_ref[1, :, i, :] = res[half:].astype(o_ref.dtype)


def _ft_positions_latent(h_lat):
    n, d = h_lat.shape
    l1 = l2 = int(round(math.sqrt(n)))
    assert l1 * l2 == n
    idx = jnp.arange(l1, dtype=jnp.int32)
    re, im = _dft_cos_sin(idx[:, None] * idx[None, :], l1)
    m1 = (jnp.concatenate([re, im], axis=0) * (l1 ** -0.5)).astype(BF16)
    y = _left_matmul(m1, h_lat.reshape(l1, l2 * d), out_dtype=BF16, tn=8192)
    y = y.reshape(2, l1, l2, d)
    k1 = idx[:, None, None]
    k2 = idx[None, :, None]
    n2 = idx[None, None, :]
    gr, gi = _dft_cos_sin(n2 * k2 * l1 + n2 * k1, n)
    g = jnp.concatenate(
        [jnp.concatenate([gr, -gi], axis=2), jnp.concatenate([gi, gr], axis=2)], axis=1
    )
    g = (g * (l2 ** -0.5)).astype(BF16)
    nb, cb = 8, min(1024, d)
    p = pl.pallas_call(
        functools.partial(_ft_stage2_kernel, nb=nb),
        grid=(l1 // nb, d // cb),
        in_specs=[
            pl.BlockSpec((nb, 2 * l2, 2 * l2), lambda a, c: (a, 0, 0)),
            pl.BlockSpec((2, nb, l2, cb), lambda a, c: (0, a, 0, c)),
        ],
        out_specs=pl.BlockSpec((2, l2, nb, cb), lambda a, c: (0, 0, a, c)),
        out_shape=jax.ShapeDtypeStruct((2, l2, l1, d), BF16),
        compiler_params=_params(2),
        name="ft_stage2",
    )(g, y)
    return p.reshape(2, n, d)


def _ft_positions_dense(h_ctx):
    n, d = h_ctx.shape
    idx = jnp.arange(n, dtype=jnp.int32)
    re, im = _dft_cos_sin(idx[:, None] * idx[None, :], n)
    m = (jnp.concatenate([re, im], axis=0) * (n ** -0.5)).astype(BF16)
    return _left_matmul(m, h_ctx, out_dtype=BF16, tn=min(d, 2048)).reshape(2, n, d)


def _ft_channel_kernel(pr_ref, pi_ref, c_ref, s_ref, o_ref):
    o_ref[...] = (_dot(pr_ref[...], c_ref[...]) + _dot(pi_ref[...], s_ref[...])).astype(o_ref.dtype)


def _ft_channels(p):
    _, t, d = p.shape
    gc = d // FT_GROUPS
    idx = jnp.arange(gc, dtype=jnp.int32)
    re, im = _dft_cos_sin(idx[:, None] * idx[None, :], gc)
    cc = (re * (gc ** -0.5)).astype(BF16)
    sc = (-im * (gc ** -0.5)).astype(BF16)
    tm = MM_TILE_M if t % MM_TILE_M == 0 else t
    return pl.pallas_call(
        _ft_channel_kernel,
        grid=(t // tm, FT_GROUPS),
        in_specs=[
            pl.BlockSpec((None, tm, gc), lambda i, g: (0, i, g)),
            pl.BlockSpec((None, tm, gc), lambda i, g: (1, i, g)),
            pl.BlockSpec((gc, gc), lambda i, g: (0, 0)),
            pl.BlockSpec((gc, gc), lambda i, g: (0, 0)),
        ],
        out_specs=pl.BlockSpec((tm, gc), lambda i, g: (i, g)),
        out_shape=jax.ShapeDtypeStruct((t, d), BF16),
        compiler_params=_params(2),
        name="ft_channels",
    )(p, p, cc, sc)


def _rope_tables(n_lat, n_ctx):
    half = HEAD_DIM // 2
    t = jnp.arange(n_lat)
    inv = ROPE_THETA ** (-jnp.arange(0, half, 2, dtype=F32) / half)
    ang_r = (t // GRID_W).astype(F32)[:, None] * inv[None, :]
    ang_c = (t % GRID_W).astype(F32)[:, None] * inv[None, :]
    cos = jnp.concatenate([jnp.cos(ang_r)] * 2 + [jnp.cos(ang_c)] * 2, axis=-1)
    sin = jnp.concatenate([-jnp.sin(ang_r), jnp.sin(ang_r), -jnp.sin(ang_c), jnp.sin(ang_c)], axis=-1)
    cos = jnp.concatenate([cos, jnp.ones((n_ctx, HEAD_DIM), F32)], axis=0)
    sin = jnp.concatenate([sin, jnp.zeros((n_ctx, HEAD_DIM), F32)], axis=0)
    return cos, sin


def _norm_rope_kernel(x_ref, g_ref, cos_ref, sin_ref, o_ref, *, n_heads, scale):
    quarter = HEAD_DIM // 4
    lane = lax.broadcasted_iota(jnp.int32, (x_ref.shape[0], HEAD_DIM), 1)
    first = (lane % (2 * quarter)) < quarter
    cos = cos_ref[...]
    sin = sin_ref[...]
    g = g_ref[...]
    for hh in range(n_heads):
        cols = slice(hh * HEAD_DIM, (hh + 1) * HEAD_DIM)
        x = x_ref[:, cols].astype(F32)
        xn = x * lax.rsqrt(jnp.mean(x * x, axis=-1, keepdims=True) + RMS_EPS) * g
        partner = jnp.where(first, pltpu.roll(xn, HEAD_DIM - quarter, 1), pltpu.roll(xn, quarter, 1))
        o_ref[:, cols] = ((xn * cos + partner * sin) * scale).astype(o_ref.dtype)


def _norm_rope(qkv, col_block, n_heads, gain, cos, sin, scale):
    t = qkv.shape[0]
    tm = ROW_TILE
    w = n_heads * HEAD_DIM
    return pl.pallas_call(
        functools.partial(_norm_rope_kernel, n_heads=n_heads, scale=scale),
        grid=(t // tm,),
        in_specs=[
            pl.BlockSpec((tm, w), lambda i: (i, col_block)),
            pl.BlockSpec((1, HEAD_DIM), lambda i: (0, 0)),
            pl.BlockSpec((tm, HEAD_DIM), lambda i: (i, 0)),
            pl.BlockSpec((tm, HEAD_DIM), lambda i: (i, 0)),
        ],
        out_specs=pl.BlockSpec((tm, w), lambda i: (i, 0)),
        out_shape=jax.ShapeDtypeStruct((t, w), BF16),
        compiler_params=_params(1),
        name="norm_rope",
    )(qkv, gain.reshape(1, HEAD_DIM), cos, sin)


def _flash_kernel(q_ref, k_ref, v_ref, o_ref, m_sc, l_sc, acc_sc, *, rep, tq, tk, n_chunks):
    q = jnp.concatenate([q_ref[:, r * HEAD_DIM:(r + 1) * HEAD_DIM] for r in range(rep)], axis=0)
    m_sc[...] = jnp.full(m_sc.shape, -jnp.inf, F32)
    l_sc[...] = jnp.zeros(l_sc.shape, F32)
    acc_sc[...] = jnp.zeros(acc_sc.shape, F32)

    def body(c, carry):
        start = pl.multiple_of(c * tk, tk)
        ks = k_ref[pl.ds(start, tk), :]
        vs = v_ref[pl.ds(start, tk), :]
        s = _dot_nt(q, ks)
        m_prev = m_sc[...]
        m_new = jnp.maximum(m_prev, jnp.max(s, axis=-1, keepdims=True))
        alpha = jnp.exp(m_prev - m_new)
        p = jnp.exp(s - m_new)
        l_sc[...] = alpha * l_sc[...] + jnp.sum(p, axis=-1, keepdims=True)
        acc_sc[...] = alpha * acc_sc[...] + _dot(p.astype(BF16), vs)
        m_sc[...] = m_new
        return carry

    lax.fori_loop(0, n_chunks, body, 0)
    o = acc_sc[...] / l_sc[...]
    for r in range(rep):
        o_ref[:, r * HEAD_DIM:(r + 1) * HEAD_DIM] = o[r * tq:(r + 1) * tq].astype(o_ref.dtype)


def _gqa_attention(q, k, qkv, v_col_block0, n_q):
    t = k.shape[0]
    rep = N_HEADS // N_KV_HEADS
    tq, tk = GQA_TQ, GQA_TK
    assert n_q % tq == 0 and t % tk == 0
    w = rep * HEAD_DIM
    return pl.pallas_call(
        functools.partial(_flash_kernel, rep=rep, tq=tq, tk=tk, n_chunks=t // tk),
        grid=(N_KV_HEADS, n_q // tq),
        in_specs=[
            pl.BlockSpec((tq, w), lambda g, i: (i, g)),
            pl.BlockSpec((t, HEAD_DIM), lambda g, i: (0, g)),
            pl.BlockSpec((t, HEAD_DIM), lambda g, i: (0, v_col_block0 + g)),
        ],
        out_specs=pl.BlockSpec((tq, w), lambda g, i: (i, g)),
        out_shape=jax.ShapeDtypeStruct((n_q, N_HEADS * HEAD_DIM), BF16),
        scratch_shapes=[
            pltpu.VMEM((rep * tq, 1), F32),
            pltpu.VMEM((rep * tq, 1), F32),
            pltpu.VMEM((rep * tq, HEAD_DIM), F32),
        ],
        compiler_params=_params(2),
        name="gqa_flash_attention",
    )(q, k, qkv)


def _route(logits, router_bias, block_rows):
    n_tok, n_e = logits.shape
    scores = jax.nn.sigmoid(logits)
    biased = scores + router_bias.astype(F32)
    per_g = n_e // N_EXPERT_GROUPS
    g_score = lax.top_k(biased.reshape(n_tok, N_EXPERT_GROUPS, per_g), 2)[0].sum(-1)
    _, g_idx = lax.top_k(g_score, TOP_K_GROUPS)
    g_mask = jnp.sum(jax.nn.one_hot(g_idx, N_EXPERT_GROUPS, dtype=F32), axis=1)
    e_mask = jnp.repeat(g_mask, per_g, axis=1) > 0
    _, e_idx = lax.top_k(jnp.where(e_mask, biased, -jnp.inf), TOP_K)
    w = jnp.take_along_axis(scores, e_idx, axis=1)
    w = w / jnp.sum(w, axis=-1, keepdims=True) * ROUTED_SCALE

    n_asg = n_tok * TOP_K
    flat_e = e_idx.reshape(-1).astype(jnp.int32)
    order = jnp.argsort(flat_e, stable=True).astype(jnp.int32)
    counts = jnp.sum((flat_e[:, None] == jnp.arange(n_e, dtype=jnp.int32)[None, :]).astype(jnp.int32), axis=0)
    padded = (counts + block_rows - 1) // block_rows * block_rows
    pad_end = jnp.cumsum(padded)
    pad_start = pad_end - padded
    cum_start = jnp.cumsum(counts) - counts
    n_blocks = -(-(n_asg + n_e * (block_rows - 1)) // block_rows)
    blk_row0 = jnp.arange(n_blocks, dtype=jnp.int32) * block_rows
    block_e = jnp.minimum(jnp.searchsorted(pad_end, blk_row0, side="right"), n_e - 1).astype(jnp.int32)
    n_valid = jnp.clip(counts[block_e] - (blk_row0 - pad_start[block_e]), 0, block_rows).astype(jnp.int32)
    row = jnp.arange(n_blocks * block_rows, dtype=jnp.int32)
    row_e = jnp.repeat(block_e, block_rows)
    q = row - pad_start[row_e]
    asg = order[jnp.clip(cum_start[row_e] + q, 0, n_asg - 1)]
    valid = q < counts[row_e]
    tok = jnp.where(valid, asg // TOP_K, 0)
    dst = jnp.where(valid, (asg % TOP_K) * n_tok + tok, 0)
    rows = jnp.stack([tok.reshape(n_blocks, block_rows), dst.reshape(n_blocks, block_rows)], axis=1)
    return w, block_e, n_valid, rows.astype(jnp.int32)


def _swiglu_rows(x, wgu_ref, wd_ref):
    hgu = _dot(x, wgu_ref[...])
    de = hgu.shape[1] // 2
    gate = hgu[:, :de]
    hb = (gate * jax.nn.sigmoid(gate) * hgu[:, de:]).astype(BF16)
    return _dot(hb, wd_ref[...])


def _expert_kernel(be_ref, nv_ref, rows_hbm, t_hbm, wgu_ref, wd_ref, y_hbm,
                   rows_smem, xbuf, ybuf, sem_rows, sem_in, sem_out):
    i = pl.program_id(0)
    nv = nv_ref[i]

    @pl.when(i == 0)
    def _():
        xbuf[...] = jnp.zeros(xbuf.shape, xbuf.dtype)

    @pl.when(nv > 0)
    def _():
        cp = pltpu.make_async_copy(rows_hbm.at[i], rows_smem, sem_rows)
        cp.start()
        cp.wait()

        def gather_start(r, c):
            tok = rows_smem[0, r]
            pltpu.make_async_copy(t_hbm.at[pl.ds(tok, 1)], xbuf.at[pl.ds(r, 1)], sem_in).start()
            return c

        def gather_wait(r, c):
            pltpu.make_async_copy(t_hbm.at[pl.ds(0, 1)], xbuf.at[pl.ds(0, 1)], sem_in).wait()
            return c

        lax.fori_loop(0, nv, gather_start, 0)
        lax.fori_loop(0, nv, gather_wait, 0)
        ybuf[...] = _swiglu_rows(xbuf[...].astype(BF16), wgu_ref, wd_ref)

        def scatter_start(r, c):
            dst = rows_smem[1, r]
            pltpu.make_async_copy(ybuf.at[pl.ds(r, 1)], y_hbm.at[pl.ds(dst, 1)], sem_out).start()
            return c

        def scatter_wait(r, c):
            pltpu.make_async_copy(ybuf.at[pl.ds(0, 1)], y_hbm.at[pl.ds(0, 1)], sem_out).wait()
            return c

        lax.fori_loop(0, nv, scatter_start, 0)
        lax.fori_loop(0, nv, scatter_wait, 0)


def _routed_experts(tok_f32, block_e, n_valid, rows, w_gu, w_down):
    n_tok, d = tok_f32.shape
    n_blocks, _, br = rows.shape
    de2 = w_gu.shape[2]
    grid_spec = pltpu.PrefetchScalarGridSpec(
        num_scalar_prefetch=2,
        grid=(n_blocks,),
        in_specs=[
            pl.BlockSpec(memory_space=pl.ANY),
            pl.BlockSpec(memory_space=pl.ANY),
            pl.BlockSpec((None, d, de2), lambda i, be, nv: (be[i], 0, 0)),
            pl.BlockSpec((None, de2 // 2, d), lambda i, be, nv: (be[i], 0, 0)),
        ],
        out_specs=pl.BlockSpec(memory_space=pl.ANY),
        scratch_shapes=[
            pltpu.SMEM((2, br), jnp.int32),
            pltpu.VMEM((br, d), F32),
            pltpu.VMEM((br, d), F32),
            pltpu.SemaphoreType.DMA(()),
            pltpu.SemaphoreType.DMA(()),
            pltpu.SemaphoreType.DMA(()),
        ],
    )
    return pl.pallas_call(
        _expert_kernel,
        grid_spec=grid_spec,
        out_shape=jax.ShapeDtypeStruct((TOP_K * n_tok, d), F32),
        compiler_params=_params(1),
        name="routed_experts",
    )(block_e, n_valid, rows, tok_f32, w_gu, w_down)


def _shared_kernel(x_ref, wgu_ref, wd_ref, o_ref):
    o_ref[...] = _swiglu_rows(x_ref[...].astype(BF16), wgu_ref, wd_ref).astype(o_ref.dtype)


def _shared_expert(tok_f32, w_gu, w_down):
    t, d = tok_f32.shape
    tm = ROW_TILE
    de2 = w_gu.shape[1]
    return pl.pallas_call(
        _shared_kernel,
        grid=(t // tm,),
        in_specs=[
            pl.BlockSpec((tm, d), lambda i: (i, 0)),
            pl.BlockSpec((d, de2), lambda i: (0, 0)),
            pl.BlockSpec((de2 // 2, d), lambda i: (0, 0)),
        ],
        out_specs=pl.BlockSpec((tm, d), lambda i: (i, 0)),
        out_shape=jax.ShapeDtypeStruct((t, d), F32),
        compiler_params=_params(1),
        name="shared_expert",
    )(tok_f32, w_gu, w_down)


def _combine_kernel(*refs, with_h):
    x_ref, sh_ref, y0_ref, y1_ref, y2_ref, y3_ref, w_ref, gate_ref, g_ref, b_ref = refs[:10]
    w = w_ref[...]
    y = sh_ref[...]
    for k, y_ref in enumerate((y0_ref, y1_ref, y2_ref, y3_ref)):
        y = y + w[:, k:k + 1] * y_ref[...]
    z = DEEPNORM_ALPHA * x_ref[...] + gate_ref[...] * y
    xn = _layer_norm_rows(z, g_ref[...], b_ref[...])
    if with_h:
        sc_ref, shf_ref, xo_ref, h_ref = refs[10:14]
        xo_ref[...] = xn
        h_ref[...] = (xn * (1.0 + sc_ref[...]) + shf_ref[...]).astype(h_ref.dtype)
    else:
        xo_ref = refs[10]
        xo_ref[...] = xn


def _moe_combine(x, shared, y_slots, w, mod, gate_slot, ln_g, ln_b, n_lat, *, next_mod=None,
                 n_out_rows=None):
    t, d = x.shape
    n_out_rows = t if n_out_rows is None else n_out_rows
    tm = COMBINE_ROWS
    nlb = n_lat // tm
    with_h = next_mod is not None
    y3 = y_slots.reshape(TOP_K, t, d)
    row = pl.BlockSpec((tm, d), lambda i: (i, 0))
    vec = pl.BlockSpec((1, d), lambda i: (0, 0))

    def slot_spec(k):
        return pl.BlockSpec((None, tm, d), lambda i: (k, i, 0))

    in_specs = [row, row] + [slot_spec(k) for k in range(TOP_K)] + [
        pl.BlockSpec((tm, TOP_K), lambda i: (i, 0)),
        _mod_spec(gate_slot, nlb, d), vec, vec,
    ]
    args = [x, shared, y3, y3, y3, y3, w, mod, ln_g.reshape(1, d), ln_b.reshape(1, d)]
    out_specs = [row]
    out_shape = [jax.ShapeDtypeStruct((n_out_rows, d), F32)]
    if with_h:
        in_specs += [_mod_spec(1, nlb, d), _mod_spec(0, nlb, d)]
        args += [next_mod, next_mod]
        out_specs.append(row)
        out_shape.append(jax.ShapeDtypeStruct((n_out_rows, d), BF16))
    return pl.pallas_call(
        functools.partial(_combine_kernel, with_h=with_h),
        grid=(n_out_rows // tm,),
        in_specs=in_specs,
        out_specs=out_specs,
        out_shape=out_shape,
        compiler_params=_params(1),
        name="moe_combine",
    )(*args)


def kernel(x, c, ctx, c_ctx, w_mod_down, w_mod_up, b_mod, ln_g, ln_b, gm_w_in, gm_v_g, gm_v_b, gm_w_s, gm_b_s, gm_w_out, na_w_qkv, na_rpb, na_w_out, ft_w_out, ga_w_qkv, ga_q_g, ga_k_g, ga_w_out, router_w, router_bias, exp_w_gate, exp_w_up, exp_w_down, sh_w_gate, sh_w_up, sh_w_down):
    b, n, d = x.shape
    nc = ctx.shape[1]
    assert b == 1 and c.shape[0] == 1
    n_tok = n + nc
    rows = n // GRID_W

    xs = jnp.concatenate([x[0], ctx[0]], axis=0)
    cond8 = jnp.zeros((8, d), F32).at[0].set(c[0]).at[1].set(c_ctx)
    mods = _ada_mod_all(cond8, w_mod_down, w_mod_up, b_mod)
    router_w_pad = jnp.pad(router_w, ((0, 0), (0, 0), (0, LANES - N_EXPERTS)))

    h = _modulate(xs, mods[0], 1, 0, n, BF16)
    for i in range(DEPTH):
        kind = i % 4
        mod = mods[i]
        last = i == DEPTH - 1

        if kind == 0:
            uv = _matmul(h, gm_w_in[0].astype(BF16), out_dtype=BF16, act="gelu")
            gated = _gmlp_gate(uv, gm_v_g[0], gm_v_b[0], gm_w_s[0], gm_b_s[0])
            o = _matmul(gated, gm_w_out[0].astype(BF16), out_dtype=F32)
        elif kind == 1:
            qkv = _matmul(h, na_w_qkv[0].astype(BF16), out_dtype=BF16)
            a_lat = _na_attention(qkv, _na_bias_table(na_rpb[0], rows), n, nc)
            a_ctx = _ctx_attention(qkv, n, nc)
            o = _matmul(jnp.concatenate([a_lat, a_ctx], axis=0), na_w_out[0].astype(BF16), out_dtype=F32)
        elif kind == 2:
            p = jnp.concatenate([_ft_positions_latent(h[:n]), _ft_positions_dense(h[n:])], axis=1)
            o = _matmul(_ft_channels(p), ft_w_out[0].astype(BF16), out_dtype=F32)
        else:
            qkv = _matmul(h, ga_w_qkv[0].astype(BF16), out_dtype=BF16, tn=1024)
            cos, sin = _rope_tables(n, nc)
            qn = _norm_rope(qkv, 0, N_HEADS, ga_q_g[0], cos, sin, HEAD_DIM ** -0.5)
            kn = _norm_rope(qkv, N_HEADS // N_KV_HEADS, N_KV_HEADS, ga_k_g[0], cos, sin, 1.0)
            att = _gqa_attention(qn, kn, qkv, (N_HEADS + N_KV_HEADS), n)
            att = jnp.concatenate([att, jnp.zeros((nc, d), BF16)], axis=0)
            o = _matmul(att, ga_w_out[0].astype(BF16), out_dtype=F32)

        xs, h2, logits = _resln(xs, o, mod, 2, ln_g[i, 0], ln_b[i, 0], n, next_mod=mod,
                                scale_slot=4, shift_slot=3, h_dtype=F32, router_w=router_w_pad[i])

        w, block_e, n_valid, plan = _route(logits[:, :N_EXPERTS], router_bias[i], MOE_ROWS)
        w_gu = jnp.concatenate([exp_w_gate[i], exp_w_up[i]], axis=-1).astype(BF16)
        y_slots = _routed_experts(h2, block_e, n_valid, plan, w_gu, exp_w_down[i].astype(BF16))
        s_gu = jnp.concatenate([sh_w_gate[i], sh_w_up[i]], axis=-1).astype(BF16)
        shared = _shared_expert(h2, s_gu, sh_w_down[i].astype(BF16))
        if last:
            (xs,) = _moe_combine(xs, shared, y_slots, w, mod, 5, ln_g[i, 1], ln_b[i, 1], n,
                                 n_out_rows=n)
        else:
            xs, h = _moe_combine(xs, shared, y_slots, w, mod, 5, ln_g[i, 1], ln_b[i, 1], n,
                                 next_mod=mods[i + 1])
    return xs.reshape(b, n, d)
```

```python
import functools
import math

import jax
import jax.numpy as jnp
import numpy as np
from jax import lax
from jax.experimental import pallas as pl
from jax.experimental.pallas import tpu as pltpu

F32 = jnp.float32
BF16 = jnp.bfloat16

D_MODEL = 4096
DEPTH = 4
GRID_W = 64
N_MOD = 6
LN_EPS = 1e-5
RMS_EPS = 1e-6
DEEPNORM_ALPHA = (2 * DEPTH) ** 0.25
GM_CHUNK = 128
GM_GROUPS = 16
N_HEADS = 32
HEAD_DIM = 128
N_KV_HEADS = 8
ROPE_THETA = 10000.0
NA_WIN_H = 8
NA_WIN_W = 16
FT_GROUPS = 8
N_EXPERTS = 32
N_EXPERT_GROUPS = 4
TOP_K_GROUPS = 2
TOP_K = 4
D_EXPERT = 384
ROUTED_SCALE = 2.5

LANES = 128
VMEM_LIMIT_BYTES = 56 * 1024 * 1024
ROW_TILE = 256
MM_TILE_M = 1280
MM_TILE_N = 1024
MOE_ROWS = 256
COMBINE_ROWS = 128
NA_Q_ROWS = 4
NA_KEY_ROWS = 12
GQA_TQ = 256
GQA_TK = 512
MASK_VALUE = -1e30


def _params(n_axes):
    return pltpu.CompilerParams(
        dimension_semantics=("arbitrary",) * n_axes,
        vmem_limit_bytes=VMEM_LIMIT_BYTES,
    )


def _dot(a, b, precision=None):
    return jnp.dot(a, b, preferred_element_type=F32, precision=precision)


def _dot_nt(a, b):
    return lax.dot_general(a, b, (((1,), (1,)), ((), ())), preferred_element_type=F32)


def _mm_kernel(a_ref, b_ref, o_ref, *, act):
    acc = _dot(a_ref[...], b_ref[...])
    if act == "gelu":
        acc = jax.nn.gelu(acc)
    o_ref[...] = acc.astype(o_ref.dtype)


def _matmul(a, b, *, out_dtype, act=None, tm=MM_TILE_M, tn=MM_TILE_N):
    m, k = a.shape
    n = b.shape[1]
    tm = min(tm, m)
    tn = min(tn, n)
    if jnp.dtype(out_dtype).itemsize == 4:
        tn = min(tn, MM_TILE_N // 2)
    while n % tn:
        tn //= 2
    assert m % tm == 0 and tn % LANES == 0
    return pl.pallas_call(
        functools.partial(_mm_kernel, act=act),
        grid=(m // tm, n // tn),
        in_specs=[
            pl.BlockSpec((tm, k), lambda i, j: (i, 0)),
            pl.BlockSpec((k, tn), lambda i, j: (0, j)),
        ],
        out_specs=pl.BlockSpec((tm, tn), lambda i, j: (i, j)),
        out_shape=jax.ShapeDtypeStruct((m, n), out_dtype),
        compiler_params=_params(2),
        name="matmul",
    )(a, b)


def _lmm_kernel(m_ref, x_ref, o_ref):
    o_ref[...] = _dot(m_ref[...], x_ref[...]).astype(o_ref.dtype)


def _left_matmul(mat, x, *, out_dtype, tn):
    r, k = mat.shape
    n = x.shape[1]
    assert n % tn == 0
    return pl.pallas_call(
        _lmm_kernel,
        grid=(n // tn,),
        in_specs=[
            pl.BlockSpec((r, k), lambda j: (0, 0)),
            pl.BlockSpec((k, tn), lambda j: (0, j)),
        ],
        out_specs=pl.BlockSpec((r, tn), lambda j: (0, j)),
        out_shape=jax.ShapeDtypeStruct((r, n), out_dtype),
        compiler_params=_params(1),
        name="left_matmul",
    )(mat, x)


def _mod_kernel(cond_ref, wd_ref, wu_ref, b_ref, o_ref):
    cond = cond_ref[...]
    t = _dot(cond * jax.nn.sigmoid(cond), wd_ref[...], precision=lax.Precision.HIGHEST)
    o_ref[...] = _dot(t, wu_ref[...], precision=lax.Precision.HIGHEST) + b_ref[...]


def _ada_mod_all(cond8, w_mod_down, w_mod_up, b_mod):
    depth, d, rank = w_mod_down.shape
    out = pl.pallas_call(
        _mod_kernel,
        grid=(depth, N_MOD),
        in_specs=[
            pl.BlockSpec((8, d), lambda l, j: (0, 0)),
            pl.BlockSpec((None, d, rank), lambda l, j: (l, 0, 0)),
            pl.BlockSpec((None, rank, d), lambda l, j: (l, 0, j)),
            pl.BlockSpec((None, None, 1, d), lambda l, j: (l, j, 0, 0)),
        ],
        out_specs=pl.BlockSpec((None, None, 8, d), lambda l, j: (l, j, 0, 0)),
        out_shape=jax.ShapeDtypeStruct((depth, N_MOD, 8, d), F32),
        compiler_params=_params(2),
        name="ada_mod",
    )(cond8, w_mod_down, w_mod_up, b_mod.reshape(depth, N_MOD, 1, d))
    return jnp.transpose(out[:, :, :2], (0, 2, 1, 3))[:, :, :, None, :]


def _mod_spec(slot, n_lat_blocks, d):
    return pl.BlockSpec((None, None, 1, d), lambda i: (i // n_lat_blocks, slot, 0, 0))


def _modulate_kernel(x_ref, sc_ref, sh_ref, h_ref):
    h_ref[...] = (x_ref[...] * (1.0 + sc_ref[...]) + sh_ref[...]).astype(h_ref.dtype)


def _modulate(x, mod, scale_slot, shift_slot, n_lat, out_dtype):
    t, d = x.shape
    tm = ROW_TILE
    nlb = n_lat // tm
    return pl.pallas_call(
        _modulate_kernel,
        grid=(t // tm,),
        in_specs=[
            pl.BlockSpec((tm, d), lambda i: (i, 0)),
            _mod_spec(scale_slot, nlb, d),
            _mod_spec(shift_slot, nlb, d),
        ],
        out_specs=pl.BlockSpec((tm, d), lambda i: (i, 0)),
        out_shape=jax.ShapeDtypeStruct((t, d), out_dtype),
        compiler_params=_params(1),
        name="modulate",
    )(x, mod, mod)


def _layer_norm_rows(z, g, b):
    mu = jnp.mean(z, axis=-1, keepdims=True)
    zc = z - mu
    var = jnp.mean(zc * zc, axis=-1, keepdims=True)
    return zc * lax.rsqrt(var + LN_EPS) * g + b


def _resln_kernel(*refs, with_h, with_router):
    x_ref, o_ref, gate_ref, g_ref, b_ref = refs[:5]
    pos = 5
    if with_h:
        sc_ref, sh_ref = refs[pos:pos + 2]
        pos += 2
    if with_router:
        rw_ref = refs[pos]
        pos += 1
    xo_ref = refs[pos]
    pos += 1
    z = DEEPNORM_ALPHA * x_ref[...] + gate_ref[...] * o_ref[...].astype(F32)
    xn = _layer_norm_rows(z, g_ref[...], b_ref[...])
    xo_ref[...] = xn
    if with_h:
        h_ref = refs[pos]
        pos += 1
        h = xn * (1.0 + sc_ref[...]) + sh_ref[...]
        h_ref[...] = h.astype(h_ref.dtype)
        if with_router:
            lg_ref = refs[pos]
            lg_ref[...] = _dot(h, rw_ref[...], precision=lax.Precision.HIGHEST)


def _resln(x, o, mod, gate_slot, ln_g, ln_b, n_lat, *, next_mod=None, scale_slot=None,
           shift_slot=None, h_dtype=None, router_w=None, n_out_rows=None):
    t, d = x.shape
    n_out_rows = t if n_out_rows is None else n_out_rows
    tm = ROW_TILE
    nlb = n_lat // tm
    with_h = next_mod is not None
    with_router = router_w is not None
    row = pl.BlockSpec((tm, d), lambda i: (i, 0))
    vec = pl.BlockSpec((1, d), lambda i: (0, 0))
    in_specs = [row, row, _mod_spec(gate_slot, nlb, d), vec, vec]
    args = [x, o, mod, ln_g.reshape(1, d), ln_b.reshape(1, d)]
    out_specs = [row]
    out_shape = [jax.ShapeDtypeStruct((n_out_rows, d), F32)]
    if with_h:
        in_specs += [_mod_spec(scale_slot, nlb, d), _mod_spec(shift_slot, nlb, d)]
        args += [next_mod, next_mod]
        out_specs.append(row)
        out_shape.append(jax.ShapeDtypeStruct((n_out_rows, d), h_dtype))
    if with_router:
        ne = router_w.shape[1]
        in_specs.append(pl.BlockSpec((d, ne), lambda i: (0, 0)))
        args.append(router_w)
        out_specs.append(pl.BlockSpec((tm, ne), lambda i: (i, 0)))
        out_shape.append(jax.ShapeDtypeStruct((n_out_rows, ne), F32))
    return pl.pallas_call(
        functools.partial(_resln_kernel, with_h=with_h, with_router=with_router),
        grid=(n_out_rows // tm,),
        in_specs=in_specs,
        out_specs=out_specs,
        out_shape=out_shape,
        compiler_params=_params(1),
        name="residual_layernorm",
    )(*args)


def _gmlp_gate_kernel(u_ref, v_ref, vg_ref, vb_ref, ws_ref, bs_ref, o_ref, *, n_groups, gw):
    vn = _layer_norm_rows(v_ref[...].astype(F32), vg_ref[...], vb_ref[...]).astype(BF16)
    for g in range(n_groups):
        cols = slice(g * gw, (g + 1) * gw)
        sv = _dot(ws_ref[g], vn[:, cols]) + bs_ref[:, cols]
        o_ref[:, cols] = (u_ref[:, cols].astype(F32) * sv).astype(o_ref.dtype)


def _gmlp_gate(uv, v_g, v_b, w_s, b_s):
    t, two_w = uv.shape
    width = two_w // 2
    n_groups, chunk, _ = w_s.shape
    gw = width // n_groups
    bias_full = jnp.repeat(b_s.T, gw, axis=1).astype(F32)
    return pl.pallas_call(
        functools.partial(_gmlp_gate_kernel, n_groups=n_groups, gw=gw),
        grid=(t // chunk,),
        in_specs=[
            pl.BlockSpec((chunk, width), lambda i: (i, 0)),
            pl.BlockSpec((chunk, width), lambda i: (i, 1)),
            pl.BlockSpec((1, width), lambda i: (0, 0)),
            pl.BlockSpec((1, width), lambda i: (0, 0)),
            pl.BlockSpec((n_groups, chunk, chunk), lambda i: (0, 0, 0)),
            pl.BlockSpec((chunk, width), lambda i: (0, 0)),
        ],
        out_specs=pl.BlockSpec((chunk, width), lambda i: (i, 0)),
        out_shape=jax.ShapeDtypeStruct((t, width), BF16),
        compiler_params=_params(1),
        name="gmlp_gate",
    )(uv, uv, v_g.reshape(1, width), v_b.reshape(1, width), w_s.astype(BF16), bias_full)


def _na_bias_table(rpb, rows):
    nb = rows // NA_Q_ROWS
    tables = []
    for b in (0, 1, nb - 1):
        r0 = b * NA_Q_ROWS
        kw0 = int(np.clip(r0 - NA_WIN_H // 2, 0, rows - NA_KEY_ROWS))
        r = r0 + np.arange(NA_Q_ROWS)[:, None, None, None]
        c = np.arange(GRID_W)[None, :, None, None]
        kr = kw0 + np.arange(NA_KEY_ROWS)[None, None, :, None]
        kc = np.arange(GRID_W)[None, None, None, :]
        rs = np.clip(r - NA_WIN_H // 2, 0, rows - NA_WIN_H)
        cs = np.clip(c - NA_WIN_W // 2, 0, GRID_W - NA_WIN_W)
        ok = (kr >= rs) & (kr < rs + NA_WIN_H) & (kc >= cs) & (kc < cs + NA_WIN_W)
        dr = np.clip(kr - r + NA_WIN_H - 1, 0, 2 * NA_WIN_H - 2)
        dc = np.clip(kc - c + NA_WIN_W - 1, 0, 2 * NA_WIN_W - 2)
        shape = (NA_Q_ROWS, GRID_W, NA_KEY_ROWS, GRID_W)
        dr = np.broadcast_to(dr, shape).reshape(NA_Q_ROWS * GRID_W, NA_KEY_ROWS * GRID_W)
        dc = np.broadcast_to(dc, shape).reshape(NA_Q_ROWS * GRID_W, NA_KEY_ROWS * GRID_W)
        ok = np.broadcast_to(ok, shape).reshape(NA_Q_ROWS * GRID_W, NA_KEY_ROWS * GRID_W)
        bias = rpb[:, dr, dc]
        tables.append(jnp.where(ok[None], bias, MASK_VALUE))
    return jnp.stack(tables, axis=0).astype(F32)


def _na_kernel(q_ref, k_ref, v_ref, bias_ref, o_ref, *, rows, n_lat, n_ctx):
    b = pl.program_id(1)
    kw0 = jnp.clip(b * NA_Q_ROWS - NA_WIN_H // 2, 0, rows - NA_KEY_ROWS)
    start = pl.multiple_of(kw0 * GRID_W, GRID_W)
    nk = NA_KEY_ROWS * GRID_W
    scale = HEAD_DIM ** -0.5
    q = q_ref[...]
    kw = k_ref[pl.ds(start, nk), :]
    vw = v_ref[pl.ds(start, nk), :]
    kc = k_ref[pl.ds(n_lat, n_ctx), :]
    vc = v_ref[pl.ds(n_lat, n_ctx), :]
    s_loc = _dot_nt(q, kw) * scale + bias_ref[...]
    s_ctx = _dot_nt(q, kc) * scale
    m = jnp.maximum(jnp.max(s_loc, axis=-1, keepdims=True), jnp.max(s_ctx, axis=-1, keepdims=True))
    p_loc = jnp.exp(s_loc - m)
    p_ctx = jnp.exp(s_ctx - m)
    l = jnp.sum(p_loc, axis=-1, keepdims=True) + jnp.sum(p_ctx, axis=-1, keepdims=True)
    o = _dot(p_loc.astype(BF16), vw) + _dot(p_ctx.astype(BF16), vc)
    o_ref[...] = (o / l).astype(o_ref.dtype)


def _na_attention(qkv, bias_table, n_lat, n_ctx):
    t = qkv.shape[0]
    rows = n_lat // GRID_W
    nb = rows // NA_Q_ROWS
    tq = NA_Q_ROWS * GRID_W
    nk = NA_KEY_ROWS * GRID_W
    h = N_HEADS
    return pl.pallas_call(
        functools.partial(_na_kernel, rows=rows, n_lat=n_lat, n_ctx=n_ctx),
        grid=(h, nb),
        in_specs=[
            pl.BlockSpec((tq, HEAD_DIM), lambda hh, b: (b, hh)),
            pl.BlockSpec((t, HEAD_DIM), lambda hh, b: (0, h + hh)),
            pl.BlockSpec((t, HEAD_DIM), lambda hh, b: (0, 2 * h + hh)),
            pl.BlockSpec((None, None, tq, nk),
                         lambda hh, b: (jnp.where(b == 0, 0, jnp.where(b == nb - 1, 2, 1)), hh, 0, 0)),
        ],
        out_specs=pl.BlockSpec((tq, HEAD_DIM), lambda hh, b: (b, hh)),
        out_shape=jax.ShapeDtypeStruct((n_lat, h * HEAD_DIM), BF16),
        compiler_params=_params(2),
        name="neighbourhood_attention",
    )(qkv, qkv, qkv, bias_table)


def _ctx_attn_kernel(q_ref, k_ref, v_ref, o_ref):
    s = _dot_nt(q_ref[...], k_ref[...]) * (HEAD_DIM ** -0.5)
    m = jnp.max(s, axis=-1, keepdims=True)
    p = jnp.exp(s - m)
    l = jnp.sum(p, axis=-1, keepdims=True)
    o_ref[...] = (_dot(p.astype(BF16), v_ref[...]) / l).astype(o_ref.dtype)


def _ctx_attention(qkv, n_lat, n_ctx):
    h = N_HEADS
    rb = n_lat // n_ctx
    return pl.pallas_call(
        _ctx_attn_kernel,
        grid=(h,),
        in_specs=[
            pl.BlockSpec((n_ctx, HEAD_DIM), lambda hh: (rb, hh)),
            pl.BlockSpec((n_ctx, HEAD_DIM), lambda hh: (rb, h + hh)),
            pl.BlockSpec((n_ctx, HEAD_DIM), lambda hh: (rb, 2 * h + hh)),
        ],
        out_specs=pl.BlockSpec((n_ctx, HEAD_DIM), lambda hh: (0, hh)),
        out_shape=jax.ShapeDtypeStruct((n_ctx, h * HEAD_DIM), BF16),
        compiler_params=_params(1),
        name="context_attention",
    )(qkv, qkv, qkv)


def _dft_cos_sin(k_times_n, period):
    ang = (2.0 * math.pi / period) * (k_times_n % period).astype(F32)
    return jnp.cos(ang), -jnp.sin(ang)


def _ft_stage2_kernel(g_ref, y_ref, o_ref, *, nb):
    for i in range(nb):
        rhs = jnp.concatenate([y_ref[0, i], y_ref[1, i]], axis=0)
        res = _dot(g_ref[i], rhs)
        half = res.shape[0] // 2
        o_ref[0, :, i, :] = res[:half].astype(o_ref.dtype)
        o_ref[1, :, i, :] = res[half:].astype(o_ref.dtype)


def _ft_positions_latent(h_lat):
    n, d = h_lat.shape
    l1 = l2 = int(round(math.sqrt(n)))
    assert l1 * l2 == n
    idx = jnp.arange(l1, dtype=jnp.int32)
    re, im = _dft_cos_sin(idx[:, None] * idx[None, :], l1)
    m1 = (jnp.concatenate([re, im], axis=0) * (l1 ** -0.5)).astype(BF16)
    y = _left_matmul(m1, h_lat.reshape(l1, l2 * d), out_dtype=BF16, tn=8192)
    y = y.reshape(2, l1, l2, d)
    k1 = idx[:, None, None]
    k2 = idx[None, :, None]
    n2 = idx[None, None, :]
    gr, gi = _dft_cos_sin(n2 * k2 * l1 + n2 * k1, n)
    g = jnp.concatenate(
        [jnp.concatenate([gr, -gi], axis=2), jnp.concatenate([gi, gr], axis=2)], axis=1
    )
    g = (g * (l2 ** -0.5)).astype(BF16)
    nb, cb = 8, min(1024, d)
    p = pl.pallas_call(
        functools.partial(_ft_stage2_kernel, nb=nb),
        grid=(l1 // nb, d // cb),
        in_specs=[
            pl.BlockSpec((nb, 2 * l2, 2 * l2), lambda a, c: (a, 0, 0)),
            pl.BlockSpec((2, nb, l2, cb), lambda a, c: (0, a, 0, c)),
        ],
        out_specs=pl.BlockSpec((2, l2, nb, cb), lambda a, c: (0, 0, a, c)),
        out_shape=jax.ShapeDtypeStruct((2, l2, l1, d), BF16),
        compiler_params=_params(2),
        name="ft_stage2",
    )(g, y)
    return p.reshape(2, n, d)


def _ft_positions_dense(h_ctx):
    n, d = h_ctx.shape
    idx = jnp.arange(n, dtype=jnp.int32)
    re, im = _dft_cos_sin(idx[:, None] * idx[None, :], n)
    m = (jnp.concatenate([re, im], axis=0) * (n ** -0.5)).astype(BF16)
    return _left_matmul(m, h_ctx, out_dtype=BF16, tn=min(d, 2048)).reshape(2, n, d)


def _ft_channel_kernel(pr_ref, pi_ref, c_ref, s_ref, o_ref):
    o_ref[...] = (_dot(pr_ref[...], c_ref[...]) + _dot(pi_ref[...], s_ref[...])).astype(o_ref.dtype)


def _ft_channels(p):
    _, t, d = p.shape
    gc = d // FT_GROUPS
    idx = jnp.arange(gc, dtype=jnp.int32)
    re, im = _dft_cos_sin(idx[:, None] * idx[None, :], gc)
    cc = (re * (gc ** -0.5)).astype(BF16)
    sc = (-im * (gc ** -0.5)).astype(BF16)
    tm = MM_TILE_M if t % MM_TILE_M == 0 else t
    return pl.pallas_call(
        _ft_channel_kernel,
        grid=(t // tm, FT_GROUPS),
        in_specs=[
            pl.BlockSpec((None, tm, gc), lambda i, g: (0, i, g)),
            pl.BlockSpec((None, tm, gc), lambda i, g: (1, i, g)),
            pl.BlockSpec((gc, gc), lambda i, g: (0, 0)),
            pl.BlockSpec((gc, gc), lambda i, g: (0, 0)),
        ],
        out_specs=pl.BlockSpec((tm, gc), lambda i, g: (i, g)),
        out_shape=jax.ShapeDtypeStruct((t, d), BF16),
        compiler_params=_params(2),
        name="ft_channels",
    )(p, p, cc, sc)


def _rope_tables(n_lat, n_ctx):
    half = HEAD_DIM // 2
    t = jnp.arange(n_lat)
    inv = ROPE_THETA ** (-jnp.arange(0, half, 2, dtype=F32) / half)
    ang_r = (t // GRID_W).astype(F32)[:, None] * inv[None, :]
    ang_c = (t % GRID_W).astype(F32)[:, None] * inv[None, :]
    cos = jnp.concatenate([jnp.cos(ang_r)] * 2 + [jnp.cos(ang_c)] * 2, axis=-1)
    sin = jnp.concatenate([-jnp.sin(ang_r), jnp.sin(ang_r), -jnp.sin(ang_c), jnp.sin(ang_c)], axis=-1)
    cos = jnp.concatenate([cos, jnp.ones((n_ctx, HEAD_DIM), F32)], axis=0)
    sin = jnp.concatenate([sin, jnp.zeros((n_ctx, HEAD_DIM), F32)], axis=0)
    return cos, sin


def _norm_rope_kernel(x_ref, g_ref, cos_ref, sin_ref, o_ref, *, n_heads, scale):
    quarter = HEAD_DIM // 4
    lane = lax.broadcasted_iota(jnp.int32, (x_ref.shape[0], HEAD_DIM), 1)
    first = (lane % (2 * quarter)) < quarter
    cos = cos_ref[...]
    sin = sin_ref[...]
    g = g_ref[...]
    for hh in range(n_heads):
        cols = slice(hh * HEAD_DIM, (hh + 1) * HEAD_DIM)
        x = x_ref[:, cols].astype(F32)
        xn = x * lax.rsqrt(jnp.mean(x * x, axis=-1, keepdims=True) + RMS_EPS) * g
        partner = jnp.where(first, pltpu.roll(xn, HEAD_DIM - quarter, 1), pltpu.roll(xn, quarter, 1))
        o_ref[:, cols] = ((xn * cos + partner * sin) * scale).astype(o_ref.dtype)


def _norm_rope(qkv, col_block, n_heads, gain, cos, sin, scale):
    t = qkv.shape[0]
    tm = ROW_TILE
    w = n_heads * HEAD_DIM
    return pl.pallas_call(
        functools.partial(_norm_rope_kernel, n_heads=n_heads, scale=scale),
        grid=(t // tm,),
        in_specs=[
            pl.BlockSpec((tm, w), lambda i: (i, col_block)),
            pl.BlockSpec((1, HEAD_DIM), lambda i: (0, 0)),
            pl.BlockSpec((tm, HEAD_DIM), lambda i: (i, 0)),
            pl.BlockSpec((tm, HEAD_DIM), lambda i: (i, 0)),
        ],
        out_specs=pl.BlockSpec((tm, w), lambda i: (i, 0)),
        out_shape=jax.ShapeDtypeStruct((t, w), BF16),
        compiler_params=_params(1),
        name="norm_rope",
    )(qkv, gain.reshape(1, HEAD_DIM), cos, sin)


def _lane_tiles(x):
    return [x[:, j * LANES:(j + 1) * LANES] for j in range(x.shape[1] // LANES)]


FLASH_SUB_ROWS = 64


def _flash_scores(q_ref, ks, s_sc, slot, rep):
    n = ks.shape[0]
    for r in range(rep):
        s_sc[slot, r, :, :n] = _dot_nt(q_ref[:, r * HEAD_DIM:(r + 1) * HEAD_DIM], ks)


def _flash_update(s_sc, slot, vs, m_sc, acc_sc, rep, tq):
    n = vs.shape[0]
    v_ext = jnp.concatenate([vs, jnp.ones_like(vs)], axis=1)
    for r in range(rep):
        for b in range(tq // FLASH_SUB_ROWS):
            rows = slice(b * FLASH_SUB_ROWS, (b + 1) * FLASH_SUB_ROWS)
            tiles = _lane_tiles(s_sc[slot, r, rows, :n])
            m_prev = m_sc[r, rows]
            m_new = jnp.maximum(m_prev, jnp.max(functools.reduce(jnp.maximum, tiles), axis=-1, keepdims=True))
            alpha = jnp.exp2(m_prev - m_new)
            p = jnp.concatenate([jnp.exp2(t - m_new).astype(BF16) for t in tiles], axis=1)
            acc_sc[r, rows] = jnp.concatenate([alpha, alpha], axis=1) * acc_sc[r, rows] + _dot(p, v_ext)
            m_sc[r, rows] = m_new


def _flash_kernel(q_ref, k_ref, v_ref, o_ref, s_sc, m_sc, acc_sc, *, rep, tq, tk, n_main, n_tail):
    m_sc[...] = jnp.full(m_sc.shape, -jnp.inf, F32)
    acc_sc[...] = jnp.zeros(acc_sc.shape, F32)
    n_pairs = n_main // (2 * tk)

    def body(c, carry):
        base = pl.multiple_of(c * 2 * tk, 2 * tk)
        nxt = pl.multiple_of(jnp.minimum(c + 1, n_pairs - 1) * 2 * tk, 2 * tk)
        _flash_scores(q_ref, k_ref[pl.ds(base + tk, tk), :], s_sc, 1, rep)
        _flash_update(s_sc, 0, v_ref[pl.ds(base, tk), :], m_sc, acc_sc, rep, tq)
        _flash_scores(q_ref, k_ref[pl.ds(nxt, tk), :], s_sc, 0, rep)
        _flash_update(s_sc, 1, v_ref[pl.ds(base + tk, tk), :], m_sc, acc_sc, rep, tq)
        return carry

    _flash_scores(q_ref, k_ref[pl.ds(0, tk), :], s_sc, 0, rep)
    lax.fori_loop(0, n_pairs, body, 0)
    if n_tail:
        _flash_scores(q_ref, k_ref[pl.ds(n_main, n_tail), :], s_sc, 0, rep)
        _flash_update(s_sc, 0, v_ref[pl.ds(n_main, n_tail), :], m_sc, acc_sc, rep, tq)
    for r in range(rep):
        o_ref[:, r * HEAD_DIM:(r + 1) * HEAD_DIM] = (
            acc_sc[r, :, :HEAD_DIM] / acc_sc[r, :, HEAD_DIM:]).astype(o_ref.dtype)


def _gqa_attention(q, k, qkv, v_col_block0, n_q):
    t = k.shape[0]
    rep = N_HEADS // N_KV_HEADS
    tq, tk = GQA_TQ, GQA_TK
    n_main = t // (2 * tk) * (2 * tk)
    n_tail = t - n_main
    assert n_q % tq == 0 and n_tail % LANES == 0 and n_tail <= tk
    w = rep * HEAD_DIM
    return pl.pallas_call(
        functools.partial(_flash_kernel, rep=rep, tq=tq, tk=tk, n_main=n_main, n_tail=n_tail),
        grid=(N_KV_HEADS, n_q // tq),
        in_specs=[
            pl.BlockSpec((tq, w), lambda g, i: (i, g)),
            pl.BlockSpec((t, HEAD_DIM), lambda g, i: (0, g)),
            pl.BlockSpec((t, HEAD_DIM), lambda g, i: (0, v_col_block0 + g)),
        ],
        out_specs=pl.BlockSpec((tq, w), lambda g, i: (i, g)),
        out_shape=jax.ShapeDtypeStruct((n_q, N_HEADS * HEAD_DIM), BF16),
        scratch_shapes=[
            pltpu.VMEM((2, rep, tq, tk), F32),
            pltpu.VMEM((rep, tq, HEAD_DIM), F32),
            pltpu.VMEM((rep, tq, 2 * HEAD_DIM), F32),
        ],
        compiler_params=_params(2),
        name="gqa_flash_attention",
    )(q, k, qkv)


def _route(logits, router_bias, block_rows):
    n_tok, n_e = logits.shape
    scores = jax.nn.sigmoid(logits)
    biased = scores + router_bias.astype(F32)
    per_g = n_e // N_EXPERT_GROUPS
    grp = biased.reshape(n_tok, N_EXPERT_GROUPS, per_g)
    g_first, g_mask1 = _first_max(grp)
    g_second, _ = _first_max(jnp.where(g_mask1, -jnp.inf, grp))
    g_score = g_first + g_second
    g_sel = jnp.zeros(g_score.shape, bool)
    for _ in range(TOP_K_GROUPS):
        _, pick = _first_max(jnp.where(g_sel, -jnp.inf, g_score))
        g_sel = g_sel | pick
    e_mask = jnp.repeat(g_sel, per_g, axis=1)
    cand = jnp.where(e_mask, biased, -jnp.inf)
    picks = []
    for _ in range(TOP_K):
        _, pick = _first_max(cand)
        picks.append(pick)
        cand = jnp.where(pick, -jnp.inf, cand)
    onehot = jnp.stack(picks, axis=1)
    w = jnp.sum(jnp.where(onehot, scores[:, None, :], 0.0), axis=-1)
    w = w / jnp.sum(w, axis=-1, keepdims=True) * ROUTED_SCALE

    n_asg = n_tok * TOP_K
    oh = onehot.reshape(n_asg, n_e).astype(jnp.int32)
    csum = jnp.cumsum(oh, axis=0)
    counts = csum[-1]
    padded = (counts + block_rows - 1) // block_rows * block_rows
    pad_end = jnp.cumsum(padded)
    pad_start = pad_end - padded
    dest = jnp.sum(oh * (csum - 1 + pad_start[None, :]), axis=1).astype(jnp.int32)
    n_blocks = -(-(n_asg + n_e * (block_rows - 1)) // block_rows)
    blk_row0 = jnp.arange(n_blocks, dtype=jnp.int32) * block_rows
    block_e = jnp.minimum(jnp.sum((pad_end[None, :] <= blk_row0[:, None]).astype(jnp.int32), axis=1), n_e - 1)
    be_oh = (block_e[:, None] == jnp.arange(n_e, dtype=jnp.int32)[None, :]).astype(jnp.int32)
    seg_left = jnp.sum(be_oh * (counts + pad_start)[None, :], axis=1) - blk_row0
    n_valid = jnp.clip(seg_left, 0, block_rows).astype(jnp.int32)
    codes = _plan_rows(dest, n_blocks * block_rows, block_rows)
    return w, block_e.astype(jnp.int32), n_valid, codes


def _first_max(x):
    m = jnp.max(x, axis=-1, keepdims=True)
    idx = lax.broadcasted_iota(jnp.int32, x.shape, x.ndim - 1)
    first = jnp.min(jnp.where(x == m, idx, x.shape[-1]), axis=-1, keepdims=True)
    return m[..., 0], idx == first


CODE_SLOT_SHIFT = 16


def _plan_kernel(dest_hbm, codes_ref, dest_smem, sem, *, chunk, block_rows):
    j = pl.program_id(0)
    n_rows = codes_ref.shape[0]
    log_rows = block_rows.bit_length() - 1
    log_k = TOP_K.bit_length() - 1

    @pl.when(j == 0)
    def _():
        def init(r, c):
            parity = lax.shift_right_logical(r, log_rows) & 1
            codes_ref[r] = ((r & (block_rows - 1)) + parity * block_rows) | (TOP_K << CODE_SLOT_SHIFT)
            return c

        lax.fori_loop(0, n_rows, init, 0)

    cp = pltpu.make_async_copy(dest_hbm.at[j], dest_smem, sem)
    cp.start()
    cp.wait()

    def place(a, c):
        g = j * chunk + a
        codes_ref[dest_smem[a]] = lax.shift_right_logical(g, log_k) | ((g & (TOP_K - 1)) << CODE_SLOT_SHIFT)
        return c

    lax.fori_loop(0, chunk, place, 0)


def _plan_rows(dest, n_rows, block_rows):
    n_asg = dest.shape[0]
    assert block_rows & (block_rows - 1) == 0 and TOP_K & (TOP_K - 1) == 0
    chunk = max(c for c in range(LANES, 8192 + 1, LANES) if n_asg % c == 0)
    n_chunks = n_asg // chunk
    return pl.pallas_call(
        functools.partial(_plan_kernel, chunk=chunk, block_rows=block_rows),
        grid=(n_chunks,),
        in_specs=[pl.BlockSpec(memory_space=pl.ANY)],
        out_specs=pl.BlockSpec(memory_space=pltpu.SMEM),
        out_shape=jax.ShapeDtypeStruct((n_rows,), jnp.int32),
        scratch_shapes=[pltpu.SMEM((chunk,), jnp.int32), pltpu.SemaphoreType.DMA(())],
        compiler_params=_params(1),
        name="plan_rows",
    )(dest.reshape(n_chunks, chunk))


def _swiglu_rows(x, wgu_ref, wd_ref):
    hgu = _dot(x, wgu_ref[...])
    de = hgu.shape[1] // 2
    gate = hgu[:, :de]
    hb = (gate * jax.nn.sigmoid(gate) * hgu[:, de:]).astype(BF16)
    return _dot(hb, wd_ref[...])


def _expert_kernel(be_ref, nv_ref, codes_ref, t_hbm, wgu_ref, wd_ref, y_hbm,
                   xbuf, ybuf, sem_in, sem_out, *, n_tok, br):
    i = pl.program_id(0)
    nb = pl.num_programs(0)
    slot = i % 2
    active = nv_ref[i] > 0
    prev_active = nv_ref[jnp.maximum(i - 1, 0)] > 0
    token_mask = (1 << CODE_SLOT_SHIFT) - 1

    def gather(block, buf_slot):
        for r in range(br):
            tok = codes_ref[block * br + r] & token_mask
            pltpu.make_async_copy(t_hbm.at[pl.ds(tok, 1)], xbuf.at[buf_slot, pl.ds(r, 1)],
                                  sem_in.at[buf_slot]).start()

    def gather_wait(buf_slot):
        pltpu.make_async_copy(t_hbm.at[pl.ds(0, br)], xbuf.at[buf_slot], sem_in.at[buf_slot]).wait()

    def scatter(block, buf_slot):
        for r in range(br):
            code = codes_ref[block * br + r]
            dst = lax.shift_right_logical(code, CODE_SLOT_SHIFT) * n_tok + (code & token_mask)
            pltpu.make_async_copy(ybuf.at[buf_slot, pl.ds(r, 1)], y_hbm.at[pl.ds(dst, 1)],
                                  sem_out.at[buf_slot]).start()

    def scatter_wait(buf_slot):
        pltpu.make_async_copy(ybuf.at[buf_slot], y_hbm.at[pl.ds(0, br)], sem_out.at[buf_slot]).wait()

    @pl.when(i == 0)
    def _():
        gather(0, 0)

    @pl.when(active)
    def _():
        gather(jnp.minimum(i + 1, nb - 1), 1 - slot)
        gather_wait(slot)
        ybuf[slot] = _swiglu_rows(xbuf[slot].astype(BF16), wgu_ref, wd_ref)
        scatter(i, slot)

        @pl.when(i > 0)
        def _():
            scatter_wait(1 - slot)

        @pl.when(i == nb - 1)
        def _():
            gather_wait(1 - slot)
            scatter_wait(slot)

    @pl.when(jnp.logical_not(active) & ((i == 0) | prev_active))
    def _():
        gather_wait(slot)

        @pl.when(i > 0)
        def _():
            scatter_wait(1 - slot)


def _routed_experts(tok_f32, block_e, n_valid, codes, w_gu, w_down):
    n_tok, d = tok_f32.shape
    n_blocks = block_e.shape[0]
    br = codes.shape[0] // n_blocks
    de2 = w_gu.shape[2]
    grid_spec = pltpu.PrefetchScalarGridSpec(
        num_scalar_prefetch=3,
        grid=(n_blocks,),
        in_specs=[
            pl.BlockSpec(memory_space=pl.ANY),
            pl.BlockSpec((None, d, de2), lambda i, be, nv, cd: (be[i], 0, 0)),
            pl.BlockSpec((None, de2 // 2, d), lambda i, be, nv, cd: (be[i], 0, 0)),
        ],
        out_specs=pl.BlockSpec(memory_space=pl.ANY),
        scratch_shapes=[
            pltpu.VMEM((2, br, d), F32),
            pltpu.VMEM((2, br, d), F32),
            pltpu.SemaphoreType.DMA((2,)),
            pltpu.SemaphoreType.DMA((2,)),
        ],
    )
    return pl.pallas_call(
        functools.partial(_expert_kernel, n_tok=n_tok, br=br),
        grid_spec=grid_spec,
        out_shape=jax.ShapeDtypeStruct((TOP_K * n_tok + 2 * br, d), F32),
        compiler_params=_params(1),
        name="routed_experts",
    )(block_e, n_valid, codes, tok_f32, w_gu, w_down)


def _shared_kernel(x_ref, wgu_ref, wd_ref, o_ref):
    o_ref[...] = _swiglu_rows(x_ref[...].astype(BF16), wgu_ref, wd_ref).astype(o_ref.dtype)


def _shared_expert(tok_f32, w_gu, w_down):
    t, d = tok_f32.shape
    tm = ROW_TILE
    de2 = w_gu.shape[1]
    return pl.pallas_call(
        _shared_kernel,
        grid=(t // tm,),
        in_specs=[
            pl.BlockSpec((tm, d), lambda i: (i, 0)),
            pl.BlockSpec((d, de2), lambda i: (0, 0)),
            pl.BlockSpec((de2 // 2, d), lambda i: (0, 0)),
        ],
        out_specs=pl.BlockSpec((tm, d), lambda i: (i, 0)),
        out_shape=jax.ShapeDtypeStruct((t, d), F32),
        compiler_params=_params(1),
        name="shared_expert",
    )(tok_f32, w_gu, w_down)


def _combine_kernel(*refs, with_h):
    x_ref, sh_ref, y0_ref, y1_ref, y2_ref, y3_ref, w_ref, gate_ref, g_ref, b_ref = refs[:10]
    w = w_ref[...]
    y = sh_ref[...]
    for k, y_ref in enumerate((y0_ref, y1_ref, y2_ref, y3_ref)):
        y = y + w[:, k:k + 1] * y_ref[...]
    z = DEEPNORM_ALPHA * x_ref[...] + gate_ref[...] * y
    xn = _layer_norm_rows(z, g_ref[...], b_ref[...])
    if with_h:
        sc_ref, shf_ref, xo_ref, h_ref = refs[10:14]
        xo_ref[...] = xn
        h_ref[...] = (xn * (1.0 + sc_ref[...]) + shf_ref[...]).astype(h_ref.dtype)
    else:
        xo_ref = refs[10]
        xo_ref[...] = xn


def _moe_combine(x, shared, y_slots, w, mod, gate_slot, ln_g, ln_b, n_lat, *, next_mod=None,
                 n_out_rows=None):
    t, d = x.shape
    n_out_rows = t if n_out_rows is None else n_out_rows
    tm = COMBINE_ROWS
    nlb = n_lat // tm
    with_h = next_mod is not None
    y3 = y_slots
    row = pl.BlockSpec((tm, d), lambda i: (i, 0))
    vec = pl.BlockSpec((1, d), lambda i: (0, 0))

    def slot_spec(k):
        return pl.BlockSpec((tm, d), lambda i: (k * (t // tm) + i, 0))

    in_specs = [row, row] + [slot_spec(k) for k in range(TOP_K)] + [
        pl.BlockSpec((tm, TOP_K), lambda i: (i, 0)),
        _mod_spec(gate_slot, nlb, d), vec, vec,
    ]
    args = [x, shared, y3, y3, y3, y3, w, mod, ln_g.reshape(1, d), ln_b.reshape(1, d)]
    out_specs = [row]
    out_shape = [jax.ShapeDtypeStruct((n_out_rows, d), F32)]
    if with_h:
        in_specs += [_mod_spec(1, nlb, d), _mod_spec(0, nlb, d)]
        args += [next_mod, next_mod]
        out_specs.append(row)
        out_shape.append(jax.ShapeDtypeStruct((n_out_rows, d), BF16))
    return pl.pallas_call(
        functools.partial(_combine_kernel, with_h=with_h),
        grid=(n_out_rows // tm,),
        in_specs=in_specs,
        out_specs=out_specs,
        out_shape=out_shape,
        compiler_params=_params(1),
        name="moe_combine",
    )(*args)


def kernel(x, c, ctx, c_ctx, w_mod_down, w_mod_up, b_mod, ln_g, ln_b, gm_w_in, gm_v_g, gm_v_b, gm_w_s, gm_b_s, gm_w_out, na_w_qkv, na_rpb, na_w_out, ft_w_out, ga_w_qkv, ga_q_g, ga_k_g, ga_w_out, router_w, router_bias, exp_w_gate, exp_w_up, exp_w_down, sh_w_gate, sh_w_up, sh_w_down):
    b, n, d = x.shape
    nc = ctx.shape[1]
    assert b == 1 and c.shape[0] == 1
    n_tok = n + nc
    rows = n // GRID_W

    xs = jnp.concatenate([x[0], ctx[0]], axis=0)
    cond8 = jnp.zeros((8, d), F32).at[0].set(c[0]).at[1].set(c_ctx)
    mods = _ada_mod_all(cond8, w_mod_down, w_mod_up, b_mod)
    router_w_pad = jnp.pad(router_w, ((0, 0), (0, 0), (0, LANES - N_EXPERTS)))

    h = _modulate(xs, mods[0], 1, 0, n, BF16)
    for i in range(DEPTH):
        kind = i % 4
        mod = mods[i]
        last = i == DEPTH - 1

        if kind == 0:
            uv = _matmul(h, gm_w_in[0].astype(BF16), out_dtype=BF16, act="gelu")
            gated = _gmlp_gate(uv, gm_v_g[0], gm_v_b[0], gm_w_s[0], gm_b_s[0])
            o = _matmul(gated, gm_w_out[0].astype(BF16), out_dtype=F32)
        elif kind == 1:
            qkv = _matmul(h, na_w_qkv[0].astype(BF16), out_dtype=BF16)
            a_lat = _na_attention(qkv, _na_bias_table(na_rpb[0], rows), n, nc)
            a_ctx = _ctx_attention(qkv, n, nc)
            o = _matmul(jnp.concatenate([a_lat, a_ctx], axis=0), na_w_out[0].astype(BF16), out_dtype=F32)
        elif kind == 2:
            p = jnp.concatenate([_ft_positions_latent(h[:n]), _ft_positions_dense(h[n:])], axis=1)
            o = _matmul(_ft_channels(p), ft_w_out[0].astype(BF16), out_dtype=F32)
        else:
            qkv = _matmul(h, ga_w_qkv[0].astype(BF16), out_dtype=BF16, tn=1024)
            cos, sin = _rope_tables(n, nc)
            qn = _norm_rope(qkv, 0, N_HEADS, ga_q_g[0], cos, sin, HEAD_DIM ** -0.5 * math.log2(math.e))
            kn = _norm_rope(qkv, N_HEADS // N_KV_HEADS, N_KV_HEADS, ga_k_g[0], cos, sin, 1.0)
            att = _gqa_attention(qn, kn, qkv, (N_HEADS + N_KV_HEADS), n)
            att = jnp.concatenate([att, jnp.zeros((nc, d), BF16)], axis=0)
            o = _matmul(att, ga_w_out[0].astype(BF16), out_dtype=F32)

        xs, h2, logits = _resln(xs, o, mod, 2, ln_g[i, 0], ln_b[i, 0], n, next_mod=mod,
                                scale_slot=4, shift_slot=3, h_dtype=F32, router_w=router_w_pad[i])

        w, block_e, n_valid, codes = _route(logits[:, :N_EXPERTS], router_bias[i], MOE_ROWS)
        w_gu = jnp.concatenate([exp_w_gate[i], exp_w_up[i]], axis=-1).astype(BF16)
        y_slots = _routed_experts(h2, block_e, n_valid, codes, w_gu, exp_w_down[i].astype(BF16))
        s_gu = jnp.concatenate([sh_w_gate[i], sh_w_up[i]], axis=-1).astype(BF16)
        shared = _shared_expert(h2, s_gu, sh_w_down[i].astype(BF16))
        if last:
            (xs,) = _moe_combine(xs, shared, y_slots, w, mod, 5, ln_g[i, 1], ln_b[i, 1], n,
                                 n_out_rows=n)
        else:
            xs, h = _moe_combine(xs, shared, y_slots, w, mod, 5, ln_g[i, 1], ln_b[i, 1], n,
                                 next_mod=mods[i + 1])
    return xs.reshape(b, n, d)
```

```python
import functools
import math

import jax
import jax.numpy as jnp
import numpy as np
from jax import lax
from jax.experimental import pallas as pl
from jax.experimental.pallas import tpu as pltpu

F32 = jnp.float32
BF16 = jnp.bfloat16

D_MODEL = 4096
DEPTH = 4
GRID_W = 64
N_MOD = 6
LN_EPS = 1e-5
RMS_EPS = 1e-6
DEEPNORM_ALPHA = (2 * DEPTH) ** 0.25
GM_CHUNK = 128
GM_GROUPS = 16
N_HEADS = 32
HEAD_DIM = 128
N_KV_HEADS = 8
ROPE_THETA = 10000.0
NA_WIN_H = 8
NA_WIN_W = 16
FT_GROUPS = 8
N_EXPERTS = 32
N_EXPERT_GROUPS = 4
TOP_K_GROUPS = 2
TOP_K = 4
D_EXPERT = 384
ROUTED_SCALE = 2.5

LANES = 128
VMEM_LIMIT_BYTES = 56 * 1024 * 1024
ROW_TILE = 256
MM_TILE_M = 1280
MM_TILE_N = 1024
MOE_ROWS = 256
COMBINE_ROWS = 128
NA_Q_ROWS = 4
NA_KEY_ROWS = 12
GQA_TQ = 256
GQA_TK = 512
MASK_VALUE = -1e30


def _params(n_axes):
    return pltpu.CompilerParams(
        dimension_semantics=("arbitrary",) * n_axes,
        vmem_limit_bytes=VMEM_LIMIT_BYTES,
    )


def _dot(a, b, precision=None):
    return jnp.dot(a, b, preferred_element_type=F32, precision=precision)


def _dot_nt(a, b):
    return lax.dot_general(a, b, (((1,), (1,)), ((), ())), preferred_element_type=F32)


def _mm_kernel(a_ref, b_ref, o_ref, *, act):
    acc = _dot(a_ref[...], b_ref[...])
    if act == "gelu":
        acc = jax.nn.gelu(acc)
    o_ref[...] = acc.astype(o_ref.dtype)


def _matmul(a, b, *, out_dtype, act=None, tm=MM_TILE_M, tn=MM_TILE_N):
    m, k = a.shape
    n = b.shape[1]
    tm = min(tm, m)
    tn = min(tn, n)
    if jnp.dtype(out_dtype).itemsize == 4:
        tn = min(tn, MM_TILE_N // 2)
    while n % tn:
        tn //= 2
    assert m % tm == 0 and tn % LANES == 0
    return pl.pallas_call(
        functools.partial(_mm_kernel, act=act),
        grid=(m // tm, n // tn),
        in_specs=[
            pl.BlockSpec((tm, k), lambda i, j: (i, 0)),
            pl.BlockSpec((k, tn), lambda i, j: (0, j)),
        ],
        out_specs=pl.BlockSpec((tm, tn), lambda i, j: (i, j)),
        out_shape=jax.ShapeDtypeStruct((m, n), out_dtype),
        compiler_params=_params(2),
        name="matmul",
    )(a, b)


def _lmm_kernel(m_ref, x_ref, o_ref):
    o_ref[...] = _dot(m_ref[...], x_ref[...]).astype(o_ref.dtype)


def _left_matmul(mat, x, *, out_dtype, tn):
    r, k = mat.shape
    n = x.shape[1]
    assert n % tn == 0
    return pl.pallas_call(
        _lmm_kernel,
        grid=(n // tn,),
        in_specs=[
            pl.BlockSpec((r, k), lambda j: (0, 0)),
            pl.BlockSpec((k, tn), lambda j: (0, j)),
        ],
        out_specs=pl.BlockSpec((r, tn), lambda j: (0, j)),
        out_shape=jax.ShapeDtypeStruct((r, n), out_dtype),
        compiler_params=_params(1),
        name="left_matmul",
    )(mat, x)


def _mod_kernel(cond_ref, wd_ref, wu_ref, b_ref, o_ref):
    cond = cond_ref[...]
    t = _dot(cond * jax.nn.sigmoid(cond), wd_ref[...], precision=lax.Precision.HIGHEST)
    o_ref[...] = _dot(t, wu_ref[...], precision=lax.Precision.HIGHEST) + b_ref[...]


def _ada_mod_all(cond8, w_mod_down, w_mod_up, b_mod):
    depth, d, rank = w_mod_down.shape
    out = pl.pallas_call(
        _mod_kernel,
        grid=(depth, N_MOD),
        in_specs=[
            pl.BlockSpec((8, d), lambda l, j: (0, 0)),
            pl.BlockSpec((None, d, rank), lambda l, j: (l, 0, 0)),
            pl.BlockSpec((None, rank, d), lambda l, j: (l, 0, j)),
            pl.BlockSpec((None, None, 1, d), lambda l, j: (l, j, 0, 0)),
        ],
        out_specs=pl.BlockSpec((None, None, 8, d), lambda l, j: (l, j, 0, 0)),
        out_shape=jax.ShapeDtypeStruct((depth, N_MOD, 8, d), F32),
        compiler_params=_params(2),
        name="ada_mod",
    )(cond8, w_mod_down, w_mod_up, b_mod.reshape(depth, N_MOD, 1, d))
    return jnp.transpose(out[:, :, :2], (0, 2, 1, 3))[:, :, :, None, :]


def _mod_spec(slot, n_lat_blocks, d):
    return pl.BlockSpec((None, None, 1, d), lambda i: (i // n_lat_blocks, slot, 0, 0))


def _modulate_kernel(x_ref, sc_ref, sh_ref, h_ref):
    h_ref[...] = (x_ref[...] * (1.0 + sc_ref[...]) + sh_ref[...]).astype(h_ref.dtype)


def _modulate(x, mod, scale_slot, shift_slot, n_lat, out_dtype):
    t, d = x.shape
    tm = ROW_TILE
    nlb = n_lat // tm
    return pl.pallas_call(
        _modulate_kernel,
        grid=(t // tm,),
        in_specs=[
            pl.BlockSpec((tm, d), lambda i: (i, 0)),
            _mod_spec(scale_slot, nlb, d),
            _mod_spec(shift_slot, nlb, d),
        ],
        out_specs=pl.BlockSpec((tm, d), lambda i: (i, 0)),
        out_shape=jax.ShapeDtypeStruct((t, d), out_dtype),
        compiler_params=_params(1),
        name="modulate",
    )(x, mod, mod)


def _layer_norm_rows(z, g, b):
    mu = jnp.mean(z, axis=-1, keepdims=True)
    zc = z - mu
    var = jnp.mean(zc * zc, axis=-1, keepdims=True)
    return zc * lax.rsqrt(var + LN_EPS) * g + b


def _resln_kernel(*refs, with_h, with_router):
    x_ref, o_ref, gate_ref, g_ref, b_ref = refs[:5]
    pos = 5
    if with_h:
        sc_ref, sh_ref = refs[pos:pos + 2]
        pos += 2
    if with_router:
        rw_ref = refs[pos]
        pos += 1
    xo_ref = refs[pos]
    pos += 1
    z = DEEPNORM_ALPHA * x_ref[...] + gate_ref[...] * o_ref[...].astype(F32)
    xn = _layer_norm_rows(z, g_ref[...], b_ref[...])
    xo_ref[...] = xn
    if with_h:
        h_ref = refs[pos]
        pos += 1
        h = xn * (1.0 + sc_ref[...]) + sh_ref[...]
        h_ref[...] = h.astype(h_ref.dtype)
        if with_router:
            lg_ref = refs[pos]
            lg_ref[...] = _dot(h, rw_ref[...], precision=lax.Precision.HIGHEST)


def _resln(x, o, mod, gate_slot, ln_g, ln_b, n_lat, *, next_mod=None, scale_slot=None,
           shift_slot=None, h_dtype=None, router_w=None, n_out_rows=None):
    t, d = x.shape
    n_out_rows = t if n_out_rows is None else n_out_rows
    tm = ROW_TILE
    nlb = n_lat // tm
    with_h = next_mod is not None
    with_router = router_w is not None
    row = pl.BlockSpec((tm, d), lambda i: (i, 0))
    vec = pl.BlockSpec((1, d), lambda i: (0, 0))
    in_specs = [row, row, _mod_spec(gate_slot, nlb, d), vec, vec]
    args = [x, o, mod, ln_g.reshape(1, d), ln_b.reshape(1, d)]
    out_specs = [row]
    out_shape = [jax.ShapeDtypeStruct((n_out_rows, d), F32)]
    if with_h:
        in_specs += [_mod_spec(scale_slot, nlb, d), _mod_spec(shift_slot, nlb, d)]
        args += [next_mod, next_mod]
        out_specs.append(row)
        out_shape.append(jax.ShapeDtypeStruct((n_out_rows, d), h_dtype))
    if with_router:
        ne = router_w.shape[1]
        in_specs.append(pl.BlockSpec((d, ne), lambda i: (0, 0)))
        args.append(router_w)
        out_specs.append(pl.BlockSpec((tm, ne), lambda i: (i, 0)))
        out_shape.append(jax.ShapeDtypeStruct((n_out_rows, ne), F32))
    return pl.pallas_call(
        functools.partial(_resln_kernel, with_h=with_h, with_router=with_router),
        grid=(n_out_rows // tm,),
        in_specs=in_specs,
        out_specs=out_specs,
        out_shape=out_shape,
        compiler_params=_params(1),
        name="residual_layernorm",
    )(*args)


def _gmlp_gate_kernel(u_ref, v_ref, vg_ref, vb_ref, ws_ref, bs_ref, o_ref, *, n_groups, gw):
    vn = _layer_norm_rows(v_ref[...].astype(F32), vg_ref[...], vb_ref[...]).astype(BF16)
    for g in range(n_groups):
        cols = slice(g * gw, (g + 1) * gw)
        sv = _dot(ws_ref[g], vn[:, cols]) + bs_ref[:, cols]
        o_ref[:, cols] = (u_ref[:, cols].astype(F32) * sv).astype(o_ref.dtype)


def _gmlp_gate(uv, v_g, v_b, w_s, b_s):
    t, two_w = uv.shape
    width = two_w // 2
    n_groups, chunk, _ = w_s.shape
    gw = width // n_groups
    bias_full = jnp.repeat(b_s.T, gw, axis=1).astype(F32)
    return pl.pallas_call(
        functools.partial(_gmlp_gate_kernel, n_groups=n_groups, gw=gw),
        grid=(t // chunk,),
        in_specs=[
            pl.BlockSpec((chunk, width), lambda i: (i, 0)),
            pl.BlockSpec((chunk, width), lambda i: (i, 1)),
            pl.BlockSpec((1, width), lambda i: (0, 0)),
            pl.BlockSpec((1, width), lambda i: (0, 0)),
            pl.BlockSpec((n_groups, chunk, chunk), lambda i: (0, 0, 0)),
            pl.BlockSpec((chunk, width), lambda i: (0, 0)),
        ],
        out_specs=pl.BlockSpec((chunk, width), lambda i: (i, 0)),
        out_shape=jax.ShapeDtypeStruct((t, width), BF16),
        compiler_params=_params(1),
        name="gmlp_gate",
    )(uv, uv, v_g.reshape(1, width), v_b.reshape(1, width), w_s.astype(BF16), bias_full)


def _na_bias_table(rpb, rows):
    nb = rows // NA_Q_ROWS
    h = rpb.shape[0]
    n_dr = 2 * NA_WIN_H - 1
    c = np.arange(GRID_W)
    dc = c[None, :] - c[:, None] + NA_WIN_W - 1
    cs = np.clip(c - NA_WIN_W // 2, 0, GRID_W - NA_WIN_W)
    ok_c = (c[None, :] >= cs[:, None]) & (c[None, :] < cs[:, None] + NA_WIN_W)
    sel = ((np.arange(2 * NA_WIN_W - 1)[:, None, None] == dc[None]) & ok_c[None]).astype(np.float32)
    planes = jnp.einsum("hrd,dqk->hrqk", rpb.astype(F32), jnp.asarray(sel), precision=lax.Precision.HIGHEST)
    planes = jnp.where(ok_c[None, None], planes, MASK_VALUE)
    planes = jnp.concatenate([planes, jnp.full((h, 1, GRID_W, GRID_W), MASK_VALUE, F32)], axis=1)
    tables = []
    for b in (0, 1, nb - 1):
        r0 = b * NA_Q_ROWS
        kw0 = int(np.clip(r0 - NA_WIN_H // 2, 0, rows - NA_KEY_ROWS))
        r = r0 + np.arange(NA_Q_ROWS)[:, None]
        kr = kw0 + np.arange(NA_KEY_ROWS)[None, :]
        rs = np.clip(r - NA_WIN_H // 2, 0, rows - NA_WIN_H)
        ok_r = (kr >= rs) & (kr < rs + NA_WIN_H)
        dr = np.where(ok_r, kr - r + NA_WIN_H - 1, n_dr)
        tbl = jnp.stack([planes[:, int(d)] for d in dr.reshape(-1)], axis=1)
        tbl = tbl.reshape(h, NA_Q_ROWS, NA_KEY_ROWS, GRID_W, GRID_W).transpose(0, 1, 3, 2, 4)
        tables.append(tbl.reshape(h, NA_Q_ROWS * GRID_W, NA_KEY_ROWS * GRID_W))
    return jnp.stack(tables, axis=0)


def _na_kernel(q_ref, k_ref, v_ref, bias_ref, o_ref, *, rows, n_lat, n_ctx):
    b = pl.program_id(1)
    kw0 = jnp.clip(b * NA_Q_ROWS - NA_WIN_H // 2, 0, rows - NA_KEY_ROWS)
    start = pl.multiple_of(kw0 * GRID_W, GRID_W)
    nk = NA_KEY_ROWS * GRID_W
    scale = HEAD_DIM ** -0.5
    q = q_ref[...]
    kw = k_ref[pl.ds(start, nk), :]
    vw = v_ref[pl.ds(start, nk), :]
    kc = k_ref[pl.ds(n_lat, n_ctx), :]
    vc = v_ref[pl.ds(n_lat, n_ctx), :]
    s_loc = _dot_nt(q, kw) * scale + bias_ref[...]
    s_ctx = _dot_nt(q, kc) * scale
    m = jnp.maximum(jnp.max(s_loc, axis=-1, keepdims=True), jnp.max(s_ctx, axis=-1, keepdims=True))
    p_loc = jnp.exp(s_loc - m)
    p_ctx = jnp.exp(s_ctx - m)
    l = jnp.sum(p_loc, axis=-1, keepdims=True) + jnp.sum(p_ctx, axis=-1, keepdims=True)
    o = _dot(p_loc.astype(BF16), vw) + _dot(p_ctx.astype(BF16), vc)
    o_ref[...] = (o / l).astype(o_ref.dtype)


def _na_attention(qkv, bias_table, n_lat, n_ctx):
    t = qkv.shape[0]
    rows = n_lat // GRID_W
    nb = rows // NA_Q_ROWS
    tq = NA_Q_ROWS * GRID_W
    nk = NA_KEY_ROWS * GRID_W
    h = N_HEADS
    return pl.pallas_call(
        functools.partial(_na_kernel, rows=rows, n_lat=n_lat, n_ctx=n_ctx),
        grid=(h, nb),
        in_specs=[
            pl.BlockSpec((tq, HEAD_DIM), lambda hh, b: (b, hh)),
            pl.BlockSpec((t, HEAD_DIM), lambda hh, b: (0, h + hh)),
            pl.BlockSpec((t, HEAD_DIM), lambda hh, b: (0, 2 * h + hh)),
            pl.BlockSpec((None, None, tq, nk),
                         lambda hh, b: (jnp.where(b == 0, 0, jnp.where(b == nb - 1, 2, 1)), hh, 0, 0)),
        ],
        out_specs=pl.BlockSpec((tq, HEAD_DIM), lambda hh, b: (b, hh)),
        out_shape=jax.ShapeDtypeStruct((n_lat, h * HEAD_DIM), BF16),
        compiler_params=_params(2),
        name="neighbourhood_attention",
    )(qkv, qkv, qkv, bias_table)


def _ctx_attn_kernel(q_ref, k_ref, v_ref, o_ref):
    s = _dot_nt(q_ref[...], k_ref[...]) * (HEAD_DIM ** -0.5)
    m = jnp.max(s, axis=-1, keepdims=True)
    p = jnp.exp(s - m)
    l = jnp.sum(p, axis=-1, keepdims=True)
    o_ref[...] = (_dot(p.astype(BF16), v_ref[...]) / l).astype(o_ref.dtype)


def _ctx_attention(qkv, n_lat, n_ctx):
    h = N_HEADS
    rb = n_lat // n_ctx
    return pl.pallas_call(
        _ctx_attn_kernel,
        grid=(h,),
        in_specs=[
            pl.BlockSpec((n_ctx, HEAD_DIM), lambda hh: (rb, hh)),
            pl.BlockSpec((n_ctx, HEAD_DIM), lambda hh: (rb, h + hh)),
            pl.BlockSpec((n_ctx, HEAD_DIM), lambda hh: (rb, 2 * h + hh)),
        ],
        out_specs=pl.BlockSpec((n_ctx, HEAD_DIM), lambda hh: (0, hh)),
        out_shape=jax.ShapeDtypeStruct((n_ctx, h * HEAD_DIM), BF16),
        compiler_params=_params(1),
        name="context_attention",
    )(qkv, qkv, qkv)


def _dft_cos_sin(k_times_n, period):
    ang = (2.0 * math.pi / period) * (k_times_n % period).astype(F32)
    return jnp.cos(ang), -jnp.sin(ang)


def _ft_stage2_kernel(g_ref, y_ref, o_ref, *, nb):
    for i in range(nb):
        rhs = jnp.concatenate([y_ref[0, i], y_ref[1, i]], axis=0)
        res = _dot(g_ref[i], rhs)
        half = res.shape[0] // 2
        o_ref[0, :, i, :] = res[:half].astype(o_ref.dtype)
        o_ref[1, :, i, :] = res[half:].astype(o_ref.dtype)


def _ft_positions_latent(h_lat):
    n, d = h_lat.shape
    l1 = l2 = int(round(math.sqrt(n)))
    assert l1 * l2 == n
    idx = jnp.arange(l1, dtype=jnp.int32)
    re, im = _dft_cos_sin(idx[:, None] * idx[None, :], l1)
    m1 = (jnp.concatenate([re, im], axis=0) * (l1 ** -0.5)).astype(BF16)
    y = _left_matmul(m1, h_lat.reshape(l1, l2 * d), out_dtype=BF16, tn=8192)
    y = y.reshape(2, l1, l2, d)
    k1 = idx[:, None, None]
    k2 = idx[None, :, None]
    n2 = idx[None, None, :]
    gr, gi = _dft_cos_sin(n2 * k2 * l1 + n2 * k1, n)
    g = jnp.concatenate(
        [jnp.concatenate([gr, -gi], axis=2), jnp.concatenate([gi, gr], axis=2)], axis=1
    )
    g = (g * (l2 ** -0.5)).astype(BF16)
    nb, cb = 8, min(1024, d)
    p = pl.pallas_call(
        functools.partial(_ft_stage2_kernel, nb=nb),
        grid=(l1 // nb, d // cb),
        in_specs=[
            pl.BlockSpec((nb, 2 * l2, 2 * l2), lambda a, c: (a, 0, 0)),
            pl.BlockSpec((2, nb, l2, cb), lambda a, c: (0, a, 0, c)),
        ],
        out_specs=pl.BlockSpec((2, l2, nb, cb), lambda a, c: (0, 0, a, c)),
        out_shape=jax.ShapeDtypeStruct((2, l2, l1, d), BF16),
        compiler_params=_params(2),
        name="ft_stage2",
    )(g, y)
    return p.reshape(2, n, d)


def _ft_positions_dense(h_ctx):
    n, d = h_ctx.shape
    idx = jnp.arange(n, dtype=jnp.int32)
    re, im = _dft_cos_sin(idx[:, None] * idx[None, :], n)
    m = (jnp.concatenate([re, im], axis=0) * (n ** -0.5)).astype(BF16)
    return _left_matmul(m, h_ctx, out_dtype=BF16, tn=min(d, 2048)).reshape(2, n, d)


def _ft_channel_kernel(pr_ref, pi_ref, c_ref, s_ref, o_ref):
    o_ref[...] = (_dot(pr_ref[...], c_ref[...]) + _dot(pi_ref[...], s_ref[...])).astype(o_ref.dtype)


def _ft_channels(p):
    _, t, d = p.shape
    gc = d // FT_GROUPS
    idx = jnp.arange(gc, dtype=jnp.int32)
    re, im = _dft_cos_sin(idx[:, None] * idx[None, :], gc)
    cc = (re * (gc ** -0.5)).astype(BF16)
    sc = (-im * (gc ** -0.5)).astype(BF16)
    tm = MM_TILE_M if t % MM_TILE_M == 0 else t
    return pl.pallas_call(
        _ft_channel_kernel,
        grid=(t // tm, FT_GROUPS),
        in_specs=[
            pl.BlockSpec((None, tm, gc), lambda i, g: (0, i, g)),
            pl.BlockSpec((None, tm, gc), lambda i, g: (1, i, g)),
            pl.BlockSpec((gc, gc), lambda i, g: (0, 0)),
            pl.BlockSpec((gc, gc), lambda i, g: (0, 0)),
        ],
        out_specs=pl.BlockSpec((tm, gc), lambda i, g: (i, g)),
        out_shape=jax.ShapeDtypeStruct((t, d), BF16),
        compiler_params=_params(2),
        name="ft_channels",
    )(p, p, cc, sc)


def _rope_tables(n_lat, n_ctx):
    half = HEAD_DIM // 2
    t = jnp.arange(n_lat)
    inv = ROPE_THETA ** (-jnp.arange(0, half, 2, dtype=F32) / half)
    ang_r = (t // GRID_W).astype(F32)[:, None] * inv[None, :]
    ang_c = (t % GRID_W).astype(F32)[:, None] * inv[None, :]
    cos = jnp.concatenate([jnp.cos(ang_r)] * 2 + [jnp.cos(ang_c)] * 2, axis=-1)
    sin = jnp.concatenate([-jnp.sin(ang_r), jnp.sin(ang_r), -jnp.sin(ang_c), jnp.sin(ang_c)], axis=-1)
    cos = jnp.concatenate([cos, jnp.ones((n_ctx, HEAD_DIM), F32)], axis=0)
    sin = jnp.concatenate([sin, jnp.zeros((n_ctx, HEAD_DIM), F32)], axis=0)
    return cos, sin


def _norm_rope_kernel(x_ref, g_ref, cos_ref, sin_ref, o_ref, *, n_heads, scale):
    quarter = HEAD_DIM // 4
    lane = lax.broadcasted_iota(jnp.int32, (x_ref.shape[0], HEAD_DIM), 1)
    first = (lane % (2 * quarter)) < quarter
    cos = cos_ref[...]
    sin = sin_ref[...]
    g = g_ref[...]
    for hh in range(n_heads):
        cols = slice(hh * HEAD_DIM, (hh + 1) * HEAD_DIM)
        x = x_ref[:, cols].astype(F32)
        xn = x * lax.rsqrt(jnp.mean(x * x, axis=-1, keepdims=True) + RMS_EPS) * g
        partner = jnp.where(first, pltpu.roll(xn, HEAD_DIM - quarter, 1), pltpu.roll(xn, quarter, 1))
        o_ref[:, cols] = ((xn * cos + partner * sin) * scale).astype(o_ref.dtype)


def _norm_rope(qkv, col_block, n_heads, gain, cos, sin, scale):
    t = qkv.shape[0]
    tm = ROW_TILE
    w = n_heads * HEAD_DIM
    return pl.pallas_call(
        functools.partial(_norm_rope_kernel, n_heads=n_heads, scale=scale),
        grid=(t // tm,),
        in_specs=[
            pl.BlockSpec((tm, w), lambda i: (i, col_block)),
            pl.BlockSpec((1, HEAD_DIM), lambda i: (0, 0)),
            pl.BlockSpec((tm, HEAD_DIM), lambda i: (i, 0)),
            pl.BlockSpec((tm, HEAD_DIM), lambda i: (i, 0)),
        ],
        out_specs=pl.BlockSpec((tm, w), lambda i: (i, 0)),
        out_shape=jax.ShapeDtypeStruct((t, w), BF16),
        compiler_params=_params(1),
        name="norm_rope",
    )(qkv, gain.reshape(1, HEAD_DIM), cos, sin)


def _lane_tiles(x):
    return [x[:, j * LANES:(j + 1) * LANES] for j in range(x.shape[1] // LANES)]


FLASH_SUB_ROWS = 64


def _flash_scores(q_ref, kt, s_sc, slot, rep):
    n = kt.shape[1]
    for r in range(rep):
        s_sc[slot, r, :, :n] = _dot(q_ref[:, r * HEAD_DIM:(r + 1) * HEAD_DIM], kt)


def _flash_update(s_sc, slot, vs, m_sc, acc_sc, rep, tq):
    n = vs.shape[0]
    v_ext = jnp.concatenate([vs, jnp.ones_like(vs)], axis=1)
    for r in range(rep):
        for b in range(tq // FLASH_SUB_ROWS):
            rows = slice(b * FLASH_SUB_ROWS, (b + 1) * FLASH_SUB_ROWS)
            tiles = _lane_tiles(s_sc[slot, r, rows, :n])
            m_prev = m_sc[r, rows]
            m_new = jnp.maximum(m_prev, jnp.max(functools.reduce(jnp.maximum, tiles), axis=-1, keepdims=True))
            alpha = jnp.exp2(m_prev - m_new)
            p = jnp.concatenate([jnp.exp2(t - m_new).astype(BF16) for t in tiles], axis=1)
            acc_sc[r, rows] = jnp.concatenate([alpha, alpha], axis=1) * acc_sc[r, rows] + _dot(p, v_ext)
            m_sc[r, rows] = m_new


def _flash_kernel(q_ref, k_ref, v_ref, o_ref, s_sc, m_sc, acc_sc, *, rep, tq, tk, n_main, n_tail):
    m_sc[...] = jnp.full(m_sc.shape, -jnp.inf, F32)
    acc_sc[...] = jnp.zeros(acc_sc.shape, F32)
    n_pairs = n_main // (2 * tk)

    def body(c, carry):
        base = pl.multiple_of(c * 2 * tk, 2 * tk)
        nxt = pl.multiple_of(jnp.minimum(c + 1, n_pairs - 1) * 2 * tk, 2 * tk)
        _flash_scores(q_ref, k_ref[:, pl.ds(base + tk, tk)], s_sc, 1, rep)
        _flash_update(s_sc, 0, v_ref[pl.ds(base, tk), :], m_sc, acc_sc, rep, tq)
        _flash_scores(q_ref, k_ref[:, pl.ds(nxt, tk)], s_sc, 0, rep)
        _flash_update(s_sc, 1, v_ref[pl.ds(base + tk, tk), :], m_sc, acc_sc, rep, tq)
        return carry

    _flash_scores(q_ref, k_ref[:, pl.ds(0, tk)], s_sc, 0, rep)
    lax.fori_loop(0, n_pairs, body, 0)
    if n_tail:
        _flash_scores(q_ref, k_ref[:, pl.ds(n_main, n_tail)], s_sc, 0, rep)
        _flash_update(s_sc, 0, v_ref[pl.ds(n_main, n_tail), :], m_sc, acc_sc, rep, tq)
    for r in range(rep):
        o_ref[:, r * HEAD_DIM:(r + 1) * HEAD_DIM] = (
            acc_sc[r, :, :HEAD_DIM] / acc_sc[r, :, HEAD_DIM:]).astype(o_ref.dtype)


def _gqa_attention(q, k_t, qkv, v_col_block0, n_q):
    t = k_t.shape[1]
    rep = N_HEADS // N_KV_HEADS
    tq, tk = GQA_TQ, GQA_TK
    n_main = t // (2 * tk) * (2 * tk)
    n_tail = t - n_main
    assert n_q % tq == 0 and n_tail % LANES == 0 and n_tail <= tk
    w = rep * HEAD_DIM
    return pl.pallas_call(
        functools.partial(_flash_kernel, rep=rep, tq=tq, tk=tk, n_main=n_main, n_tail=n_tail),
        grid=(N_KV_HEADS, n_q // tq),
        in_specs=[
            pl.BlockSpec((tq, w), lambda g, i: (i, g)),
            pl.BlockSpec((HEAD_DIM, t), lambda g, i: (g, 0)),
            pl.BlockSpec((t, HEAD_DIM), lambda g, i: (0, v_col_block0 + g)),
        ],
        out_specs=pl.BlockSpec((tq, w), lambda g, i: (i, g)),
        out_shape=jax.ShapeDtypeStruct((n_q, N_HEADS * HEAD_DIM), BF16),
        scratch_shapes=[
            pltpu.VMEM((2, rep, tq, tk), F32),
            pltpu.VMEM((rep, tq, HEAD_DIM), F32),
            pltpu.VMEM((rep, tq, 2 * HEAD_DIM), F32),
        ],
        compiler_params=_params(2),
        name="gqa_flash_attention",
    )(q, k_t, qkv)


def _route(logits, router_bias, block_rows):
    n_tok, n_e = logits.shape
    scores = jax.nn.sigmoid(logits)
    biased = scores + router_bias.astype(F32)
    per_g = n_e // N_EXPERT_GROUPS
    grp = biased.reshape(n_tok, N_EXPERT_GROUPS, per_g)
    g_first, g_mask1 = _first_max(grp)
    g_second, _ = _first_max(jnp.where(g_mask1, -jnp.inf, grp))
    g_score = g_first + g_second
    g_sel = jnp.zeros(g_score.shape, bool)
    for _ in range(TOP_K_GROUPS):
        _, pick = _first_max(jnp.where(g_sel, -jnp.inf, g_score))
        g_sel = g_sel | pick
    e_mask = jnp.repeat(g_sel, per_g, axis=1)
    cand = jnp.where(e_mask, biased, -jnp.inf)
    picks = []
    for _ in range(TOP_K):
        _, pick = _first_max(cand)
        picks.append(pick)
        cand = jnp.where(pick, -jnp.inf, cand)
    onehot = jnp.stack(picks, axis=1)
    w = jnp.sum(jnp.where(onehot, scores[:, None, :], 0.0), axis=-1)
    w = w / jnp.sum(w, axis=-1, keepdims=True) * ROUTED_SCALE

    n_asg = n_tok * TOP_K
    oh = onehot.reshape(n_asg, n_e).astype(jnp.int32)
    csum = jnp.cumsum(oh, axis=0)
    counts = csum[-1]
    padded = (counts + block_rows - 1) // block_rows * block_rows
    pad_end = jnp.cumsum(padded)
    pad_start = pad_end - padded
    dest = jnp.sum(oh * (csum - 1 + pad_start[None, :]), axis=1).astype(jnp.int32)
    n_blocks = -(-(n_asg + n_e * (block_rows - 1)) // block_rows)
    blk_row0 = jnp.arange(n_blocks, dtype=jnp.int32) * block_rows
    block_e = jnp.minimum(jnp.sum((pad_end[None, :] <= blk_row0[:, None]).astype(jnp.int32), axis=1), n_e - 1)
    be_oh = (block_e[:, None] == jnp.arange(n_e, dtype=jnp.int32)[None, :]).astype(jnp.int32)
    seg_left = jnp.sum(be_oh * (counts + pad_start)[None, :], axis=1) - blk_row0
    n_valid = jnp.clip(seg_left, 0, block_rows).astype(jnp.int32)
    codes = _plan_rows(dest, n_blocks * block_rows, block_rows)
    return w, block_e.astype(jnp.int32), n_valid, codes


def _first_max(x):
    m = jnp.max(x, axis=-1, keepdims=True)
    idx = lax.broadcasted_iota(jnp.int32, x.shape, x.ndim - 1)
    first = jnp.min(jnp.where(x == m, idx, x.shape[-1]), axis=-1, keepdims=True)
    return m[..., 0], idx == first


CODE_SLOT_SHIFT = 16


def _plan_kernel(dest_hbm, codes_ref, dest_smem, sem, *, chunk, block_rows):
    j = pl.program_id(0)
    n_rows = codes_ref.shape[0]
    log_rows = block_rows.bit_length() - 1
    log_k = TOP_K.bit_length() - 1

    @pl.when(j == 0)
    def _():
        def init(r, c):
            parity = lax.shift_right_logical(r, log_rows) & 1
            codes_ref[r] = ((r & (block_rows - 1)) + parity * block_rows) | (TOP_K << CODE_SLOT_SHIFT)
            return c

        lax.fori_loop(0, n_rows, init, 0, unroll=16)

    cp = pltpu.make_async_copy(dest_hbm.at[j], dest_smem, sem)
    cp.start()
    cp.wait()

    def place(a, c):
        g = j * chunk + a
        codes_ref[dest_smem[a]] = lax.shift_right_logical(g, log_k) | ((g & (TOP_K - 1)) << CODE_SLOT_SHIFT)
        return c

    lax.fori_loop(0, chunk, place, 0, unroll=16)


def _plan_rows(dest, n_rows, block_rows):
    n_asg = dest.shape[0]
    assert block_rows & (block_rows - 1) == 0 and TOP_K & (TOP_K - 1) == 0
    chunk = max(c for c in range(LANES, 8192 + 1, LANES) if n_asg % c == 0)
    n_chunks = n_asg // chunk
    return pl.pallas_call(
        functools.partial(_plan_kernel, chunk=chunk, block_rows=block_rows),
        grid=(n_chunks,),
        in_specs=[pl.BlockSpec(memory_space=pl.ANY)],
        out_specs=pl.BlockSpec(memory_space=pltpu.SMEM),
        out_shape=jax.ShapeDtypeStruct((n_rows,), jnp.int32),
        scratch_shapes=[pltpu.SMEM((chunk,), jnp.int32), pltpu.SemaphoreType.DMA(())],
        compiler_params=_params(1),
        name="plan_rows",
    )(dest.reshape(n_chunks, chunk))


def _swiglu_rows(x, wgu_ref, wd_ref):
    hgu = _dot(x, wgu_ref[...])
    de = hgu.shape[1] // 2
    gate = hgu[:, :de]
    hb = (gate * jax.nn.sigmoid(gate) * hgu[:, de:]).astype(BF16)
    return _dot(hb, wd_ref[...])


def _expert_kernel(be_ref, nv_ref, codes_ref, t_hbm, wgu_ref, wd_ref, y_hbm,
                   xbuf, ybuf, sem_in, sem_out, *, n_tok, br):
    i = pl.program_id(0)
    nb = pl.num_programs(0)
    slot = i % 2
    active = nv_ref[i] > 0
    prev_active = nv_ref[jnp.maximum(i - 1, 0)] > 0
    token_mask = (1 << CODE_SLOT_SHIFT) - 1

    def gather(block, buf_slot):
        for r in range(br):
            tok = codes_ref[block * br + r] & token_mask
            pltpu.make_async_copy(t_hbm.at[pl.ds(tok, 1)], xbuf.at[buf_slot, pl.ds(r, 1)],
                                  sem_in.at[buf_slot]).start()

    def gather_wait(buf_slot):
        pltpu.make_async_copy(t_hbm.at[pl.ds(0, br)], xbuf.at[buf_slot], sem_in.at[buf_slot]).wait()

    def scatter(block, buf_slot):
        for r in range(br):
            code = codes_ref[block * br + r]
            dst = lax.shift_right_logical(code, CODE_SLOT_SHIFT) * n_tok + (code & token_mask)
            pltpu.make_async_copy(ybuf.at[buf_slot, pl.ds(r, 1)], y_hbm.at[pl.ds(dst, 1)],
                                  sem_out.at[buf_slot]).start()

    def scatter_wait(buf_slot):
        pltpu.make_async_copy(ybuf.at[buf_slot], y_hbm.at[pl.ds(0, br)], sem_out.at[buf_slot]).wait()

    @pl.when(i == 0)
    def _():
        gather(0, 0)

    @pl.when(active)
    def _():
        gather(jnp.minimum(i + 1, nb - 1), 1 - slot)
        gather_wait(slot)
        ybuf[slot] = _swiglu_rows(xbuf[slot].astype(BF16), wgu_ref, wd_ref)
        scatter(i, slot)

        @pl.when(i > 0)
        def _():
            scatter_wait(1 - slot)

        @pl.when(i == nb - 1)
        def _():
            gather_wait(1 - slot)
            scatter_wait(slot)

    @pl.when(jnp.logical_not(active) & ((i == 0) | prev_active))
    def _():
        gather_wait(slot)

        @pl.when(i > 0)
        def _():
            scatter_wait(1 - slot)


def _routed_experts(tok_f32, block_e, n_valid, codes, w_gu, w_down):
    n_tok, d = tok_f32.shape
    n_blocks = block_e.shape[0]
    br = codes.shape[0] // n_blocks
    de2 = w_gu.shape[2]
    grid_spec = pltpu.PrefetchScalarGridSpec(
        num_scalar_prefetch=3,
        grid=(n_blocks,),
        in_specs=[
            pl.BlockSpec(memory_space=pl.ANY),
            pl.BlockSpec((None, d, de2), lambda i, be, nv, cd: (be[i], 0, 0)),
            pl.BlockSpec((None, de2 // 2, d), lambda i, be, nv, cd: (be[i], 0, 0)),
        ],
        out_specs=pl.BlockSpec(memory_space=pl.ANY),
        scratch_shapes=[
            pltpu.VMEM((2, br, d), F32),
            pltpu.VMEM((2, br, d), F32),
            pltpu.SemaphoreType.DMA((2,)),
            pltpu.SemaphoreType.DMA((2,)),
        ],
    )
    return pl.pallas_call(
        functools.partial(_expert_kernel, n_tok=n_tok, br=br),
        grid_spec=grid_spec,
        out_shape=jax.ShapeDtypeStruct((TOP_K * n_tok + 2 * br, d), F32),
        compiler_params=_params(1),
        name="routed_experts",
    )(block_e, n_valid, codes, tok_f32, w_gu, w_down)


def _shared_kernel(x_ref, wgu_ref, wd_ref, o_ref):
    o_ref[...] = _swiglu_rows(x_ref[...].astype(BF16), wgu_ref, wd_ref).astype(o_ref.dtype)


def _shared_expert(tok_f32, w_gu, w_down):
    t, d = tok_f32.shape
    tm = ROW_TILE
    de2 = w_gu.shape[1]
    return pl.pallas_call(
        _shared_kernel,
        grid=(t // tm,),
        in_specs=[
            pl.BlockSpec((tm, d), lambda i: (i, 0)),
            pl.BlockSpec((d, de2), lambda i: (0, 0)),
            pl.BlockSpec((de2 // 2, d), lambda i: (0, 0)),
        ],
        out_specs=pl.BlockSpec((tm, d), lambda i: (i, 0)),
        out_shape=jax.ShapeDtypeStruct((t, d), F32),
        compiler_params=_params(1),
        name="shared_expert",
    )(tok_f32, w_gu, w_down)


def _combine_kernel(*refs, with_h):
    x_ref, sh_ref, y0_ref, y1_ref, y2_ref, y3_ref, w_ref, gate_ref, g_ref, b_ref = refs[:10]
    w = w_ref[...]
    y = sh_ref[...]
    for k, y_ref in enumerate((y0_ref, y1_ref, y2_ref, y3_ref)):
        y = y + w[:, k:k + 1] * y_ref[...]
    z = DEEPNORM_ALPHA * x_ref[...] + gate_ref[...] * y
    xn = _layer_norm_rows(z, g_ref[...], b_ref[...])
    if with_h:
        sc_ref, shf_ref, xo_ref, h_ref = refs[10:14]
        xo_ref[...] = xn
        h_ref[...] = (xn * (1.0 + sc_ref[...]) + shf_ref[...]).astype(h_ref.dtype)
    else:
        xo_ref = refs[10]
        xo_ref[...] = xn


def _moe_combine(x, shared, y_slots, w, mod, gate_slot, ln_g, ln_b, n_lat, *, next_mod=None,
                 n_out_rows=None):
    t, d = x.shape
    n_out_rows = t if n_out_rows is None else n_out_rows
    tm = COMBINE_ROWS
    nlb = n_lat // tm
    with_h = next_mod is not None
    y3 = y_slots
    row = pl.BlockSpec((tm, d), lambda i: (i, 0))
    vec = pl.BlockSpec((1, d), lambda i: (0, 0))

    def slot_spec(k):
        return pl.BlockSpec((tm, d), lambda i: (k * (t // tm) + i, 0))

    in_specs = [row, row] + [slot_spec(k) for k in range(TOP_K)] + [
        pl.BlockSpec((tm, TOP_K), lambda i: (i, 0)),
        _mod_spec(gate_slot, nlb, d), vec, vec,
    ]
    args = [x, shared, y3, y3, y3, y3, w, mod, ln_g.reshape(1, d), ln_b.reshape(1, d)]
    out_specs = [row]
    out_shape = [jax.ShapeDtypeStruct((n_out_rows, d), F32)]
    if with_h:
        in_specs += [_mod_spec(1, nlb, d), _mod_spec(0, nlb, d)]
        args += [next_mod, next_mod]
        out_specs.append(row)
        out_shape.append(jax.ShapeDtypeStruct((n_out_rows, d), BF16))
    return pl.pallas_call(
        functools.partial(_combine_kernel, with_h=with_h),
        grid=(n_out_rows // tm,),
        in_specs=in_specs,
        out_specs=out_specs,
        out_shape=out_shape,
        compiler_params=_params(1),
        name="moe_combine",
    )(*args)


def kernel(x, c, ctx, c_ctx, w_mod_down, w_mod_up, b_mod, ln_g, ln_b, gm_w_in, gm_v_g, gm_v_b, gm_w_s, gm_b_s, gm_w_out, na_w_qkv, na_rpb, na_w_out, ft_w_out, ga_w_qkv, ga_q_g, ga_k_g, ga_w_out, router_w, router_bias, exp_w_gate, exp_w_up, exp_w_down, sh_w_gate, sh_w_up, sh_w_down):
    b, n, d = x.shape
    nc = ctx.shape[1]
    assert b == 1 and c.shape[0] == 1
    n_tok = n + nc
    rows = n // GRID_W

    xs = jnp.concatenate([x[0], ctx[0]], axis=0)
    cond8 = jnp.zeros((8, d), F32).at[0].set(c[0]).at[1].set(c_ctx)
    mods = _ada_mod_all(cond8, w_mod_down, w_mod_up, b_mod)
    router_w_pad = jnp.pad(router_w, ((0, 0), (0, 0), (0, LANES - N_EXPERTS)))

    h = _modulate(xs, mods[0], 1, 0, n, BF16)
    for i in range(DEPTH):
        kind = i % 4
        mod = mods[i]
        last = i == DEPTH - 1

        if kind == 0:
            uv = _matmul(h, gm_w_in[0].astype(BF16), out_dtype=BF16, act="gelu")
            gated = _gmlp_gate(uv, gm_v_g[0], gm_v_b[0], gm_w_s[0], gm_b_s[0])
            o = _matmul(gated, gm_w_out[0].astype(BF16), out_dtype=F32)
        elif kind == 1:
            qkv = _matmul(h, na_w_qkv[0].astype(BF16), out_dtype=BF16)
            a_lat = _na_attention(qkv, _na_bias_table(na_rpb[0], rows), n, nc)
            a_ctx = _ctx_attention(qkv, n, nc)
            o = _matmul(jnp.concatenate([a_lat, a_ctx], axis=0), na_w_out[0].astype(BF16), out_dtype=F32)
        elif kind == 2:
            p = jnp.concatenate([_ft_positions_latent(h[:n]), _ft_positions_dense(h[n:])], axis=1)
            o = _matmul(_ft_channels(p), ft_w_out[0].astype(BF16), out_dtype=F32)
        else:
            qkv = _matmul(h, ga_w_qkv[0].astype(BF16), out_dtype=BF16, tn=1024)
            cos, sin = _rope_tables(n, nc)
            qn = _norm_rope(qkv, 0, N_HEADS, ga_q_g[0], cos, sin, HEAD_DIM ** -0.5 * math.log2(math.e))
            kn = _norm_rope(qkv, N_HEADS // N_KV_HEADS, N_KV_HEADS, ga_k_g[0], cos, sin, 1.0)
            att = _gqa_attention(qn, kn.T, qkv, (N_HEADS + N_KV_HEADS), n)
            att = jnp.concatenate([att, jnp.zeros((nc, d), BF16)], axis=0)
            o = _matmul(att, ga_w_out[0].astype(BF16), out_dtype=F32)

        xs, h2, logits = _resln(xs, o, mod, 2, ln_g[i, 0], ln_b[i, 0], n, next_mod=mod,
                                scale_slot=4, shift_slot=3, h_dtype=F32, router_w=router_w_pad[i])

        w, block_e, n_valid, codes = _route(logits[:, :N_EXPERTS], router_bias[i], MOE_ROWS)
        w_gu = jnp.concatenate([exp_w_gate[i], exp_w_up[i]], axis=-1).astype(BF16)
        y_slots = _routed_experts(h2, block_e, n_valid, codes, w_gu, exp_w_down[i].astype(BF16))
        s_gu = jnp.concatenate([sh_w_gate[i], sh_w_up[i]], axis=-1).astype(BF16)
        shared = _shared_expert(h2, s_gu, sh_w_down[i].astype(BF16))
        if last:
            (xs,) = _moe_combine(xs, shared, y_slots, w, mod, 5, ln_g[i, 1], ln_b[i, 1], n,
                                 n_out_rows=n)
        else:
            xs, h = _moe_combine(xs, shared, y_slots, w, mod, 5, ln_g[i, 1], ln_b[i, 1], n,
                                 next_mod=mods[i + 1])
    return xs.reshape(b, n, d)
```

```python
import functools
import math

import jax
import jax.numpy as jnp
import numpy as np
from jax import lax
from jax.experimental import pallas as pl
from jax.experimental.pallas import tpu as pltpu

F32 = jnp.float32
BF16 = jnp.bfloat16

D_MODEL = 4096
DEPTH = 4
GRID_W = 64
N_MOD = 6
LN_EPS = 1e-5
RMS_EPS = 1e-6
DEEPNORM_ALPHA = (2 * DEPTH) ** 0.25
GM_CHUNK = 128
GM_GROUPS = 16
N_HEADS = 32
HEAD_DIM = 128
N_KV_HEADS = 8
ROPE_THETA = 10000.0
NA_WIN_H = 8
NA_WIN_W = 16
FT_GROUPS = 8
N_EXPERTS = 32
N_EXPERT_GROUPS = 4
TOP_K_GROUPS = 2
TOP_K = 4
D_EXPERT = 384
ROUTED_SCALE = 2.5

LANES = 128
VMEM_LIMIT_BYTES = 56 * 1024 * 1024
ROW_TILE = 256
MM_TILE_M = 1280
MM_TILE_N = 1024
MOE_ROWS = 256
COMBINE_ROWS = 128
NA_Q_ROWS = 4
NA_KEY_ROWS = 12
GQA_TQ = 256
GQA_TK = 2048
MASK_VALUE = -1e30


def _params(n_axes):
    return pltpu.CompilerParams(
        dimension_semantics=("arbitrary",) * n_axes,
        vmem_limit_bytes=VMEM_LIMIT_BYTES,
    )


def _dot(a, b, precision=None):
    return jnp.dot(a, b, preferred_element_type=F32, precision=precision)


def _dot_nt(a, b):
    return lax.dot_general(a, b, (((1,), (1,)), ((), ())), preferred_element_type=F32)


def _mm_kernel(a_ref, b_ref, o_ref, *, act):
    acc = _dot(a_ref[...], b_ref[...])
    if act == "gelu":
        acc = jax.nn.gelu(acc)
    o_ref[...] = acc.astype(o_ref.dtype)


def _matmul(a, b, *, out_dtype, act=None, tm=MM_TILE_M, tn=MM_TILE_N):
    m, k = a.shape
    n = b.shape[1]
    tm = min(tm, m)
    tn = min(tn, n)
    if jnp.dtype(out_dtype).itemsize == 4:
        tn = min(tn, MM_TILE_N // 2)
    while n % tn:
        tn //= 2
    assert m % tm == 0 and tn % LANES == 0
    return pl.pallas_call(
        functools.partial(_mm_kernel, act=act),
        grid=(m // tm, n // tn),
        in_specs=[
            pl.BlockSpec((tm, k), lambda i, j: (i, 0)),
            pl.BlockSpec((k, tn), lambda i, j: (0, j)),
        ],
        out_specs=pl.BlockSpec((tm, tn), lambda i, j: (i, j)),
        out_shape=jax.ShapeDtypeStruct((m, n), out_dtype),
        compiler_params=_params(2),
        name="matmul",
    )(a, b)


def _lmm_kernel(m_ref, x_ref, o_ref):
    o_ref[...] = _dot(m_ref[...], x_ref[...]).astype(o_ref.dtype)


def _left_matmul(mat, x, *, out_dtype, tn):
    r, k = mat.shape
    n = x.shape[1]
    assert n % tn == 0
    return pl.pallas_call(
        _lmm_kernel,
        grid=(n // tn,),
        in_specs=[
            pl.BlockSpec((r, k), lambda j: (0, 0)),
            pl.BlockSpec((k, tn), lambda j: (0, j)),
        ],
        out_specs=pl.BlockSpec((r, tn), lambda j: (0, j)),
        out_shape=jax.ShapeDtypeStruct((r, n), out_dtype),
        compiler_params=_params(1),
        name="left_matmul",
    )(mat, x)


def _mod_kernel(cond_ref, wd_ref, wu_ref, b_ref, o_ref):
    cond = cond_ref[...]
    t = _dot(cond * jax.nn.sigmoid(cond), wd_ref[...], precision=lax.Precision.HIGHEST)
    o_ref[...] = _dot(t, wu_ref[...], precision=lax.Precision.HIGHEST) + b_ref[...]


def _ada_mod_all(cond8, w_mod_down, w_mod_up, b_mod):
    depth, d, rank = w_mod_down.shape
    out = pl.pallas_call(
        _mod_kernel,
        grid=(depth, N_MOD),
        in_specs=[
            pl.BlockSpec((8, d), lambda l, j: (0, 0)),
            pl.BlockSpec((None, d, rank), lambda l, j: (l, 0, 0)),
            pl.BlockSpec((None, rank, d), lambda l, j: (l, 0, j)),
            pl.BlockSpec((None, None, 1, d), lambda l, j: (l, j, 0, 0)),
        ],
        out_specs=pl.BlockSpec((None, None, 8, d), lambda l, j: (l, j, 0, 0)),
        out_shape=jax.ShapeDtypeStruct((depth, N_MOD, 8, d), F32),
        compiler_params=_params(2),
        name="ada_mod",
    )(cond8, w_mod_down, w_mod_up, b_mod.reshape(depth, N_MOD, 1, d))
    return jnp.transpose(out[:, :, :2], (0, 2, 1, 3))[:, :, :, None, :]


def _mod_spec(slot, n_lat_blocks, d):
    return pl.BlockSpec((None, None, 1, d), lambda i: (i // n_lat_blocks, slot, 0, 0))


def _modulate_kernel(x_ref, sc_ref, sh_ref, h_ref):
    h_ref[...] = (x_ref[...] * (1.0 + sc_ref[...]) + sh_ref[...]).astype(h_ref.dtype)


def _modulate(x, mod, scale_slot, shift_slot, n_lat, out_dtype):
    t, d = x.shape
    tm = ROW_TILE
    nlb = n_lat // tm
    return pl.pallas_call(
        _modulate_kernel,
        grid=(t // tm,),
        in_specs=[
            pl.BlockSpec((tm, d), lambda i: (i, 0)),
            _mod_spec(scale_slot, nlb, d),
            _mod_spec(shift_slot, nlb, d),
        ],
        out_specs=pl.BlockSpec((tm, d), lambda i: (i, 0)),
        out_shape=jax.ShapeDtypeStruct((t, d), out_dtype),
        compiler_params=_params(1),
        name="modulate",
    )(x, mod, mod)


def _layer_norm_rows(z, g, b):
    mu = jnp.mean(z, axis=-1, keepdims=True)
    zc = z - mu
    var = jnp.mean(zc * zc, axis=-1, keepdims=True)
    return zc * lax.rsqrt(var + LN_EPS) * g + b


def _resln_kernel(*refs, with_h, with_router):
    x_ref, o_ref, gate_ref, g_ref, b_ref = refs[:5]
    pos = 5
    if with_h:
        sc_ref, sh_ref = refs[pos:pos + 2]
        pos += 2
    if with_router:
        rw_ref = refs[pos]
        pos += 1
    xo_ref = refs[pos]
    pos += 1
    z = DEEPNORM_ALPHA * x_ref[...] + gate_ref[...] * o_ref[...].astype(F32)
    xn = _layer_norm_rows(z, g_ref[...], b_ref[...])
    xo_ref[...] = xn
    if with_h:
        h_ref = refs[pos]
        pos += 1
        h = xn * (1.0 + sc_ref[...]) + sh_ref[...]
        h_ref[...] = h.astype(h_ref.dtype)
        if with_router:
            lg_ref = refs[pos]
            lg_ref[...] = _dot(h, rw_ref[...], precision=lax.Precision.HIGHEST)


def _resln(x, o, mod, gate_slot, ln_g, ln_b, n_lat, *, next_mod=None, scale_slot=None,
           shift_slot=None, h_dtype=None, router_w=None, n_out_rows=None):
    t, d = x.shape
    n_out_rows = t if n_out_rows is None else n_out_rows
    tm = ROW_TILE
    nlb = n_lat // tm
    with_h = next_mod is not None
    with_router = router_w is not None
    row = pl.BlockSpec((tm, d), lambda i: (i, 0))
    vec = pl.BlockSpec((1, d), lambda i: (0, 0))
    in_specs = [row, row, _mod_spec(gate_slot, nlb, d), vec, vec]
    args = [x, o, mod, ln_g.reshape(1, d), ln_b.reshape(1, d)]
    out_specs = [row]
    out_shape = [jax.ShapeDtypeStruct((n_out_rows, d), F32)]
    if with_h:
        in_specs += [_mod_spec(scale_slot, nlb, d), _mod_spec(shift_slot, nlb, d)]
        args += [next_mod, next_mod]
        out_specs.append(row)
        out_shape.append(jax.ShapeDtypeStruct((n_out_rows, d), h_dtype))
    if with_router:
        ne = router_w.shape[1]
        in_specs.append(pl.BlockSpec((d, ne), lambda i: (0, 0)))
        args.append(router_w)
        out_specs.append(pl.BlockSpec((tm, ne), lambda i: (i, 0)))
        out_shape.append(jax.ShapeDtypeStruct((n_out_rows, ne), F32))
    return pl.pallas_call(
        functools.partial(_resln_kernel, with_h=with_h, with_router=with_router),
        grid=(n_out_rows // tm,),
        in_specs=in_specs,
        out_specs=out_specs,
        out_shape=out_shape,
        compiler_params=_params(1),
        name="residual_layernorm",
    )(*args)


def _gmlp_gate_kernel(u_ref, v_ref, vg_ref, vb_ref, ws_ref, bs_ref, o_ref, *, n_groups, gw):
    vn = _layer_norm_rows(v_ref[...].astype(F32), vg_ref[...], vb_ref[...]).astype(BF16)
    for g in range(n_groups):
        cols = slice(g * gw, (g + 1) * gw)
        sv = _dot(ws_ref[g], vn[:, cols]) + bs_ref[:, cols]
        o_ref[:, cols] = (u_ref[:, cols].astype(F32) * sv).astype(o_ref.dtype)


def _gmlp_gate(uv, v_g, v_b, w_s, b_s):
    t, two_w = uv.shape
    width = two_w // 2
    n_groups, chunk, _ = w_s.shape
    gw = width // n_groups
    bias_full = jnp.repeat(b_s.T, gw, axis=1).astype(F32)
    return pl.pallas_call(
        functools.partial(_gmlp_gate_kernel, n_groups=n_groups, gw=gw),
        grid=(t // chunk,),
        in_specs=[
            pl.BlockSpec((chunk, width), lambda i: (i, 0)),
            pl.BlockSpec((chunk, width), lambda i: (i, 1)),
            pl.BlockSpec((1, width), lambda i: (0, 0)),
            pl.BlockSpec((1, width), lambda i: (0, 0)),
            pl.BlockSpec((n_groups, chunk, chunk), lambda i: (0, 0, 0)),
            pl.BlockSpec((chunk, width), lambda i: (0, 0)),
        ],
        out_specs=pl.BlockSpec((chunk, width), lambda i: (i, 0)),
        out_shape=jax.ShapeDtypeStruct((t, width), BF16),
        compiler_params=_params(1),
        name="gmlp_gate",
    )(uv, uv, v_g.reshape(1, width), v_b.reshape(1, width), w_s.astype(BF16), bias_full)


def _na_bias_table(rpb, rows):
    nb = rows // NA_Q_ROWS
    h = rpb.shape[0]
    n_dr = 2 * NA_WIN_H - 1
    c = np.arange(GRID_W)
    dc = c[None, :] - c[:, None] + NA_WIN_W - 1
    cs = np.clip(c - NA_WIN_W // 2, 0, GRID_W - NA_WIN_W)
    ok_c = (c[None, :] >= cs[:, None]) & (c[None, :] < cs[:, None] + NA_WIN_W)
    sel = ((np.arange(2 * NA_WIN_W - 1)[:, None, None] == dc[None]) & ok_c[None]).astype(np.float32)
    planes = jnp.einsum("hrd,dqk->hrqk", rpb.astype(F32), jnp.asarray(sel), precision=lax.Precision.HIGHEST)
    planes = jnp.where(ok_c[None, None], planes, MASK_VALUE)
    planes = jnp.concatenate([planes, jnp.full((h, 1, GRID_W, GRID_W), MASK_VALUE, F32)], axis=1)
    tables = []
    for b in (0, 1, nb - 1):
        r0 = b * NA_Q_ROWS
        kw0 = int(np.clip(r0 - NA_WIN_H // 2, 0, rows - NA_KEY_ROWS))
        r = r0 + np.arange(NA_Q_ROWS)[:, None]
        kr = kw0 + np.arange(NA_KEY_ROWS)[None, :]
        rs = np.clip(r - NA_WIN_H // 2, 0, rows - NA_WIN_H)
        ok_r = (kr >= rs) & (kr < rs + NA_WIN_H)
        dr = np.where(ok_r, kr - r + NA_WIN_H - 1, n_dr)
        tbl = jnp.stack([planes[:, int(d)] for d in dr.reshape(-1)], axis=1)
        tbl = tbl.reshape(h, NA_Q_ROWS, NA_KEY_ROWS, GRID_W, GRID_W).transpose(0, 1, 3, 2, 4)
        tables.append(tbl.reshape(h, NA_Q_ROWS * GRID_W, NA_KEY_ROWS * GRID_W))
    return jnp.stack(tables, axis=0)


def _na_kernel(q_ref, k_ref, v_ref, bias_ref, o_ref, *, rows, n_lat, n_ctx):
    b = pl.program_id(1)
    kw0 = jnp.clip(b * NA_Q_ROWS - NA_WIN_H // 2, 0, rows - NA_KEY_ROWS)
    start = pl.multiple_of(kw0 * GRID_W, GRID_W)
    nk = NA_KEY_ROWS * GRID_W
    scale = HEAD_DIM ** -0.5
    q = q_ref[...]
    kw = k_ref[pl.ds(start, nk), :]
    vw = v_ref[pl.ds(start, nk), :]
    kc = k_ref[pl.ds(n_lat, n_ctx), :]
    vc = v_ref[pl.ds(n_lat, n_ctx), :]
    s_loc = _dot_nt(q, kw) * scale + bias_ref[...]
    s_ctx = _dot_nt(q, kc) * scale
    m = jnp.maximum(jnp.max(s_loc, axis=-1, keepdims=True), jnp.max(s_ctx, axis=-1, keepdims=True))
    p_loc = jnp.exp(s_loc - m)
    p_ctx = jnp.exp(s_ctx - m)
    l = jnp.sum(p_loc, axis=-1, keepdims=True) + jnp.sum(p_ctx, axis=-1, keepdims=True)
    o = _dot(p_loc.astype(BF16), vw) + _dot(p_ctx.astype(BF16), vc)
    o_ref[...] = (o / l).astype(o_ref.dtype)


def _na_attention(qkv, bias_table, n_lat, n_ctx):
    t = qkv.shape[0]
    rows = n_lat // GRID_W
    nb = rows // NA_Q_ROWS
    tq = NA_Q_ROWS * GRID_W
    nk = NA_KEY_ROWS * GRID_W
    h = N_HEADS
    return pl.pallas_call(
        functools.partial(_na_kernel, rows=rows, n_lat=n_lat, n_ctx=n_ctx),
        grid=(h, nb),
        in_specs=[
            pl.BlockSpec((tq, HEAD_DIM), lambda hh, b: (b, hh)),
            pl.BlockSpec((t, HEAD_DIM), lambda hh, b: (0, h + hh)),
            pl.BlockSpec((t, HEAD_DIM), lambda hh, b: (0, 2 * h + hh)),
            pl.BlockSpec((None, None, tq, nk),
                         lambda hh, b: (jnp.where(b == 0, 0, jnp.where(b == nb - 1, 2, 1)), hh, 0, 0)),
        ],
        out_specs=pl.BlockSpec((tq, HEAD_DIM), lambda hh, b: (b, hh)),
        out_shape=jax.ShapeDtypeStruct((n_lat, h * HEAD_DIM), BF16),
        compiler_params=_params(2),
        name="neighbourhood_attention",
    )(qkv, qkv, qkv, bias_table)


def _ctx_attn_kernel(q_ref, k_ref, v_ref, o_ref):
    s = _dot_nt(q_ref[...], k_ref[...]) * (HEAD_DIM ** -0.5)
    m = jnp.max(s, axis=-1, keepdims=True)
    p = jnp.exp(s - m)
    l = jnp.sum(p, axis=-1, keepdims=True)
    o_ref[...] = (_dot(p.astype(BF16), v_ref[...]) / l).astype(o_ref.dtype)


def _ctx_attention(qkv, n_lat, n_ctx):
    h = N_HEADS
    rb = n_lat // n_ctx
    return pl.pallas_call(
        _ctx_attn_kernel,
        grid=(h,),
        in_specs=[
            pl.BlockSpec((n_ctx, HEAD_DIM), lambda hh: (rb, hh)),
            pl.BlockSpec((n_ctx, HEAD_DIM), lambda hh: (rb, h + hh)),
            pl.BlockSpec((n_ctx, HEAD_DIM), lambda hh: (rb, 2 * h + hh)),
        ],
        out_specs=pl.BlockSpec((n_ctx, HEAD_DIM), lambda hh: (0, hh)),
        out_shape=jax.ShapeDtypeStruct((n_ctx, h * HEAD_DIM), BF16),
        compiler_params=_params(1),
        name="context_attention",
    )(qkv, qkv, qkv)


def _dft_cos_sin(k_times_n, period):
    ang = (2.0 * math.pi / period) * (k_times_n % period).astype(F32)
    return jnp.cos(ang), -jnp.sin(ang)


def _ft_stage2_kernel(g_ref, y_ref, o_ref, *, nb):
    for i in range(nb):
        rhs = jnp.concatenate([y_ref[0, i], y_ref[1, i]], axis=0)
        res = _dot(g_ref[i], rhs)
        half = res.shape[0] // 2
        o_ref[0, :, i, :] = res[:half].astype(o_ref.dtype)
        o_ref[1, :, i, :] = res[half:].astype(o_ref.dtype)


def _ft_positions_latent(h_lat):
    n, d = h_lat.shape
    l1 = l2 = int(round(math.sqrt(n)))
    assert l1 * l2 == n
    idx = jnp.arange(l1, dtype=jnp.int32)
    re, im = _dft_cos_sin(idx[:, None] * idx[None, :], l1)
    m1 = (jnp.concatenate([re, im], axis=0) * (l1 ** -0.5)).astype(BF16)
    y = _left_matmul(m1, h_lat.reshape(l1, l2 * d), out_dtype=BF16, tn=8192)
    y = y.reshape(2, l1, l2, d)
    k1 = idx[:, None, None]
    k2 = idx[None, :, None]
    n2 = idx[None, None, :]
    gr, gi = _dft_cos_sin(n2 * k2 * l1 + n2 * k1, n)
    g = jnp.concatenate(
        [jnp.concatenate([gr, -gi], axis=2), jnp.concatenate([gi, gr], axis=2)], axis=1
    )
    g = (g * (l2 ** -0.5)).astype(BF16)
    nb, cb = 8, min(1024, d)
    p = pl.pallas_call(
        functools.partial(_ft_stage2_kernel, nb=nb),
        grid=(l1 // nb, d // cb),
        in_specs=[
            pl.BlockSpec((nb, 2 * l2, 2 * l2), lambda a, c: (a, 0, 0)),
            pl.BlockSpec((2, nb, l2, cb), lambda a, c: (0, a, 0, c)),
        ],
        out_specs=pl.BlockSpec((2, l2, nb, cb), lambda a, c: (0, 0, a, c)),
        out_shape=jax.ShapeDtypeStruct((2, l2, l1, d), BF16),
        compiler_params=_params(2),
        name="ft_stage2",
    )(g, y)
    return p.reshape(2, n, d)


def _ft_positions_dense(h_ctx):
    n, d = h_ctx.shape
    idx = jnp.arange(n, dtype=jnp.int32)
    re, im = _dft_cos_sin(idx[:, None] * idx[None, :], n)
    m = (jnp.concatenate([re, im], axis=0) * (n ** -0.5)).astype(BF16)
    return _left_matmul(m, h_ctx, out_dtype=BF16, tn=min(d, 2048)).reshape(2, n, d)


def _ft_channel_kernel(pr_ref, pi_ref, c_ref, s_ref, o_ref):
    o_ref[...] = (_dot(pr_ref[...], c_ref[...]) + _dot(pi_ref[...], s_ref[...])).astype(o_ref.dtype)


def _ft_channels(p):
    _, t, d = p.shape
    gc = d // FT_GROUPS
    idx = jnp.arange(gc, dtype=jnp.int32)
    re, im = _dft_cos_sin(idx[:, None] * idx[None, :], gc)
    cc = (re * (gc ** -0.5)).astype(BF16)
    sc = (-im * (gc ** -0.5)).astype(BF16)
    tm = MM_TILE_M if t % MM_TILE_M == 0 else t
    return pl.pallas_call(
        _ft_channel_kernel,
        grid=(t // tm, FT_GROUPS),
        in_specs=[
            pl.BlockSpec((None, tm, gc), lambda i, g: (0, i, g)),
            pl.BlockSpec((None, tm, gc), lambda i, g: (1, i, g)),
            pl.BlockSpec((gc, gc), lambda i, g: (0, 0)),
            pl.BlockSpec((gc, gc), lambda i, g: (0, 0)),
        ],
        out_specs=pl.BlockSpec((tm, gc), lambda i, g: (i, g)),
        out_shape=jax.ShapeDtypeStruct((t, d), BF16),
        compiler_params=_params(2),
        name="ft_channels",
    )(p, p, cc, sc)


def _rope_tables(n_lat, n_ctx):
    half = HEAD_DIM // 2
    t = jnp.arange(n_lat)
    inv = ROPE_THETA ** (-jnp.arange(0, half, 2, dtype=F32) / half)
    ang_r = (t // GRID_W).astype(F32)[:, None] * inv[None, :]
    ang_c = (t % GRID_W).astype(F32)[:, None] * inv[None, :]
    cos = jnp.concatenate([jnp.cos(ang_r)] * 2 + [jnp.cos(ang_c)] * 2, axis=-1)
    sin = jnp.concatenate([-jnp.sin(ang_r), jnp.sin(ang_r), -jnp.sin(ang_c), jnp.sin(ang_c)], axis=-1)
    cos = jnp.concatenate([cos, jnp.ones((n_ctx, HEAD_DIM), F32)], axis=0)
    sin = jnp.concatenate([sin, jnp.zeros((n_ctx, HEAD_DIM), F32)], axis=0)
    return cos, sin


def _norm_rope_kernel(x_ref, g_ref, cos_ref, sin_ref, o_ref, *, n_heads, scale):
    quarter = HEAD_DIM // 4
    lane = lax.broadcasted_iota(jnp.int32, (x_ref.shape[0], HEAD_DIM), 1)
    first = (lane % (2 * quarter)) < quarter
    cos = cos_ref[...]
    sin = sin_ref[...]
    g = g_ref[...]
    for hh in range(n_heads):
        cols = slice(hh * HEAD_DIM, (hh + 1) * HEAD_DIM)
        x = x_ref[:, cols].astype(F32)
        xn = x * lax.rsqrt(jnp.mean(x * x, axis=-1, keepdims=True) + RMS_EPS) * g
        partner = jnp.where(first, pltpu.roll(xn, HEAD_DIM - quarter, 1), pltpu.roll(xn, quarter, 1))
        o_ref[:, cols] = ((xn * cos + partner * sin) * scale).astype(o_ref.dtype)


def _norm_rope(qkv, col_block, n_heads, gain, cos, sin, scale):
    t = qkv.shape[0]
    tm = ROW_TILE
    w = n_heads * HEAD_DIM
    return pl.pallas_call(
        functools.partial(_norm_rope_kernel, n_heads=n_heads, scale=scale),
        grid=(t // tm,),
        in_specs=[
            pl.BlockSpec((tm, w), lambda i: (i, col_block)),
            pl.BlockSpec((1, HEAD_DIM), lambda i: (0, 0)),
            pl.BlockSpec((tm, HEAD_DIM), lambda i: (i, 0)),
            pl.BlockSpec((tm, HEAD_DIM), lambda i: (i, 0)),
        ],
        out_specs=pl.BlockSpec((tm, w), lambda i: (i, 0)),
        out_shape=jax.ShapeDtypeStruct((t, w), BF16),
        compiler_params=_params(1),
        name="norm_rope",
    )(qkv, gain.reshape(1, HEAD_DIM), cos, sin)


def _lane_tiles(x):
    return [x[:, j * LANES:(j + 1) * LANES] for j in range(x.shape[1] // LANES)]


FLASH_SUB_ROWS = 64


def _flash_scores(q_sc, kt, s_sc, slot):
    s_sc[slot, :, :kt.shape[1]] = _dot(q_sc[...], kt)


def _flash_softmax(s_sc, p_sc, alpha_sc, m_sc, slot, n):
    for b in range(s_sc.shape[1] // FLASH_SUB_ROWS):
        rows = slice(b * FLASH_SUB_ROWS, (b + 1) * FLASH_SUB_ROWS)
        tiles = _lane_tiles(s_sc[slot, rows, :n])
        m_prev = m_sc[rows]
        m_new = jnp.maximum(m_prev, jnp.max(functools.reduce(jnp.maximum, tiles), axis=-1, keepdims=True))
        alpha_sc[slot, rows] = jnp.exp2(m_prev - m_new)
        p_sc[slot, rows, :n] = jnp.concatenate([jnp.exp2(t - m_new).astype(BF16) for t in tiles], axis=1)
        m_sc[rows] = m_new


def _flash_accumulate(p_sc, alpha_sc, acc_sc, slot, vs):
    n = vs.shape[0]
    v_ext = jnp.concatenate([vs, jnp.ones_like(vs)], axis=1)
    alpha = alpha_sc[slot]
    acc_sc[...] = jnp.concatenate([alpha, alpha], axis=1) * acc_sc[...] + _dot(p_sc[slot, :, :n], v_ext)


def _flash_kernel(q_ref, k_ref, v_ref, o_ref, q_sc, s_sc, p_sc, alpha_sc, m_sc, acc_sc, *,
                  rep, tq, tk, n_main, n_tail):
    for r in range(rep):
        q_sc[r * tq:(r + 1) * tq, :] = q_ref[:, r * HEAD_DIM:(r + 1) * HEAD_DIM]
    m_sc[...] = jnp.full(m_sc.shape, -jnp.inf, F32)
    acc_sc[...] = jnp.zeros(acc_sc.shape, F32)
    n_pairs = n_main // (2 * tk)

    def body(c, carry):
        base = pl.multiple_of(c * 2 * tk, 2 * tk)
        nxt = pl.multiple_of(jnp.minimum(c + 1, n_pairs - 1) * 2 * tk, 2 * tk)
        _flash_scores(q_sc, k_ref[:, pl.ds(base + tk, tk)], s_sc, 1)
        _flash_softmax(s_sc, p_sc, alpha_sc, m_sc, 0, tk)
        _flash_accumulate(p_sc, alpha_sc, acc_sc, 0, v_ref[pl.ds(base, tk), :])
        _flash_scores(q_sc, k_ref[:, pl.ds(nxt, tk)], s_sc, 0)
        _flash_softmax(s_sc, p_sc, alpha_sc, m_sc, 1, tk)
        _flash_accumulate(p_sc, alpha_sc, acc_sc, 1, v_ref[pl.ds(base + tk, tk), :])
        return carry

    _flash_scores(q_sc, k_ref[:, pl.ds(0, tk)], s_sc, 0)
    lax.fori_loop(0, n_pairs, body, 0)
    if n_tail:
        _flash_scores(q_sc, k_ref[:, pl.ds(n_main, n_tail)], s_sc, 0)
        _flash_softmax(s_sc, p_sc, alpha_sc, m_sc, 0, n_tail)
        _flash_accumulate(p_sc, alpha_sc, acc_sc, 0, v_ref[pl.ds(n_main, n_tail), :])
    for r in range(rep):
        rows = slice(r * tq, (r + 1) * tq)
        o_ref[:, r * HEAD_DIM:(r + 1) * HEAD_DIM] = (
            acc_sc[rows, :HEAD_DIM] / acc_sc[rows, HEAD_DIM:]).astype(o_ref.dtype)


def _gqa_attention(q, k_t, qkv, v_col_block0, n_q):
    t = k_t.shape[1]
    rep = N_HEADS // N_KV_HEADS
    tq, tk = GQA_TQ, GQA_TK
    n_main = t // (2 * tk) * (2 * tk)
    n_tail = t - n_main
    assert n_q % tq == 0 and n_tail % LANES == 0 and n_tail <= tk
    w = rep * HEAD_DIM
    return pl.pallas_call(
        functools.partial(_flash_kernel, rep=rep, tq=tq, tk=tk, n_main=n_main, n_tail=n_tail),
        grid=(N_KV_HEADS, n_q // tq),
        in_specs=[
            pl.BlockSpec((tq, w), lambda g, i: (i, g)),
            pl.BlockSpec((HEAD_DIM, t), lambda g, i: (g, 0)),
            pl.BlockSpec((t, HEAD_DIM), lambda g, i: (0, v_col_block0 + g)),
        ],
        out_specs=pl.BlockSpec((tq, w), lambda g, i: (i, g)),
        out_shape=jax.ShapeDtypeStruct((n_q, N_HEADS * HEAD_DIM), BF16),
        scratch_shapes=[
            pltpu.VMEM((rep * tq, HEAD_DIM), BF16),
            pltpu.VMEM((2, rep * tq, tk), F32),
            pltpu.VMEM((2, rep * tq, tk), BF16),
            pltpu.VMEM((2, rep * tq, HEAD_DIM), F32),
            pltpu.VMEM((rep * tq, HEAD_DIM), F32),
            pltpu.VMEM((rep * tq, 2 * HEAD_DIM), F32),
        ],
        compiler_params=_params(2),
        name="gqa_flash_attention",
    )(q, k_t, qkv)


def _route(logits, router_bias, block_rows):
    n_tok, n_e = logits.shape
    scores = jax.nn.sigmoid(logits)
    biased = scores + router_bias.astype(F32)
    per_g = n_e // N_EXPERT_GROUPS
    grp = biased.reshape(n_tok, N_EXPERT_GROUPS, per_g)
    g_first, g_mask1 = _first_max(grp)
    g_second, _ = _first_max(jnp.where(g_mask1, -jnp.inf, grp))
    g_score = g_first + g_second
    g_sel = jnp.zeros(g_score.shape, bool)
    for _ in range(TOP_K_GROUPS):
        _, pick = _first_max(jnp.where(g_sel, -jnp.inf, g_score))
        g_sel = g_sel | pick
    e_mask = jnp.repeat(g_sel, per_g, axis=1)
    cand = jnp.where(e_mask, biased, -jnp.inf)
    picks = []
    for _ in range(TOP_K):
        _, pick = _first_max(cand)
        picks.append(pick)
        cand = jnp.where(pick, -jnp.inf, cand)
    onehot = jnp.stack(picks, axis=1)
    w = jnp.sum(jnp.where(onehot, scores[:, None, :], 0.0), axis=-1)
    w = w / jnp.sum(w, axis=-1, keepdims=True) * ROUTED_SCALE

    n_asg = n_tok * TOP_K
    oh = onehot.reshape(n_asg, n_e).astype(jnp.int32)
    csum = jnp.cumsum(oh, axis=0)
    counts = csum[-1]
    padded = (counts + block_rows - 1) // block_rows * block_rows
    pad_end = jnp.cumsum(padded)
    pad_start = pad_end - padded
    dest = jnp.sum(oh * (csum - 1 + pad_start[None, :]), axis=1).astype(jnp.int32)
    n_blocks = -(-(n_asg + n_e * (block_rows - 1)) // block_rows)
    blk_row0 = jnp.arange(n_blocks, dtype=jnp.int32) * block_rows
    block_e = jnp.minimum(jnp.sum((pad_end[None, :] <= blk_row0[:, None]).astype(jnp.int32), axis=1), n_e - 1)
    be_oh = (block_e[:, None] == jnp.arange(n_e, dtype=jnp.int32)[None, :]).astype(jnp.int32)
    seg_left = jnp.sum(be_oh * (counts + pad_start)[None, :], axis=1) - blk_row0
    n_valid = jnp.clip(seg_left, 0, block_rows).astype(jnp.int32)
    codes = _plan_rows(dest, n_blocks * block_rows, block_rows)
    return w, block_e.astype(jnp.int32), n_valid, codes


def _first_max(x):
    m = jnp.max(x, axis=-1, keepdims=True)
    idx = lax.broadcasted_iota(jnp.int32, x.shape, x.ndim - 1)
    first = jnp.min(jnp.where(x == m, idx, x.shape[-1]), axis=-1, keepdims=True)
    return m[..., 0], idx == first


CODE_SLOT_SHIFT = 16
EXPERT_K_CHUNK = 512


def _plan_kernel(dest_hbm, codes_ref, dest_smem, sem, *, chunk, block_rows):
    j = pl.program_id(0)
    n_rows = codes_ref.shape[0]
    log_rows = block_rows.bit_length() - 1
    log_k = TOP_K.bit_length() - 1

    @pl.when(j == 0)
    def _():
        def init(r, c):
            parity = lax.shift_right_logical(r, log_rows) & 1
            codes_ref[r] = ((r & (block_rows - 1)) + parity * block_rows) | (TOP_K << CODE_SLOT_SHIFT)
            return c

        lax.fori_loop(0, n_rows, init, 0, unroll=16)

    cp = pltpu.make_async_copy(dest_hbm.at[j], dest_smem, sem)
    cp.start()
    cp.wait()

    def place(a, c):
        g = j * chunk + a
        codes_ref[dest_smem[a]] = lax.shift_right_logical(g, log_k) | ((g & (TOP_K - 1)) << CODE_SLOT_SHIFT)
        return c

    lax.fori_loop(0, chunk, place, 0, unroll=16)


def _plan_rows(dest, n_rows, block_rows):
    n_asg = dest.shape[0]
    assert block_rows & (block_rows - 1) == 0 and TOP_K & (TOP_K - 1) == 0
    chunk = max(c for c in range(LANES, 8192 + 1, LANES) if n_asg % c == 0)
    n_chunks = n_asg // chunk
    return pl.pallas_call(
        functools.partial(_plan_kernel, chunk=chunk, block_rows=block_rows),
        grid=(n_chunks,),
        in_specs=[pl.BlockSpec(memory_space=pl.ANY)],
        out_specs=pl.BlockSpec(memory_space=pltpu.SMEM),
        out_shape=jax.ShapeDtypeStruct((n_rows,), jnp.int32),
        scratch_shapes=[pltpu.SMEM((chunk,), jnp.int32), pltpu.SemaphoreType.DMA(())],
        compiler_params=_params(1),
        name="plan_rows",
    )(dest.reshape(n_chunks, chunk))


def _swiglu_rows(x, wgu_ref, wd_ref):
    hgu = _dot(x, wgu_ref[...])
    de = hgu.shape[1] // 2
    gate = hgu[:, :de]
    hb = (gate * jax.nn.sigmoid(gate) * hgu[:, de:]).astype(BF16)
    return _dot(hb, wd_ref[...])


def _expert_kernel(be_ref, nv_ref, codes_ref, t_hbm, wgu_ref, wd_ref, y_hbm,
                   xbuf, ybuf, sem_in, sem_out, *, n_tok, br):
    i = pl.program_id(0)
    last_block = pl.num_programs(0) - 2
    slot = i % 2
    active = nv_ref[i] > 0
    prev_active = (i > 0) & (nv_ref[jnp.maximum(i - 1, 0)] > 0)
    token_mask = (1 << CODE_SLOT_SHIFT) - 1
    d = xbuf.shape[2]
    n_chunks = d // EXPERT_K_CHUNK
    group = br // n_chunks

    def gather(block, buf_slot, lo, hi):
        for r in range(lo, hi):
            tok = codes_ref[block * br + r] & token_mask
            pltpu.make_async_copy(t_hbm.at[pl.ds(tok, 1)], xbuf.at[buf_slot, pl.ds(r, 1)],
                                  sem_in.at[buf_slot]).start()

    def gather_wait(buf_slot):
        pltpu.make_async_copy(t_hbm.at[pl.ds(0, br)], xbuf.at[buf_slot], sem_in.at[buf_slot]).wait()

    def scatter(block, buf_slot, lo, hi):
        for r in range(lo, hi):
            code = codes_ref[block * br + r]
            dst = lax.shift_right_logical(code, CODE_SLOT_SHIFT) * n_tok + (code & token_mask)
            pltpu.make_async_copy(ybuf.at[buf_slot, pl.ds(r, 1)], y_hbm.at[pl.ds(dst, 1)],
                                  sem_out.at[buf_slot]).start()

    def scatter_wait(buf_slot):
        pltpu.make_async_copy(ybuf.at[buf_slot], y_hbm.at[pl.ds(0, br)], sem_out.at[buf_slot]).wait()

    def compute(with_scatter):
        nxt = jnp.minimum(i + 1, last_block)
        gather_wait(slot)
        hgu = None
        for c in range(n_chunks):
            cols = slice(c * EXPERT_K_CHUNK, (c + 1) * EXPERT_K_CHUNK)
            part = _dot(xbuf[slot, :, cols].astype(BF16), wgu_ref[cols, :])
            hgu = part if hgu is None else hgu + part
            gather(nxt, 1 - slot, c * group, (c + 1) * group)
            if with_scatter:
                scatter(i - 1, 1 - slot, c * group, (c + 1) * group)
        de = hgu.shape[1] // 2
        gate = hgu[:, :de]
        hb = (gate * jax.nn.sigmoid(gate) * hgu[:, de:]).astype(BF16)
        ybuf[slot] = _dot(hb, wd_ref[...])
        if with_scatter:
            scatter_wait(1 - slot)

    @pl.when(i == 0)
    def _():
        gather(0, 0, 0, br)

    @pl.when(active & (i == 0))
    def _():
        compute(False)

    @pl.when(active & (i > 0))
    def _():
        compute(True)

    @pl.when(jnp.logical_not(active) & ((i == 0) | prev_active))
    def _():
        gather_wait(slot)

        @pl.when(prev_active)
        def _():
            scatter(i - 1, 1 - slot, 0, br)
            scatter_wait(1 - slot)


def _routed_experts(tok_f32, block_e, n_valid, codes, w_gu, w_down):
    n_tok, d = tok_f32.shape
    n_blocks = block_e.shape[0]
    br = codes.shape[0] // n_blocks
    de2 = w_gu.shape[2]
    assert d % EXPERT_K_CHUNK == 0 and br % (d // EXPERT_K_CHUNK) == 0
    block_e = jnp.concatenate([block_e, block_e[-1:]])
    n_valid = jnp.concatenate([n_valid, jnp.zeros((1,), jnp.int32)])
    grid_spec = pltpu.PrefetchScalarGridSpec(
        num_scalar_prefetch=3,
        grid=(n_blocks + 1,),
        in_specs=[
            pl.BlockSpec(memory_space=pl.ANY),
            pl.BlockSpec((None, d, de2), lambda i, be, nv, cd: (be[i], 0, 0)),
            pl.BlockSpec((None, de2 // 2, d), lambda i, be, nv, cd: (be[i], 0, 0)),
        ],
        out_specs=pl.BlockSpec(memory_space=pl.ANY),
        scratch_shapes=[
            pltpu.VMEM((2, br, d), F32),
            pltpu.VMEM((2, br, d), F32),
            pltpu.SemaphoreType.DMA((2,)),
            pltpu.SemaphoreType.DMA((2,)),
        ],
    )
    return pl.pallas_call(
        functools.partial(_expert_kernel, n_tok=n_tok, br=br),
        grid_spec=grid_spec,
        out_shape=jax.ShapeDtypeStruct((TOP_K * n_tok + 2 * br, d), F32),
        compiler_params=_params(1),
        name="routed_experts",
    )(block_e, n_valid, codes, tok_f32, w_gu, w_down)


def _shared_kernel(x_ref, wgu_ref, wd_ref, o_ref):
    o_ref[...] = _swiglu_rows(x_ref[...].astype(BF16), wgu_ref, wd_ref).astype(o_ref.dtype)


def _shared_expert(tok_f32, w_gu, w_down):
    t, d = tok_f32.shape
    tm = ROW_TILE
    de2 = w_gu.shape[1]
    return pl.pallas_call(
        _shared_kernel,
        grid=(t // tm,),
        in_specs=[
            pl.BlockSpec((tm, d), lambda i: (i, 0)),
            pl.BlockSpec((d, de2), lambda i: (0, 0)),
            pl.BlockSpec((de2 // 2, d), lambda i: (0, 0)),
        ],
        out_specs=pl.BlockSpec((tm, d), lambda i: (i, 0)),
        out_shape=jax.ShapeDtypeStruct((t, d), F32),
        compiler_params=_params(1),
        name="shared_expert",
    )(tok_f32, w_gu, w_down)


def _combine_kernel(*refs, with_h):
    x_ref, sh_ref, y0_ref, y1_ref, y2_ref, y3_ref, w_ref, gate_ref, g_ref, b_ref = refs[:10]
    w = w_ref[...]
    y = sh_ref[...]
    for k, y_ref in enumerate((y0_ref, y1_ref, y2_ref, y3_ref)):
        y = y + w[:, k:k + 1] * y_ref[...]
    z = DEEPNORM_ALPHA * x_ref[...] + gate_ref[...] * y
    xn = _layer_norm_rows(z, g_ref[...], b_ref[...])
    if with_h:
        sc_ref, shf_ref, xo_ref, h_ref = refs[10:14]
        xo_ref[...] = xn
        h_ref[...] = (xn * (1.0 + sc_ref[...]) + shf_ref[...]).astype(h_ref.dtype)
    else:
        xo_ref = refs[10]
        xo_ref[...] = xn


def _moe_combine(x, shared, y_slots, w, mod, gate_slot, ln_g, ln_b, n_lat, *, next_mod=None,
                 n_out_rows=None):
    t, d = x.shape
    n_out_rows = t if n_out_rows is None else n_out_rows
    tm = COMBINE_ROWS
    nlb = n_lat // tm
    with_h = next_mod is not None
    y3 = y_slots
    row = pl.BlockSpec((tm, d), lambda i: (i, 0))
    vec = pl.BlockSpec((1, d), lambda i: (0, 0))

    def slot_spec(k):
        return pl.BlockSpec((tm, d), lambda i: (k * (t // tm) + i, 0))

    in_specs = [row, row] + [slot_spec(k) for k in range(TOP_K)] + [
        pl.BlockSpec((tm, TOP_K), lambda i: (i, 0)),
        _mod_spec(gate_slot, nlb, d), vec, vec,
    ]
    args = [x, shared, y3, y3, y3, y3, w, mod, ln_g.reshape(1, d), ln_b.reshape(1, d)]
    out_specs = [row]
    out_shape = [jax.ShapeDtypeStruct((n_out_rows, d), F32)]
    if with_h:
        in_specs += [_mod_spec(1, nlb, d), _mod_spec(0, nlb, d)]
        args += [next_mod, next_mod]
        out_specs.append(row)
        out_shape.append(jax.ShapeDtypeStruct((n_out_rows, d), BF16))
    return pl.pallas_call(
        functools.partial(_combine_kernel, with_h=with_h),
        grid=(n_out_rows // tm,),
        in_specs=in_specs,
        out_specs=out_specs,
        out_shape=out_shape,
        compiler_params=_params(1),
        name="moe_combine",
    )(*args)


def kernel(x, c, ctx, c_ctx, w_mod_down, w_mod_up, b_mod, ln_g, ln_b, gm_w_in, gm_v_g, gm_v_b, gm_w_s, gm_b_s, gm_w_out, na_w_qkv, na_rpb, na_w_out, ft_w_out, ga_w_qkv, ga_q_g, ga_k_g, ga_w_out, router_w, router_bias, exp_w_gate, exp_w_up, exp_w_down, sh_w_gate, sh_w_up, sh_w_down):
    b, n, d = x.shape
    nc = ctx.shape[1]
    assert b == 1 and c.shape[0] == 1
    n_tok = n + nc
    rows = n // GRID_W

    xs = jnp.concatenate([x[0], ctx[0]], axis=0)
    cond8 = jnp.zeros((8, d), F32).at[0].set(c[0]).at[1].set(c_ctx)
    mods = _ada_mod_all(cond8, w_mod_down, w_mod_up, b_mod)
    router_w_pad = jnp.pad(router_w, ((0, 0), (0, 0), (0, LANES - N_EXPERTS)))

    h = _modulate(xs, mods[0], 1, 0, n, BF16)
    for i in range(DEPTH):
        kind = i % 4
        mod = mods[i]
        last = i == DEPTH - 1

        if kind == 0:
            uv = _matmul(h, gm_w_in[0].astype(BF16), out_dtype=BF16, act="gelu")
            gated = _gmlp_gate(uv, gm_v_g[0], gm_v_b[0], gm_w_s[0], gm_b_s[0])
            o = _matmul(gated, gm_w_out[0].astype(BF16), out_dtype=F32)
        elif kind == 1:
            qkv = _matmul(h, na_w_qkv[0].astype(BF16), out_dtype=BF16)
            a_lat = _na_attention(qkv, _na_bias_table(na_rpb[0], rows), n, nc)
            a_ctx = _ctx_attention(qkv, n, nc)
            o = _matmul(jnp.concatenate([a_lat, a_ctx], axis=0), na_w_out[0].astype(BF16), out_dtype=F32)
        elif kind == 2:
            p = jnp.concatenate([_ft_positions_latent(h[:n]), _ft_positions_dense(h[n:])], axis=1)
            o = _matmul(_ft_channels(p), ft_w_out[0].astype(BF16), out_dtype=F32)
        else:
            qkv = _matmul(h, ga_w_qkv[0].astype(BF16), out_dtype=BF16, tn=1024)
            cos, sin = _rope_tables(n, nc)
            qn = _norm_rope(qkv, 0, N_HEADS, ga_q_g[0], cos, sin, HEAD_DIM ** -0.5 * math.log2(math.e))
            kn = _norm_rope(qkv, N_HEADS // N_KV_HEADS, N_KV_HEADS, ga_k_g[0], cos, sin, 1.0)
            att = _gqa_attention(qn, kn.T, qkv, (N_HEADS + N_KV_HEADS), n)
            att = jnp.concatenate([att, jnp.zeros((nc, d), BF16)], axis=0)
            o = _matmul(att, ga_w_out[0].astype(BF16), out_dtype=F32)

        xs, h2, logits = _resln(xs, o, mod, 2, ln_g[i, 0], ln_b[i, 0], n, next_mod=mod,
                                scale_slot=4, shift_slot=3, h_dtype=F32, router_w=router_w_pad[i])

        w, block_e, n_valid, codes = _route(logits[:, :N_EXPERTS], router_bias[i], MOE_ROWS)
        w_gu = jnp.concatenate([exp_w_gate[i], exp_w_up[i]], axis=-1).astype(BF16)
        y_slots = _routed_experts(h2, block_e, n_valid, codes, w_gu, exp_w_down[i].astype(BF16))
        s_gu = jnp.concatenate([sh_w_gate[i], sh_w_up[i]], axis=-1).astype(BF16)
        shared = _shared_expert(h2, s_gu, sh_w_down[i].astype(BF16))
        if last:
            (xs,) = _moe_combine(xs, shared, y_slots, w, mod, 5, ln_g[i, 1], ln_b[i, 1], n,
                                 n_out_rows=n)
        else:
            xs, h = _moe_combine(xs, shared, y_slots, w, mod, 5, ln_g[i, 1], ln_b[i, 1], n,
                                 next_mod=mods[i + 1])
    return xs.reshape(b, n, d)
```

```python
import functools
import math

import jax
import jax.numpy as jnp
import numpy as np
from jax import lax
from jax.experimental import pallas as pl
from jax.experimental.pallas import tpu as pltpu

F32 = jnp.float32
BF16 = jnp.bfloat16

D_MODEL = 4096
DEPTH = 4
GRID_W = 64
N_MOD = 6
LN_EPS = 1e-5
RMS_EPS = 1e-6
DEEPNORM_ALPHA = (2 * DEPTH) ** 0.25
GM_CHUNK = 128
GM_GROUPS = 16
N_HEADS = 32
HEAD_DIM = 128
N_KV_HEADS = 8
ROPE_THETA = 10000.0
NA_WIN_H = 8
NA_WIN_W = 16
FT_GROUPS = 8
N_EXPERTS = 32
N_EXPERT_GROUPS = 4
TOP_K_GROUPS = 2
TOP_K = 4
D_EXPERT = 384
ROUTED_SCALE = 2.5

LANES = 128
VMEM_LIMIT_BYTES = 56 * 1024 * 1024
ROW_TILE = 256
MM_TILE_M = 1280
MM_TILE_N = 1024
MOE_ROWS = 256
COMBINE_ROWS = 128
NA_Q_ROWS = 4
NA_KEY_ROWS = 12
GQA_TQ = 256
GQA_TK = 2048
MASK_VALUE = -1e30


def _params(n_axes):
    return pltpu.CompilerParams(
        dimension_semantics=("arbitrary",) * n_axes,
        vmem_limit_bytes=VMEM_LIMIT_BYTES,
    )


def _dot(a, b, precision=None):
    return jnp.dot(a, b, preferred_element_type=F32, precision=precision)


def _dot_nt(a, b):
    return lax.dot_general(a, b, (((1,), (1,)), ((), ())), preferred_element_type=F32)


def _mm_kernel(a_ref, b_ref, o_ref, *, act):
    acc = _dot(a_ref[...], b_ref[...])
    if act == "gelu":
        acc = jax.nn.gelu(acc)
    o_ref[...] = acc.astype(o_ref.dtype)


def _matmul(a, b, *, out_dtype, act=None, tm=MM_TILE_M, tn=MM_TILE_N):
    m, k = a.shape
    n = b.shape[1]
    tm = min(tm, m)
    tn = min(tn, n)
    if jnp.dtype(out_dtype).itemsize == 4:
        tn = min(tn, MM_TILE_N // 2)
    while n % tn:
        tn //= 2
    assert m % tm == 0 and tn % LANES == 0
    return pl.pallas_call(
        functools.partial(_mm_kernel, act=act),
        grid=(m // tm, n // tn),
        in_specs=[
            pl.BlockSpec((tm, k), lambda i, j: (i, 0)),
            pl.BlockSpec((k, tn), lambda i, j: (0, j)),
        ],
        out_specs=pl.BlockSpec((tm, tn), lambda i, j: (i, j)),
        out_shape=jax.ShapeDtypeStruct((m, n), out_dtype),
        compiler_params=_params(2),
        name="matmul",
    )(a, b)


def _lmm_kernel(m_ref, x_ref, o_ref):
    o_ref[...] = _dot(m_ref[...], x_ref[...]).astype(o_ref.dtype)


def _left_matmul(mat, x, *, out_dtype, tn):
    r, k = mat.shape
    n = x.shape[1]
    assert n % tn == 0
    return pl.pallas_call(
        _lmm_kernel,
        grid=(n // tn,),
        in_specs=[
            pl.BlockSpec((r, k), lambda j: (0, 0)),
            pl.BlockSpec((k, tn), lambda j: (0, j)),
        ],
        out_specs=pl.BlockSpec((r, tn), lambda j: (0, j)),
        out_shape=jax.ShapeDtypeStruct((r, n), out_dtype),
        compiler_params=_params(1),
        name="left_matmul",
    )(mat, x)


def _mod_kernel(cond_ref, wd_ref, wu_ref, b_ref, o_ref):
    cond = cond_ref[...]
    t = _dot(cond * jax.nn.sigmoid(cond), wd_ref[...], precision=lax.Precision.HIGHEST)
    o_ref[...] = _dot(t, wu_ref[...], precision=lax.Precision.HIGHEST) + b_ref[...]


def _ada_mod_all(cond8, w_mod_down, w_mod_up, b_mod):
    depth, d, rank = w_mod_down.shape
    out = pl.pallas_call(
        _mod_kernel,
        grid=(depth, N_MOD),
        in_specs=[
            pl.BlockSpec((8, d), lambda l, j: (0, 0)),
            pl.BlockSpec((None, d, rank), lambda l, j: (l, 0, 0)),
            pl.BlockSpec((None, rank, d), lambda l, j: (l, 0, j)),
            pl.BlockSpec((None, None, 1, d), lambda l, j: (l, j, 0, 0)),
        ],
        out_specs=pl.BlockSpec((None, None, 8, d), lambda l, j: (l, j, 0, 0)),
        out_shape=jax.ShapeDtypeStruct((depth, N_MOD, 8, d), F32),
        compiler_params=_params(2),
        name="ada_mod",
    )(cond8, w_mod_down, w_mod_up, b_mod.reshape(depth, N_MOD, 1, d))
    return jnp.transpose(out[:, :, :2], (0, 2, 1, 3))[:, :, :, None, :]


def _mod_spec(slot, n_lat_blocks, d):
    return pl.BlockSpec((None, None, 1, d), lambda i: (i // n_lat_blocks, slot, 0, 0))


def _modulate_kernel(x_ref, sc_ref, sh_ref, h_ref):
    h_ref[...] = (x_ref[...] * (1.0 + sc_ref[...]) + sh_ref[...]).astype(h_ref.dtype)


def _modulate(x, mod, scale_slot, shift_slot, n_lat, out_dtype):
    t, d = x.shape
    tm = ROW_TILE
    nlb = n_lat // tm
    return pl.pallas_call(
        _modulate_kernel,
        grid=(t // tm,),
        in_specs=[
            pl.BlockSpec((tm, d), lambda i: (i, 0)),
            _mod_spec(scale_slot, nlb, d),
            _mod_spec(shift_slot, nlb, d),
        ],
        out_specs=pl.BlockSpec((tm, d), lambda i: (i, 0)),
        out_shape=jax.ShapeDtypeStruct((t, d), out_dtype),
        compiler_params=_params(1),
        name="modulate",
    )(x, mod, mod)


def _layer_norm_rows(z, g, b):
    mu = jnp.mean(z, axis=-1, keepdims=True)
    zc = z - mu
    var = jnp.mean(zc * zc, axis=-1, keepdims=True)
    return zc * lax.rsqrt(var + LN_EPS) * g + b


def _pack_halves(x):
    c = x.shape[1] // 2
    lo = lax.bitcast_convert_type(x[:, :c].astype(BF16).astype(F32), jnp.uint32)
    hi = lax.bitcast_convert_type(x[:, c:].astype(BF16).astype(F32), jnp.uint32)
    return hi | lax.shift_right_logical(lo, jnp.uint32(16))


def _unpack_low(u):
    return lax.bitcast_convert_type(lax.shift_left(u, jnp.uint32(16)), F32)


def _unpack_high(u):
    return lax.bitcast_convert_type(u & jnp.uint32(0xFFFF0000), F32)


def _unpack_halves(u):
    return jnp.concatenate([_unpack_low(u), _unpack_high(u)], axis=1)


def _resln_kernel(*refs, with_h, with_router):
    x_ref, o_ref, gate_ref, g_ref, b_ref = refs[:5]
    pos = 5
    if with_h:
        sc_ref, sh_ref = refs[pos:pos + 2]
        pos += 2
    if with_router:
        rw_ref = refs[pos]
        pos += 1
    xo_ref = refs[pos]
    pos += 1
    z = DEEPNORM_ALPHA * x_ref[...] + gate_ref[...] * o_ref[...].astype(F32)
    xn = _layer_norm_rows(z, g_ref[...], b_ref[...])
    xo_ref[...] = xn
    if with_h:
        h_ref = refs[pos]
        pos += 1
        h = xn * (1.0 + sc_ref[...]) + sh_ref[...]
        if h_ref.dtype == jnp.uint32:
            h_ref[...] = _pack_halves(h)
        else:
            h_ref[...] = h.astype(h_ref.dtype)
        if with_router:
            lg_ref = refs[pos]
            lg_ref[...] = _dot(h, rw_ref[...], precision=lax.Precision.HIGHEST)


def _resln(x, o, mod, gate_slot, ln_g, ln_b, n_lat, *, next_mod=None, scale_slot=None,
           shift_slot=None, h_dtype=None, router_w=None, n_out_rows=None):
    t, d = x.shape
    n_out_rows = t if n_out_rows is None else n_out_rows
    tm = ROW_TILE
    nlb = n_lat // tm
    with_h = next_mod is not None
    with_router = router_w is not None
    row = pl.BlockSpec((tm, d), lambda i: (i, 0))
    vec = pl.BlockSpec((1, d), lambda i: (0, 0))
    in_specs = [row, row, _mod_spec(gate_slot, nlb, d), vec, vec]
    args = [x, o, mod, ln_g.reshape(1, d), ln_b.reshape(1, d)]
    out_specs = [row]
    out_shape = [jax.ShapeDtypeStruct((n_out_rows, d), F32)]
    if with_h:
        in_specs += [_mod_spec(scale_slot, nlb, d), _mod_spec(shift_slot, nlb, d)]
        args += [next_mod, next_mod]
        h_cols = d // 2 if h_dtype == jnp.uint32 else d
        out_specs.append(pl.BlockSpec((tm, h_cols), lambda i: (i, 0)))
        out_shape.append(jax.ShapeDtypeStruct((n_out_rows, h_cols), h_dtype))
    if with_router:
        ne = router_w.shape[1]
        in_specs.append(pl.BlockSpec((d, ne), lambda i: (0, 0)))
        args.append(router_w)
        out_specs.append(pl.BlockSpec((tm, ne), lambda i: (i, 0)))
        out_shape.append(jax.ShapeDtypeStruct((n_out_rows, ne), F32))
    return pl.pallas_call(
        functools.partial(_resln_kernel, with_h=with_h, with_router=with_router),
        grid=(n_out_rows // tm,),
        in_specs=in_specs,
        out_specs=out_specs,
        out_shape=out_shape,
        compiler_params=_params(1),
        name="residual_layernorm",
    )(*args)


def _gmlp_gate_kernel(u_ref, v_ref, vg_ref, vb_ref, ws_ref, bs_ref, o_ref, *, n_groups, gw):
    vn = _layer_norm_rows(v_ref[...].astype(F32), vg_ref[...], vb_ref[...]).astype(BF16)
    for g in range(n_groups):
        cols = slice(g * gw, (g + 1) * gw)
        sv = _dot(ws_ref[g], vn[:, cols]) + bs_ref[:, cols]
        o_ref[:, cols] = (u_ref[:, cols].astype(F32) * sv).astype(o_ref.dtype)


def _gmlp_gate(uv, v_g, v_b, w_s, b_s):
    t, two_w = uv.shape
    width = two_w // 2
    n_groups, chunk, _ = w_s.shape
    gw = width // n_groups
    bias_full = jnp.repeat(b_s.T, gw, axis=1).astype(F32)
    return pl.pallas_call(
        functools.partial(_gmlp_gate_kernel, n_groups=n_groups, gw=gw),
        grid=(t // chunk,),
        in_specs=[
            pl.BlockSpec((chunk, width), lambda i: (i, 0)),
            pl.BlockSpec((chunk, width), lambda i: (i, 1)),
            pl.BlockSpec((1, width), lambda i: (0, 0)),
            pl.BlockSpec((1, width), lambda i: (0, 0)),
            pl.BlockSpec((n_groups, chunk, chunk), lambda i: (0, 0, 0)),
            pl.BlockSpec((chunk, width), lambda i: (0, 0)),
        ],
        out_specs=pl.BlockSpec((chunk, width), lambda i: (i, 0)),
        out_shape=jax.ShapeDtypeStruct((t, width), BF16),
        compiler_params=_params(1),
        name="gmlp_gate",
    )(uv, uv, v_g.reshape(1, width), v_b.reshape(1, width), w_s.astype(BF16), bias_full)


def _na_bias_table(rpb, rows):
    nb = rows // NA_Q_ROWS
    h = rpb.shape[0]
    n_dr = 2 * NA_WIN_H - 1
    c = np.arange(GRID_W)
    dc = c[None, :] - c[:, None] + NA_WIN_W - 1
    cs = np.clip(c - NA_WIN_W // 2, 0, GRID_W - NA_WIN_W)
    ok_c = (c[None, :] >= cs[:, None]) & (c[None, :] < cs[:, None] + NA_WIN_W)
    sel = ((np.arange(2 * NA_WIN_W - 1)[:, None, None] == dc[None]) & ok_c[None]).astype(np.float32)
    planes = jnp.einsum("hrd,dqk->hrqk", rpb.astype(F32), jnp.asarray(sel), precision=lax.Precision.HIGHEST)
    planes = jnp.where(ok_c[None, None], planes, MASK_VALUE)
    planes = jnp.concatenate([planes, jnp.full((h, 1, GRID_W, GRID_W), MASK_VALUE, F32)], axis=1)
    tables = []
    for b in (0, 1, nb - 1):
        r0 = b * NA_Q_ROWS
        kw0 = int(np.clip(r0 - NA_WIN_H // 2, 0, rows - NA_KEY_ROWS))
        r = r0 + np.arange(NA_Q_ROWS)[:, None]
        kr = kw0 + np.arange(NA_KEY_ROWS)[None, :]
        rs = np.clip(r - NA_WIN_H // 2, 0, rows - NA_WIN_H)
        ok_r = (kr >= rs) & (kr < rs + NA_WIN_H)
        dr = np.where(ok_r, kr - r + NA_WIN_H - 1, n_dr)
        tbl = jnp.stack([planes[:, int(d)] for d in dr.reshape(-1)], axis=1)
        tbl = tbl.reshape(h, NA_Q_ROWS, NA_KEY_ROWS, GRID_W, GRID_W).transpose(0, 1, 3, 2, 4)
        tables.append(tbl.reshape(h, NA_Q_ROWS * GRID_W, NA_KEY_ROWS * GRID_W))
    return jnp.stack(tables, axis=0)


def _na_kernel(q_ref, k_ref, v_ref, bias_ref, o_ref, *, rows, n_lat, n_ctx):
    b = pl.program_id(1)
    kw0 = jnp.clip(b * NA_Q_ROWS - NA_WIN_H // 2, 0, rows - NA_KEY_ROWS)
    start = pl.multiple_of(kw0 * GRID_W, GRID_W)
    nk = NA_KEY_ROWS * GRID_W
    scale = HEAD_DIM ** -0.5
    q = q_ref[...]
    kw = k_ref[pl.ds(start, nk), :]
    vw = v_ref[pl.ds(start, nk), :]
    kc = k_ref[pl.ds(n_lat, n_ctx), :]
    vc = v_ref[pl.ds(n_lat, n_ctx), :]
    s_loc = _dot_nt(q, kw) * scale + bias_ref[...]
    s_ctx = _dot_nt(q, kc) * scale
    m = jnp.maximum(jnp.max(s_loc, axis=-1, keepdims=True), jnp.max(s_ctx, axis=-1, keepdims=True))
    p_loc = jnp.exp(s_loc - m)
    p_ctx = jnp.exp(s_ctx - m)
    l = jnp.sum(p_loc, axis=-1, keepdims=True) + jnp.sum(p_ctx, axis=-1, keepdims=True)
    o = _dot(p_loc.astype(BF16), vw) + _dot(p_ctx.astype(BF16), vc)
    o_ref[...] = (o / l).astype(o_ref.dtype)


def _na_attention(qkv, bias_table, n_lat, n_ctx):
    t = qkv.shape[0]
    rows = n_lat // GRID_W
    nb = rows // NA_Q_ROWS
    tq = NA_Q_ROWS * GRID_W
    nk = NA_KEY_ROWS * GRID_W
    h = N_HEADS
    return pl.pallas_call(
        functools.partial(_na_kernel, rows=rows, n_lat=n_lat, n_ctx=n_ctx),
        grid=(h, nb),
        in_specs=[
            pl.BlockSpec((tq, HEAD_DIM), lambda hh, b: (b, hh)),
            pl.BlockSpec((t, HEAD_DIM), lambda hh, b: (0, h + hh)),
            pl.BlockSpec((t, HEAD_DIM), lambda hh, b: (0, 2 * h + hh)),
            pl.BlockSpec((None, None, tq, nk),
                         lambda hh, b: (jnp.where(b == 0, 0, jnp.where(b == nb - 1, 2, 1)), hh, 0, 0)),
        ],
        out_specs=pl.BlockSpec((tq, HEAD_DIM), lambda hh, b: (b, hh)),
        out_shape=jax.ShapeDtypeStruct((n_lat, h * HEAD_DIM), BF16),
        compiler_params=_params(2),
        name="neighbourhood_attention",
    )(qkv, qkv, qkv, bias_table)


def _ctx_attn_kernel(q_ref, k_ref, v_ref, o_ref):
    s = _dot_nt(q_ref[...], k_ref[...]) * (HEAD_DIM ** -0.5)
    m = jnp.max(s, axis=-1, keepdims=True)
    p = jnp.exp(s - m)
    l = jnp.sum(p, axis=-1, keepdims=True)
    o_ref[...] = (_dot(p.astype(BF16), v_ref[...]) / l).astype(o_ref.dtype)


def _ctx_attention(qkv, n_lat, n_ctx):
    h = N_HEADS
    rb = n_lat // n_ctx
    return pl.pallas_call(
        _ctx_attn_kernel,
        grid=(h,),
        in_specs=[
            pl.BlockSpec((n_ctx, HEAD_DIM), lambda hh: (rb, hh)),
            pl.BlockSpec((n_ctx, HEAD_DIM), lambda hh: (rb, h + hh)),
            pl.BlockSpec((n_ctx, HEAD_DIM), lambda hh: (rb, 2 * h + hh)),
        ],
        out_specs=pl.BlockSpec((n_ctx, HEAD_DIM), lambda hh: (0, hh)),
        out_shape=jax.ShapeDtypeStruct((n_ctx, h * HEAD_DIM), BF16),
        compiler_params=_params(1),
        name="context_attention",
    )(qkv, qkv, qkv)


def _dft_cos_sin(k_times_n, period):
    ang = (2.0 * math.pi / period) * (k_times_n % period).astype(F32)
    return jnp.cos(ang), -jnp.sin(ang)


def _ft_stage2_kernel(g_ref, y_ref, o_ref, *, nb):
    for i in range(nb):
        rhs = jnp.concatenate([y_ref[0, i], y_ref[1, i]], axis=0)
        res = _dot(g_ref[i], rhs)
        half = res.shape[0] // 2
        o_ref[0, :, i, :] = res[:half].astype(o_ref.dtype)
        o_ref[1, :, i, :] = res[half:].astype(o_ref.dtype)


def _ft_positions_latent(h_lat):
    n, d = h_lat.shape
    l1 = l2 = int(round(math.sqrt(n)))
    assert l1 * l2 == n
    idx = jnp.arange(l1, dtype=jnp.int32)
    re, im = _dft_cos_sin(idx[:, None] * idx[None, :], l1)
    m1 = (jnp.concatenate([re, im], axis=0) * (l1 ** -0.5)).astype(BF16)
    y = _left_matmul(m1, h_lat.reshape(l1, l2 * d), out_dtype=BF16, tn=8192)
    y = y.reshape(2, l1, l2, d)
    k1 = idx[:, None, None]
    k2 = idx[None, :, None]
    n2 = idx[None, None, :]
    gr, gi = _dft_cos_sin(n2 * k2 * l1 + n2 * k1, n)
    g = jnp.concatenate(
        [jnp.concatenate([gr, -gi], axis=2), jnp.concatenate([gi, gr], axis=2)], axis=1
    )
    g = (g * (l2 ** -0.5)).astype(BF16)
    nb, cb = 8, min(1024, d)
    p = pl.pallas_call(
        functools.partial(_ft_stage2_kernel, nb=nb),
        grid=(l1 // nb, d // cb),
        in_specs=[
            pl.BlockSpec((nb, 2 * l2, 2 * l2), lambda a, c: (a, 0, 0)),
            pl.BlockSpec((2, nb, l2, cb), lambda a, c: (0, a, 0, c)),
        ],
        out_specs=pl.BlockSpec((2, l2, nb, cb), lambda a, c: (0, 0, a, c)),
        out_shape=jax.ShapeDtypeStruct((2, l2, l1, d), BF16),
        compiler_params=_params(2),
        name="ft_stage2",
    )(g, y)
    return p.reshape(2, n, d)


def _ft_positions_dense(h_ctx):
    n, d = h_ctx.shape
    idx = jnp.arange(n, dtype=jnp.int32)
    re, im = _dft_cos_sin(idx[:, None] * idx[None, :], n)
    m = (jnp.concatenate([re, im], axis=0) * (n ** -0.5)).astype(BF16)
    return _left_matmul(m, h_ctx, out_dtype=BF16, tn=min(d, 2048)).reshape(2, n, d)


def _ft_channel_kernel(pr_ref, pi_ref, c_ref, s_ref, o_ref):
    o_ref[...] = (_dot(pr_ref[...], c_ref[...]) + _dot(pi_ref[...], s_ref[...])).astype(o_ref.dtype)


def _ft_channels(p):
    _, t, d = p.shape
    gc = d // FT_GROUPS
    idx = jnp.arange(gc, dtype=jnp.int32)
    re, im = _dft_cos_sin(idx[:, None] * idx[None, :], gc)
    cc = (re * (gc ** -0.5)).astype(BF16)
    sc = (-im * (gc ** -0.5)).astype(BF16)
    tm = MM_TILE_M if t % MM_TILE_M == 0 else t
    return pl.pallas_call(
        _ft_channel_kernel,
        grid=(t // tm, FT_GROUPS),
        in_specs=[
            pl.BlockSpec((None, tm, gc), lambda i, g: (0, i, g)),
            pl.BlockSpec((None, tm, gc), lambda i, g: (1, i, g)),
            pl.BlockSpec((gc, gc), lambda i, g: (0, 0)),
            pl.BlockSpec((gc, gc), lambda i, g: (0, 0)),
        ],
        out_specs=pl.BlockSpec((tm, gc), lambda i, g: (i, g)),
        out_shape=jax.ShapeDtypeStruct((t, d), BF16),
        compiler_params=_params(2),
        name="ft_channels",
    )(p, p, cc, sc)


def _rope_tables(n_lat, n_ctx):
    half = HEAD_DIM // 2
    t = jnp.arange(n_lat)
    inv = ROPE_THETA ** (-jnp.arange(0, half, 2, dtype=F32) / half)
    ang_r = (t // GRID_W).astype(F32)[:, None] * inv[None, :]
    ang_c = (t % GRID_W).astype(F32)[:, None] * inv[None, :]
    cos = jnp.concatenate([jnp.cos(ang_r)] * 2 + [jnp.cos(ang_c)] * 2, axis=-1)
    sin = jnp.concatenate([-jnp.sin(ang_r), jnp.sin(ang_r), -jnp.sin(ang_c), jnp.sin(ang_c)], axis=-1)
    cos = jnp.concatenate([cos, jnp.ones((n_ctx, HEAD_DIM), F32)], axis=0)
    sin = jnp.concatenate([sin, jnp.zeros((n_ctx, HEAD_DIM), F32)], axis=0)
    return cos, sin


def _norm_rope_kernel(x_ref, g_ref, cos_ref, sin_ref, o_ref, *, n_heads, scale):
    quarter = HEAD_DIM // 4
    lane = lax.broadcasted_iota(jnp.int32, (x_ref.shape[0], HEAD_DIM), 1)
    first = (lane % (2 * quarter)) < quarter
    cos = cos_ref[...]
    sin = sin_ref[...]
    g = g_ref[...]
    for hh in range(n_heads):
        cols = slice(hh * HEAD_DIM, (hh + 1) * HEAD_DIM)
        x = x_ref[:, cols].astype(F32)
        xn = x * lax.rsqrt(jnp.mean(x * x, axis=-1, keepdims=True) + RMS_EPS) * g
        partner = jnp.where(first, pltpu.roll(xn, HEAD_DIM - quarter, 1), pltpu.roll(xn, quarter, 1))
        o_ref[:, cols] = ((xn * cos + partner * sin) * scale).astype(o_ref.dtype)


def _norm_rope(qkv, col_block, n_heads, gain, cos, sin, scale):
    t = qkv.shape[0]
    tm = ROW_TILE
    w = n_heads * HEAD_DIM
    return pl.pallas_call(
        functools.partial(_norm_rope_kernel, n_heads=n_heads, scale=scale),
        grid=(t // tm,),
        in_specs=[
            pl.BlockSpec((tm, w), lambda i: (i, col_block)),
            pl.BlockSpec((1, HEAD_DIM), lambda i: (0, 0)),
            pl.BlockSpec((tm, HEAD_DIM), lambda i: (i, 0)),
            pl.BlockSpec((tm, HEAD_DIM), lambda i: (i, 0)),
        ],
        out_specs=pl.BlockSpec((tm, w), lambda i: (i, 0)),
        out_shape=jax.ShapeDtypeStruct((t, w), BF16),
        compiler_params=_params(1),
        name="norm_rope",
    )(qkv, gain.reshape(1, HEAD_DIM), cos, sin)


def _lane_tiles(x):
    return [x[:, j * LANES:(j + 1) * LANES] for j in range(x.shape[1] // LANES)]


FLASH_SUB_ROWS = 64


def _flash_scores(q_sc, kt, s_sc, slot):
    s_sc[slot, :, :kt.shape[1]] = _dot(q_sc[...], kt)


def _flash_softmax(s_sc, p_sc, alpha_sc, m_sc, slot, n):
    for b in range(s_sc.shape[1] // FLASH_SUB_ROWS):
        rows = slice(b * FLASH_SUB_ROWS, (b + 1) * FLASH_SUB_ROWS)
        tiles = _lane_tiles(s_sc[slot, rows, :n])
        m_prev = m_sc[rows]
        m_new = jnp.maximum(m_prev, jnp.max(functools.reduce(jnp.maximum, tiles), axis=-1, keepdims=True))
        alpha_sc[slot, rows] = jnp.exp2(m_prev - m_new)
        p_sc[slot, rows, :n] = jnp.concatenate([jnp.exp2(t - m_new).astype(BF16) for t in tiles], axis=1)
        m_sc[rows] = m_new


def _flash_accumulate(p_sc, alpha_sc, acc_sc, slot, vs):
    n = vs.shape[0]
    v_ext = jnp.concatenate([vs, jnp.ones_like(vs)], axis=1)
    alpha = alpha_sc[slot]
    acc_sc[...] = jnp.concatenate([alpha, alpha], axis=1) * acc_sc[...] + _dot(p_sc[slot, :, :n], v_ext)


def _flash_kernel(q_ref, k_ref, v_ref, o_ref, q_sc, s_sc, p_sc, alpha_sc, m_sc, acc_sc, *,
                  rep, tq, tk, n_main, n_tail):
    for r in range(rep):
        q_sc[r * tq:(r + 1) * tq, :] = q_ref[:, r * HEAD_DIM:(r + 1) * HEAD_DIM]
    m_sc[...] = jnp.full(m_sc.shape, -jnp.inf, F32)
    acc_sc[...] = jnp.zeros(acc_sc.shape, F32)
    n_pairs = n_main // (2 * tk)

    def body(c, carry):
        base = pl.multiple_of(c * 2 * tk, 2 * tk)
        nxt = pl.multiple_of(jnp.minimum(c + 1, n_pairs - 1) * 2 * tk, 2 * tk)
        _flash_scores(q_sc, k_ref[:, pl.ds(base + tk, tk)], s_sc, 1)
        _flash_softmax(s_sc, p_sc, alpha_sc, m_sc, 0, tk)
        _flash_accumulate(p_sc, alpha_sc, acc_sc, 0, v_ref[pl.ds(base, tk), :])
        _flash_scores(q_sc, k_ref[:, pl.ds(nxt, tk)], s_sc, 0)
        _flash_softmax(s_sc, p_sc, alpha_sc, m_sc, 1, tk)
        _flash_accumulate(p_sc, alpha_sc, acc_sc, 1, v_ref[pl.ds(base + tk, tk), :])
        return carry

    _flash_scores(q_sc, k_ref[:, pl.ds(0, tk)], s_sc, 0)
    lax.fori_loop(0, n_pairs, body, 0)
    if n_tail:
        _flash_scores(q_sc, k_ref[:, pl.ds(n_main, n_tail)], s_sc, 0)
        _flash_softmax(s_sc, p_sc, alpha_sc, m_sc, 0, n_tail)
        _flash_accumulate(p_sc, alpha_sc, acc_sc, 0, v_ref[pl.ds(n_main, n_tail), :])
    for r in range(rep):
        rows = slice(r * tq, (r + 1) * tq)
        o_ref[:, r * HEAD_DIM:(r + 1) * HEAD_DIM] = (
            acc_sc[rows, :HEAD_DIM] / acc_sc[rows, HEAD_DIM:]).astype(o_ref.dtype)


def _gqa_attention(q, k_t, qkv, v_col_block0, n_q):
    t = k_t.shape[1]
    rep = N_HEADS // N_KV_HEADS
    tq, tk = GQA_TQ, GQA_TK
    n_main = t // (2 * tk) * (2 * tk)
    n_tail = t - n_main
    assert n_q % tq == 0 and n_tail % LANES == 0 and n_tail <= tk
    w = rep * HEAD_DIM
    return pl.pallas_call(
        functools.partial(_flash_kernel, rep=rep, tq=tq, tk=tk, n_main=n_main, n_tail=n_tail),
        grid=(N_KV_HEADS, n_q // tq),
        in_specs=[
            pl.BlockSpec((tq, w), lambda g, i: (i, g)),
            pl.BlockSpec((HEAD_DIM, t), lambda g, i: (g, 0)),
            pl.BlockSpec((t, HEAD_DIM), lambda g, i: (0, v_col_block0 + g)),
        ],
        out_specs=pl.BlockSpec((tq, w), lambda g, i: (i, g)),
        out_shape=jax.ShapeDtypeStruct((n_q, N_HEADS * HEAD_DIM), BF16),
        scratch_shapes=[
            pltpu.VMEM((rep * tq, HEAD_DIM), BF16),
            pltpu.VMEM((2, rep * tq, tk), F32),
            pltpu.VMEM((2, rep * tq, tk), BF16),
            pltpu.VMEM((2, rep * tq, HEAD_DIM), F32),
            pltpu.VMEM((rep * tq, HEAD_DIM), F32),
            pltpu.VMEM((rep * tq, 2 * HEAD_DIM), F32),
        ],
        compiler_params=_params(2),
        name="gqa_flash_attention",
    )(q, k_t, qkv)


def _route(logits, router_bias, block_rows):
    n_tok, n_e = logits.shape
    scores = jax.nn.sigmoid(logits)
    biased = scores + router_bias.astype(F32)
    per_g = n_e // N_EXPERT_GROUPS
    grp = biased.reshape(n_tok, N_EXPERT_GROUPS, per_g)
    g_first, g_mask1 = _first_max(grp)
    g_second, _ = _first_max(jnp.where(g_mask1, -jnp.inf, grp))
    g_score = g_first + g_second
    g_sel = jnp.zeros(g_score.shape, bool)
    for _ in range(TOP_K_GROUPS):
        _, pick = _first_max(jnp.where(g_sel, -jnp.inf, g_score))
        g_sel = g_sel | pick
    e_mask = jnp.repeat(g_sel, per_g, axis=1)
    cand = jnp.where(e_mask, biased, -jnp.inf)
    picks = []
    for _ in range(TOP_K):
        _, pick = _first_max(cand)
        picks.append(pick)
        cand = jnp.where(pick, -jnp.inf, cand)
    onehot = jnp.stack(picks, axis=1)
    w = jnp.sum(jnp.where(onehot, scores[:, None, :], 0.0), axis=-1)
    w = w / jnp.sum(w, axis=-1, keepdims=True) * ROUTED_SCALE

    n_asg = n_tok * TOP_K
    oh = onehot.reshape(n_asg, n_e).astype(jnp.int32)
    csum = jnp.cumsum(oh, axis=0)
    counts = csum[-1]
    padded = (counts + block_rows - 1) // block_rows * block_rows
    pad_end = jnp.cumsum(padded)
    pad_start = pad_end - padded
    dest = jnp.sum(oh * (csum - 1 + pad_start[None, :]), axis=1).astype(jnp.int32)
    n_blocks = -(-(n_asg + n_e * (block_rows - 1)) // block_rows)
    blk_row0 = jnp.arange(n_blocks, dtype=jnp.int32) * block_rows
    block_e = jnp.minimum(jnp.sum((pad_end[None, :] <= blk_row0[:, None]).astype(jnp.int32), axis=1), n_e - 1)
    be_oh = (block_e[:, None] == jnp.arange(n_e, dtype=jnp.int32)[None, :]).astype(jnp.int32)
    seg_left = jnp.sum(be_oh * (counts + pad_start)[None, :], axis=1) - blk_row0
    n_valid = jnp.clip(seg_left, 0, block_rows).astype(jnp.int32)
    codes = _plan_rows(dest, n_blocks * block_rows, block_rows)
    return w, block_e.astype(jnp.int32), n_valid, codes


def _first_max(x):
    m = jnp.max(x, axis=-1, keepdims=True)
    idx = lax.broadcasted_iota(jnp.int32, x.shape, x.ndim - 1)
    first = jnp.min(jnp.where(x == m, idx, x.shape[-1]), axis=-1, keepdims=True)
    return m[..., 0], idx == first


CODE_SLOT_SHIFT = 16
EXPERT_K_CHUNK = 512


def _plan_kernel(dest_hbm, codes_ref, dest_smem, sem, *, chunk, block_rows):
    j = pl.program_id(0)
    n_rows = codes_ref.shape[0]
    log_rows = block_rows.bit_length() - 1
    log_k = TOP_K.bit_length() - 1

    @pl.when(j == 0)
    def _():
        def init(r, c):
            parity = lax.shift_right_logical(r, log_rows) & 1
            codes_ref[r] = ((r & (block_rows - 1)) + parity * block_rows) | (TOP_K << CODE_SLOT_SHIFT)
            return c

        lax.fori_loop(0, n_rows, init, 0, unroll=16)

    cp = pltpu.make_async_copy(dest_hbm.at[j], dest_smem, sem)
    cp.start()
    cp.wait()

    def place(a, c):
        g = j * chunk + a
        codes_ref[dest_smem[a]] = lax.shift_right_logical(g, log_k) | ((g & (TOP_K - 1)) << CODE_SLOT_SHIFT)
        return c

    lax.fori_loop(0, chunk, place, 0, unroll=16)


def _plan_rows(dest, n_rows, block_rows):
    n_asg = dest.shape[0]
    assert block_rows & (block_rows - 1) == 0 and TOP_K & (TOP_K - 1) == 0
    chunk = max(c for c in range(LANES, 8192 + 1, LANES) if n_asg % c == 0)
    n_chunks = n_asg // chunk
    return pl.pallas_call(
        functools.partial(_plan_kernel, chunk=chunk, block_rows=block_rows),
        grid=(n_chunks,),
        in_specs=[pl.BlockSpec(memory_space=pl.ANY)],
        out_specs=pl.BlockSpec(memory_space=pltpu.SMEM),
        out_shape=jax.ShapeDtypeStruct((n_rows,), jnp.int32),
        scratch_shapes=[pltpu.SMEM((chunk,), jnp.int32), pltpu.SemaphoreType.DMA(())],
        compiler_params=_params(1),
        name="plan_rows",
    )(dest.reshape(n_chunks, chunk))


def _swiglu_rows(x, wgu_ref, wd_ref):
    hgu = _dot(x, wgu_ref[...])
    de = hgu.shape[1] // 2
    gate = hgu[:, :de]
    hb = (gate * jax.nn.sigmoid(gate) * hgu[:, de:]).astype(BF16)
    return _dot(hb, wd_ref[...])


def _expert_kernel(be_ref, nv_ref, codes_ref, t_hbm, wgu_ref, wd_ref, y_hbm,
                   xbuf, ybuf, sem_in, sem_out, *, n_tok, br):
    i = pl.program_id(0)
    last_block = pl.num_programs(0) - 2
    slot = i % 2
    active = nv_ref[i] > 0
    prev_active = (i > 0) & (nv_ref[jnp.maximum(i - 1, 0)] > 0)
    token_mask = (1 << CODE_SLOT_SHIFT) - 1
    half = xbuf.shape[2]
    n_chunks = 2 * half // EXPERT_K_CHUNK
    group = br // n_chunks

    def gather(block, buf_slot, lo, hi):
        for r in range(lo, hi):
            tok = codes_ref[block * br + r] & token_mask
            pltpu.make_async_copy(t_hbm.at[pl.ds(tok, 1)], xbuf.at[buf_slot, pl.ds(r, 1)],
                                  sem_in.at[buf_slot]).start()

    def gather_wait(buf_slot):
        pltpu.make_async_copy(t_hbm.at[pl.ds(0, br)], xbuf.at[buf_slot], sem_in.at[buf_slot]).wait()

    def scatter(block, buf_slot, lo, hi):
        for r in range(lo, hi):
            code = codes_ref[block * br + r]
            dst = lax.shift_right_logical(code, CODE_SLOT_SHIFT) * n_tok + (code & token_mask)
            pltpu.make_async_copy(ybuf.at[buf_slot, pl.ds(r, 1)], y_hbm.at[pl.ds(dst, 1)],
                                  sem_out.at[buf_slot]).start()

    def scatter_wait(buf_slot):
        pltpu.make_async_copy(ybuf.at[buf_slot], y_hbm.at[pl.ds(0, br)], sem_out.at[buf_slot]).wait()

    def compute(with_scatter):
        nxt = jnp.minimum(i + 1, last_block)
        gather_wait(slot)
        hgu = None
        for c in range(n_chunks):
            cols = slice(c * EXPERT_K_CHUNK, (c + 1) * EXPERT_K_CHUNK)
            lo = c * EXPERT_K_CHUNK % half
            words = xbuf[slot, :, lo:lo + EXPERT_K_CHUNK]
            xc = _unpack_low(words) if c * EXPERT_K_CHUNK < half else _unpack_high(words)
            part = _dot(xc.astype(BF16), wgu_ref[cols, :])
            hgu = part if hgu is None else hgu + part
            gather(nxt, 1 - slot, c * group, (c + 1) * group)
            if with_scatter:
                scatter(i - 1, 1 - slot, c * group, (c + 1) * group)
        de = hgu.shape[1] // 2
        gate = hgu[:, :de]
        hb = (gate * jax.nn.sigmoid(gate) * hgu[:, de:]).astype(BF16)
        ybuf[slot] = _pack_halves(_dot(hb, wd_ref[...]))
        if with_scatter:
            scatter_wait(1 - slot)

    @pl.when(i == 0)
    def _():
        gather(0, 0, 0, br)

    @pl.when(active & (i == 0))
    def _():
        compute(False)

    @pl.when(active & (i > 0))
    def _():
        compute(True)

    @pl.when(jnp.logical_not(active) & ((i == 0) | prev_active))
    def _():
        gather_wait(slot)

        @pl.when(prev_active)
        def _():
            scatter(i - 1, 1 - slot, 0, br)
            scatter_wait(1 - slot)


def _routed_experts(tok_packed, block_e, n_valid, codes, w_gu, w_down):
    n_tok, half = tok_packed.shape
    d = 2 * half
    n_blocks = block_e.shape[0]
    br = codes.shape[0] // n_blocks
    de2 = w_gu.shape[2]
    assert half % EXPERT_K_CHUNK == 0 and br % (d // EXPERT_K_CHUNK) == 0
    block_e = jnp.concatenate([block_e, block_e[-1:]])
    n_valid = jnp.concatenate([n_valid, jnp.zeros((1,), jnp.int32)])
    grid_spec = pltpu.PrefetchScalarGridSpec(
        num_scalar_prefetch=3,
        grid=(n_blocks + 1,),
        in_specs=[
            pl.BlockSpec(memory_space=pl.ANY),
            pl.BlockSpec((None, d, de2), lambda i, be, nv, cd: (be[i], 0, 0)),
            pl.BlockSpec((None, de2 // 2, d), lambda i, be, nv, cd: (be[i], 0, 0)),
        ],
        out_specs=pl.BlockSpec(memory_space=pl.ANY),
        scratch_shapes=[
            pltpu.VMEM((2, br, half), jnp.uint32),
            pltpu.VMEM((2, br, half), jnp.uint32),
            pltpu.SemaphoreType.DMA((2,)),
            pltpu.SemaphoreType.DMA((2,)),
        ],
    )
    return pl.pallas_call(
        functools.partial(_expert_kernel, n_tok=n_tok, br=br),
        grid_spec=grid_spec,
        out_shape=jax.ShapeDtypeStruct((TOP_K * n_tok + 2 * br, half), jnp.uint32),
        compiler_params=_params(1),
        name="routed_experts",
    )(block_e, n_valid, codes, tok_packed, w_gu, w_down)


def _shared_kernel(x_ref, wgu_ref, wd_ref, o_ref):
    o_ref[...] = _swiglu_rows(_unpack_halves(x_ref[...]).astype(BF16), wgu_ref, wd_ref).astype(o_ref.dtype)


def _shared_expert(tok_packed, w_gu, w_down):
    t, half = tok_packed.shape
    d = 2 * half
    tm = ROW_TILE
    de2 = w_gu.shape[1]
    return pl.pallas_call(
        _shared_kernel,
        grid=(t // tm,),
        in_specs=[
            pl.BlockSpec((tm, half), lambda i: (i, 0)),
            pl.BlockSpec((d, de2), lambda i: (0, 0)),
            pl.BlockSpec((de2 // 2, d), lambda i: (0, 0)),
        ],
        out_specs=pl.BlockSpec((tm, d), lambda i: (i, 0)),
        out_shape=jax.ShapeDtypeStruct((t, d), F32),
        compiler_params=_params(1),
        name="shared_expert",
    )(tok_packed, w_gu, w_down)


def _combine_kernel(*refs, with_h):
    x_ref, sh_ref, y0_ref, y1_ref, y2_ref, y3_ref, w_ref, gate_ref, g_ref, b_ref = refs[:10]
    w = w_ref[...]
    y = sh_ref[...]
    for k, y_ref in enumerate((y0_ref, y1_ref, y2_ref, y3_ref)):
        y = y + w[:, k:k + 1] * _unpack_halves(y_ref[...])
    z = DEEPNORM_ALPHA * x_ref[...] + gate_ref[...] * y
    xn = _layer_norm_rows(z, g_ref[...], b_ref[...])
    if with_h:
        sc_ref, shf_ref, xo_ref, h_ref = refs[10:14]
        xo_ref[...] = xn
        h_ref[...] = (xn * (1.0 + sc_ref[...]) + shf_ref[...]).astype(h_ref.dtype)
    else:
        xo_ref = refs[10]
        xo_ref[...] = xn


def _moe_combine(x, shared, y_slots, w, mod, gate_slot, ln_g, ln_b, n_lat, *, next_mod=None,
                 n_out_rows=None):
    t, d = x.shape
    n_out_rows = t if n_out_rows is None else n_out_rows
    tm = COMBINE_ROWS
    nlb = n_lat // tm
    with_h = next_mod is not None
    y3 = y_slots
    row = pl.BlockSpec((tm, d), lambda i: (i, 0))
    vec = pl.BlockSpec((1, d), lambda i: (0, 0))

    def slot_spec(k):
        return pl.BlockSpec((tm, d // 2), lambda i: (k * (t // tm) + i, 0))

    in_specs = [row, row] + [slot_spec(k) for k in range(TOP_K)] + [
        pl.BlockSpec((tm, TOP_K), lambda i: (i, 0)),
        _mod_spec(gate_slot, nlb, d), vec, vec,
    ]
    args = [x, shared, y3, y3, y3, y3, w, mod, ln_g.reshape(1, d), ln_b.reshape(1, d)]
    out_specs = [row]
    out_shape = [jax.ShapeDtypeStruct((n_out_rows, d), F32)]
    if with_h:
        in_specs += [_mod_spec(1, nlb, d), _mod_spec(0, nlb, d)]
        args += [next_mod, next_mod]
        out_specs.append(row)
        out_shape.append(jax.ShapeDtypeStruct((n_out_rows, d), BF16))
    return pl.pallas_call(
        functools.partial(_combine_kernel, with_h=with_h),
        grid=(n_out_rows // tm,),
        in_specs=in_specs,
        out_specs=out_specs,
        out_shape=out_shape,
        compiler_params=_params(1),
        name="moe_combine",
    )(*args)


def kernel(x, c, ctx, c_ctx, w_mod_down, w_mod_up, b_mod, ln_g, ln_b, gm_w_in, gm_v_g, gm_v_b, gm_w_s, gm_b_s, gm_w_out, na_w_qkv, na_rpb, na_w_out, ft_w_out, ga_w_qkv, ga_q_g, ga_k_g, ga_w_out, router_w, router_bias, exp_w_gate, exp_w_up, exp_w_down, sh_w_gate, sh_w_up, sh_w_down):
    b, n, d = x.shape
    nc = ctx.shape[1]
    assert b == 1 and c.shape[0] == 1
    n_tok = n + nc
    rows = n // GRID_W

    xs = jnp.concatenate([x[0], ctx[0]], axis=0)
    cond8 = jnp.zeros((8, d), F32).at[0].set(c[0]).at[1].set(c_ctx)
    mods = _ada_mod_all(cond8, w_mod_down, w_mod_up, b_mod)
    router_w_pad = jnp.pad(router_w, ((0, 0), (0, 0), (0, LANES - N_EXPERTS)))

    h = _modulate(xs, mods[0], 1, 0, n, BF16)
    for i in range(DEPTH):
        kind = i % 4
        mod = mods[i]
        last = i == DEPTH - 1

        if kind == 0:
            uv = _matmul(h, gm_w_in[0].astype(BF16), out_dtype=BF16, act="gelu")
            gated = _gmlp_gate(uv, gm_v_g[0], gm_v_b[0], gm_w_s[0], gm_b_s[0])
            o = _matmul(gated, gm_w_out[0].astype(BF16), out_dtype=F32)
        elif kind == 1:
            qkv = _matmul(h, na_w_qkv[0].astype(BF16), out_dtype=BF16)
            a_lat = _na_attention(qkv, _na_bias_table(na_rpb[0], rows), n, nc)
            a_ctx = _ctx_attention(qkv, n, nc)
            o = _matmul(jnp.concatenate([a_lat, a_ctx], axis=0), na_w_out[0].astype(BF16), out_dtype=F32)
        elif kind == 2:
            p = jnp.concatenate([_ft_positions_latent(h[:n]), _ft_positions_dense(h[n:])], axis=1)
            o = _matmul(_ft_channels(p), ft_w_out[0].astype(BF16), out_dtype=F32)
        else:
            qkv = _matmul(h, ga_w_qkv[0].astype(BF16), out_dtype=BF16, tn=1024)
            cos, sin = _rope_tables(n, nc)
            qn = _norm_rope(qkv, 0, N_HEADS, ga_q_g[0], cos, sin, HEAD_DIM ** -0.5 * math.log2(math.e))
            kn = _norm_rope(qkv, N_HEADS // N_KV_HEADS, N_KV_HEADS, ga_k_g[0], cos, sin, 1.0)
            att = _gqa_attention(qn, kn.T, qkv, (N_HEADS + N_KV_HEADS), n)
            att = jnp.concatenate([att, jnp.zeros((nc, d), BF16)], axis=0)
            o = _matmul(att, ga_w_out[0].astype(BF16), out_dtype=F32)

        xs, h2, logits = _resln(xs, o, mod, 2, ln_g[i, 0], ln_b[i, 0], n, next_mod=mod,
                                scale_slot=4, shift_slot=3, h_dtype=jnp.uint32, router_w=router_w_pad[i])

        w, block_e, n_valid, codes = _route(logits[:, :N_EXPERTS], router_bias[i], MOE_ROWS)
        w_gu = jnp.concatenate([exp_w_gate[i], exp_w_up[i]], axis=-1).astype(BF16)
        y_slots = _routed_experts(h2, block_e, n_valid, codes, w_gu, exp_w_down[i].astype(BF16))
        s_gu = jnp.concatenate([sh_w_gate[i], sh_w_up[i]], axis=-1).astype(BF16)
        shared = _shared_expert(h2, s_gu, sh_w_down[i].astype(BF16))
        if last:
            (xs,) = _moe_combine(xs, shared, y_slots, w, mod, 5, ln_g[i, 1], ln_b[i, 1], n,
                                 n_out_rows=n)
        else:
            xs, h = _moe_combine(xs, shared, y_slots, w, mod, 5, ln_g[i, 1], ln_b[i, 1], n,
                                 next_mod=mods[i + 1])
    return xs.reshape(b, n, d)
```

```python
import functools
import math

import jax
import jax.numpy as jnp
import numpy as np
from jax import lax
from jax.experimental import pallas as pl
from jax.experimental.pallas import tpu as pltpu

F32 = jnp.float32
BF16 = jnp.bfloat16

D_MODEL = 4096
DEPTH = 4
GRID_W = 64
N_MOD = 6
LN_EPS = 1e-5
RMS_EPS = 1e-6
DEEPNORM_ALPHA = (2 * DEPTH) ** 0.25
GM_CHUNK = 128
GM_GROUPS = 16
N_HEADS = 32
HEAD_DIM = 128
N_KV_HEADS = 8
ROPE_THETA = 10000.0
NA_WIN_H = 8
NA_WIN_W = 16
FT_GROUPS = 8
N_EXPERTS = 32
N_EXPERT_GROUPS = 4
TOP_K_GROUPS = 2
TOP_K = 4
D_EXPERT = 384
ROUTED_SCALE = 2.5

LANES = 128
VMEM_LIMIT_BYTES = 56 * 1024 * 1024
ROW_TILE = 256
MM_TILE_M = 1280
MM_TILE_N = 1024
MOE_ROWS = 256
COMBINE_ROWS = 128
NA_Q_ROWS = 4
NA_KEY_ROWS = 12
GQA_TQ = 256
GQA_TK = 2048
MASK_VALUE = -1e30


def _params(n_axes):
    return pltpu.CompilerParams(
        dimension_semantics=("arbitrary",) * n_axes,
        vmem_limit_bytes=VMEM_LIMIT_BYTES,
    )


def _dot(a, b, precision=None):
    return jnp.dot(a, b, preferred_element_type=F32, precision=precision)


def _dot_nt(a, b):
    return lax.dot_general(a, b, (((1,), (1,)), ((), ())), preferred_element_type=F32)


def _mm_kernel(a_ref, b_ref, o_ref, *, act):
    acc = _dot(a_ref[...], b_ref[...])
    if act == "gelu":
        acc = jax.nn.gelu(acc)
    o_ref[...] = acc.astype(o_ref.dtype)


def _matmul(a, b, *, out_dtype, act=None, tm=MM_TILE_M, tn=MM_TILE_N):
    m, k = a.shape
    n = b.shape[1]
    tm = min(tm, m)
    tn = min(tn, n)
    if jnp.dtype(out_dtype).itemsize == 4:
        tn = min(tn, MM_TILE_N // 2)
    while n % tn:
        tn //= 2
    assert m % tm == 0 and tn % LANES == 0
    return pl.pallas_call(
        functools.partial(_mm_kernel, act=act),
        grid=(m // tm, n // tn),
        in_specs=[
            pl.BlockSpec((tm, k), lambda i, j: (i, 0)),
            pl.BlockSpec((k, tn), lambda i, j: (0, j)),
        ],
        out_specs=pl.BlockSpec((tm, tn), lambda i, j: (i, j)),
        out_shape=jax.ShapeDtypeStruct((m, n), out_dtype),
        compiler_params=_params(2),
        name="matmul",
    )(a, b)


def _lmm_kernel(m_ref, x_ref, o_ref):
    o_ref[...] = _dot(m_ref[...], x_ref[...]).astype(o_ref.dtype)


def _left_matmul(mat, x, *, out_dtype, tn):
    r, k = mat.shape
    n = x.shape[1]
    assert n % tn == 0
    return pl.pallas_call(
        _lmm_kernel,
        grid=(n // tn,),
        in_specs=[
            pl.BlockSpec((r, k), lambda j: (0, 0)),
            pl.BlockSpec((k, tn), lambda j: (0, j)),
        ],
        out_specs=pl.BlockSpec((r, tn), lambda j: (0, j)),
        out_shape=jax.ShapeDtypeStruct((r, n), out_dtype),
        compiler_params=_params(1),
        name="left_matmul",
    )(mat, x)


def _mod_kernel(cond_ref, wd_ref, wu_ref, b_ref, o_ref):
    cond = cond_ref[...]
    t = _dot(cond * jax.nn.sigmoid(cond), wd_ref[...], precision=lax.Precision.HIGHEST)
    o_ref[...] = _dot(t, wu_ref[...], precision=lax.Precision.HIGHEST) + b_ref[...]


def _ada_mod_all(cond8, w_mod_down, w_mod_up, b_mod):
    depth, d, rank = w_mod_down.shape
    out = pl.pallas_call(
        _mod_kernel,
        grid=(depth, N_MOD),
        in_specs=[
            pl.BlockSpec((8, d), lambda l, j: (0, 0)),
            pl.BlockSpec((None, d, rank), lambda l, j: (l, 0, 0)),
            pl.BlockSpec((None, rank, d), lambda l, j: (l, 0, j)),
            pl.BlockSpec((None, None, 1, d), lambda l, j: (l, j, 0, 0)),
        ],
        out_specs=pl.BlockSpec((None, None, 8, d), lambda l, j: (l, j, 0, 0)),
        out_shape=jax.ShapeDtypeStruct((depth, N_MOD, 8, d), F32),
        compiler_params=_params(2),
        name="ada_mod",
    )(cond8, w_mod_down, w_mod_up, b_mod.reshape(depth, N_MOD, 1, d))
    return jnp.transpose(out[:, :, :2], (0, 2, 1, 3))[:, :, :, None, :]


def _mod_spec(slot, n_lat_blocks, d):
    return pl.BlockSpec((None, None, 1, d), lambda i: (i // n_lat_blocks, slot, 0, 0))


def _modulate_kernel(x_ref, sc_ref, sh_ref, h_ref):
    h_ref[...] = (x_ref[...] * (1.0 + sc_ref[...]) + sh_ref[...]).astype(h_ref.dtype)


def _modulate(x, mod, scale_slot, shift_slot, n_lat, out_dtype):
    t, d = x.shape
    tm = ROW_TILE
    nlb = n_lat // tm
    return pl.pallas_call(
        _modulate_kernel,
        grid=(t // tm,),
        in_specs=[
            pl.BlockSpec((tm, d), lambda i: (i, 0)),
            _mod_spec(scale_slot, nlb, d),
            _mod_spec(shift_slot, nlb, d),
        ],
        out_specs=pl.BlockSpec((tm, d), lambda i: (i, 0)),
        out_shape=jax.ShapeDtypeStruct((t, d), out_dtype),
        compiler_params=_params(1),
        name="modulate",
    )(x, mod, mod)


def _layer_norm_rows(z, g, b):
    mu = jnp.mean(z, axis=-1, keepdims=True)
    zc = z - mu
    var = jnp.mean(zc * zc, axis=-1, keepdims=True)
    return zc * lax.rsqrt(var + LN_EPS) * g + b


def _pack_halves(x):
    c = x.shape[1] // 2
    lo = lax.bitcast_convert_type(x[:, :c].astype(BF16).astype(F32), jnp.uint32)
    hi = lax.bitcast_convert_type(x[:, c:].astype(BF16).astype(F32), jnp.uint32)
    return hi | lax.shift_right_logical(lo, jnp.uint32(16))


def _unpack_low(u):
    return lax.bitcast_convert_type(lax.shift_left(u, jnp.uint32(16)), F32)


def _unpack_high(u):
    return lax.bitcast_convert_type(u & jnp.uint32(0xFFFF0000), F32)


def _unpack_halves(u):
    return jnp.concatenate([_unpack_low(u), _unpack_high(u)], axis=1)


def _resln_kernel(*refs, with_h, with_router):
    x_ref, o_ref, gate_ref, g_ref, b_ref = refs[:5]
    pos = 5
    if with_h:
        sc_ref, sh_ref = refs[pos:pos + 2]
        pos += 2
    if with_router:
        rw_ref = refs[pos]
        pos += 1
    xo_ref = refs[pos]
    pos += 1
    z = DEEPNORM_ALPHA * x_ref[...] + gate_ref[...] * o_ref[...].astype(F32)
    xn = _layer_norm_rows(z, g_ref[...], b_ref[...])
    xo_ref[...] = xn
    if with_h:
        h_ref = refs[pos]
        pos += 1
        h = xn * (1.0 + sc_ref[...]) + sh_ref[...]
        if h_ref.dtype == jnp.uint32:
            h_ref[...] = _pack_halves(h)
        else:
            h_ref[...] = h.astype(h_ref.dtype)
        if with_router:
            lg_ref = refs[pos]
            lg_ref[...] = _dot(h, rw_ref[...], precision=lax.Precision.HIGHEST)


def _resln(x, o, mod, gate_slot, ln_g, ln_b, n_lat, *, next_mod=None, scale_slot=None,
           shift_slot=None, h_dtype=None, router_w=None, n_out_rows=None):
    t, d = x.shape
    n_out_rows = t if n_out_rows is None else n_out_rows
    tm = ROW_TILE
    nlb = n_lat // tm
    with_h = next_mod is not None
    with_router = router_w is not None
    row = pl.BlockSpec((tm, d), lambda i: (i, 0))
    vec = pl.BlockSpec((1, d), lambda i: (0, 0))
    in_specs = [row, row, _mod_spec(gate_slot, nlb, d), vec, vec]
    args = [x, o, mod, ln_g.reshape(1, d), ln_b.reshape(1, d)]
    out_specs = [row]
    out_shape = [jax.ShapeDtypeStruct((n_out_rows, d), F32)]
    if with_h:
        in_specs += [_mod_spec(scale_slot, nlb, d), _mod_spec(shift_slot, nlb, d)]
        args += [next_mod, next_mod]
        h_cols = d // 2 if h_dtype == jnp.uint32 else d
        out_specs.append(pl.BlockSpec((tm, h_cols), lambda i: (i, 0)))
        out_shape.append(jax.ShapeDtypeStruct((n_out_rows, h_cols), h_dtype))
    if with_router:
        ne = router_w.shape[1]
        in_specs.append(pl.BlockSpec((d, ne), lambda i: (0, 0)))
        args.append(router_w)
        out_specs.append(pl.BlockSpec((tm, ne), lambda i: (i, 0)))
        out_shape.append(jax.ShapeDtypeStruct((n_out_rows, ne), F32))
    return pl.pallas_call(
        functools.partial(_resln_kernel, with_h=with_h, with_router=with_router),
        grid=(n_out_rows // tm,),
        in_specs=in_specs,
        out_specs=out_specs,
        out_shape=out_shape,
        compiler_params=_params(1),
        name="residual_layernorm",
    )(*args)


def _gmlp_gate_kernel(u_ref, v_ref, vg_ref, vb_ref, ws_ref, bs_ref, o_ref, *, n_groups, gw):
    vn = _layer_norm_rows(v_ref[...].astype(F32), vg_ref[...], vb_ref[...]).astype(BF16)
    for g in range(n_groups):
        cols = slice(g * gw, (g + 1) * gw)
        sv = _dot(ws_ref[g], vn[:, cols]) + bs_ref[:, cols]
        o_ref[:, cols] = (u_ref[:, cols].astype(F32) * sv).astype(o_ref.dtype)


def _gmlp_gate(uv, v_g, v_b, w_s, b_s):
    t, two_w = uv.shape
    width = two_w // 2
    n_groups, chunk, _ = w_s.shape
    gw = width // n_groups
    bias_full = jnp.repeat(b_s.T, gw, axis=1).astype(F32)
    return pl.pallas_call(
        functools.partial(_gmlp_gate_kernel, n_groups=n_groups, gw=gw),
        grid=(t // chunk,),
        in_specs=[
            pl.BlockSpec((chunk, width), lambda i: (i, 0)),
            pl.BlockSpec((chunk, width), lambda i: (i, 1)),
            pl.BlockSpec((1, width), lambda i: (0, 0)),
            pl.BlockSpec((1, width), lambda i: (0, 0)),
            pl.BlockSpec((n_groups, chunk, chunk), lambda i: (0, 0, 0)),
            pl.BlockSpec((chunk, width), lambda i: (0, 0)),
        ],
        out_specs=pl.BlockSpec((chunk, width), lambda i: (i, 0)),
        out_shape=jax.ShapeDtypeStruct((t, width), BF16),
        compiler_params=_params(1),
        name="gmlp_gate",
    )(uv, uv, v_g.reshape(1, width), v_b.reshape(1, width), w_s.astype(BF16), bias_full)


def _na_bias_table(rpb, rows):
    nb = rows // NA_Q_ROWS
    h = rpb.shape[0]
    n_dr = 2 * NA_WIN_H - 1
    c = np.arange(GRID_W)
    dc = c[None, :] - c[:, None] + NA_WIN_W - 1
    cs = np.clip(c - NA_WIN_W // 2, 0, GRID_W - NA_WIN_W)
    ok_c = (c[None, :] >= cs[:, None]) & (c[None, :] < cs[:, None] + NA_WIN_W)
    sel = ((np.arange(2 * NA_WIN_W - 1)[:, None, None] == dc[None]) & ok_c[None]).astype(np.float32)
    planes = jnp.einsum("hrd,dqk->hrqk", rpb.astype(F32), jnp.asarray(sel), precision=lax.Precision.HIGHEST)
    planes = jnp.where(ok_c[None, None], planes, MASK_VALUE)
    planes = jnp.concatenate([planes, jnp.full((h, 1, GRID_W, GRID_W), MASK_VALUE, F32)], axis=1)
    tables = []
    for b in (0, 1, nb - 1):
        r0 = b * NA_Q_ROWS
        kw0 = int(np.clip(r0 - NA_WIN_H // 2, 0, rows - NA_KEY_ROWS))
        r = r0 + np.arange(NA_Q_ROWS)[:, None]
        kr = kw0 + np.arange(NA_KEY_ROWS)[None, :]
        rs = np.clip(r - NA_WIN_H // 2, 0, rows - NA_WIN_H)
        ok_r = (kr >= rs) & (kr < rs + NA_WIN_H)
        dr = np.where(ok_r, kr - r + NA_WIN_H - 1, n_dr)
        tbl = jnp.stack([planes[:, int(d)] for d in dr.reshape(-1)], axis=1)
        tbl = tbl.reshape(h, NA_Q_ROWS, NA_KEY_ROWS, GRID_W, GRID_W).transpose(0, 1, 3, 2, 4)
        tables.append(tbl.reshape(h, NA_Q_ROWS * GRID_W, NA_KEY_ROWS * GRID_W))
    return jnp.stack(tables, axis=0)


def _na_kernel(q_ref, k_ref, v_ref, bias_ref, o_ref, *, rows, n_lat, n_ctx):
    b = pl.program_id(1)
    kw0 = jnp.clip(b * NA_Q_ROWS - NA_WIN_H // 2, 0, rows - NA_KEY_ROWS)
    start = pl.multiple_of(kw0 * GRID_W, GRID_W)
    nk = NA_KEY_ROWS * GRID_W
    scale = HEAD_DIM ** -0.5
    q = q_ref[...]
    kw = k_ref[pl.ds(start, nk), :]
    vw = v_ref[pl.ds(start, nk), :]
    kc = k_ref[pl.ds(n_lat, n_ctx), :]
    vc = v_ref[pl.ds(n_lat, n_ctx), :]
    s_loc = _dot_nt(q, kw) * scale + bias_ref[...]
    s_ctx = _dot_nt(q, kc) * scale
    m = jnp.maximum(jnp.max(s_loc, axis=-1, keepdims=True), jnp.max(s_ctx, axis=-1, keepdims=True))
    p_loc = jnp.exp(s_loc - m)
    p_ctx = jnp.exp(s_ctx - m)
    l = jnp.sum(p_loc, axis=-1, keepdims=True) + jnp.sum(p_ctx, axis=-1, keepdims=True)
    o = _dot(p_loc.astype(BF16), vw) + _dot(p_ctx.astype(BF16), vc)
    o_ref[...] = (o / l).astype(o_ref.dtype)


def _na_attention(qkv, bias_table, n_lat, n_ctx):
    t = qkv.shape[0]
    rows = n_lat // GRID_W
    nb = rows // NA_Q_ROWS
    tq = NA_Q_ROWS * GRID_W
    nk = NA_KEY_ROWS * GRID_W
    h = N_HEADS
    return pl.pallas_call(
        functools.partial(_na_kernel, rows=rows, n_lat=n_lat, n_ctx=n_ctx),
        grid=(h, nb),
        in_specs=[
            pl.BlockSpec((tq, HEAD_DIM), lambda hh, b: (b, hh)),
            pl.BlockSpec((t, HEAD_DIM), lambda hh, b: (0, h + hh)),
            pl.BlockSpec((t, HEAD_DIM), lambda hh, b: (0, 2 * h + hh)),
            pl.BlockSpec((None, None, tq, nk),
                         lambda hh, b: (jnp.where(b == 0, 0, jnp.where(b == nb - 1, 2, 1)), hh, 0, 0)),
        ],
        out_specs=pl.BlockSpec((tq, HEAD_DIM), lambda hh, b: (b, hh)),
        out_shape=jax.ShapeDtypeStruct((n_lat, h * HEAD_DIM), BF16),
        compiler_params=_params(2),
        name="neighbourhood_attention",
    )(qkv, qkv, qkv, bias_table)


def _ctx_attn_kernel(q_ref, k_ref, v_ref, o_ref):
    s = _dot_nt(q_ref[...], k_ref[...]) * (HEAD_DIM ** -0.5)
    m = jnp.max(s, axis=-1, keepdims=True)
    p = jnp.exp(s - m)
    l = jnp.sum(p, axis=-1, keepdims=True)
    o_ref[...] = (_dot(p.astype(BF16), v_ref[...]) / l).astype(o_ref.dtype)


def _ctx_attention(qkv, n_lat, n_ctx):
    h = N_HEADS
    rb = n_lat // n_ctx
    return pl.pallas_call(
        _ctx_attn_kernel,
        grid=(h,),
        in_specs=[
            pl.BlockSpec((n_ctx, HEAD_DIM), lambda hh: (rb, hh)),
            pl.BlockSpec((n_ctx, HEAD_DIM), lambda hh: (rb, h + hh)),
            pl.BlockSpec((n_ctx, HEAD_DIM), lambda hh: (rb, 2 * h + hh)),
        ],
        out_specs=pl.BlockSpec((n_ctx, HEAD_DIM), lambda hh: (0, hh)),
        out_shape=jax.ShapeDtypeStruct((n_ctx, h * HEAD_DIM), BF16),
        compiler_params=_params(1),
        name="context_attention",
    )(qkv, qkv, qkv)


def _dft_cos_sin(k_times_n, period):
    ang = (2.0 * math.pi / period) * (k_times_n % period).astype(F32)
    return jnp.cos(ang), -jnp.sin(ang)


def _ft_stage2_kernel(g_ref, y_ref, o_ref, *, nb):
    for i in range(nb):
        rhs = jnp.concatenate([y_ref[0, i], y_ref[1, i]], axis=0)
        res = _dot(g_ref[i], rhs)
        half = res.shape[0] // 2
        o_ref[0, :, i, :] = res[:half].astype(o_ref.dtype)
        o_ref[1, :, i, :] = res[half:].astype(o_ref.dtype)


def _ft_positions_latent(h_lat):
    n, d = h_lat.shape
    l1 = l2 = int(round(math.sqrt(n)))
    assert l1 * l2 == n
    idx = jnp.arange(l1, dtype=jnp.int32)
    re, im = _dft_cos_sin(idx[:, None] * idx[None, :], l1)
    m1 = (jnp.concatenate([re, im], axis=0) * (l1 ** -0.5)).astype(BF16)
    y = _left_matmul(m1, h_lat.reshape(l1, l2 * d), out_dtype=BF16, tn=8192)
    y = y.reshape(2, l1, l2, d)
    k1 = idx[:, None, None]
    k2 = idx[None, :, None]
    n2 = idx[None, None, :]
    gr, gi = _dft_cos_sin(n2 * k2 * l1 + n2 * k1, n)
    g = jnp.concatenate(
        [jnp.concatenate([gr, -gi], axis=2), jnp.concatenate([gi, gr], axis=2)], axis=1
    )
    g = (g * (l2 ** -0.5)).astype(BF16)
    nb, cb = 8, min(1024, d)
    p = pl.pallas_call(
        functools.partial(_ft_stage2_kernel, nb=nb),
        grid=(l1 // nb, d // cb),
        in_specs=[
            pl.BlockSpec((nb, 2 * l2, 2 * l2), lambda a, c: (a, 0, 0)),
            pl.BlockSpec((2, nb, l2, cb), lambda a, c: (0, a, 0, c)),
        ],
        out_specs=pl.BlockSpec((2, l2, nb, cb), lambda a, c: (0, 0, a, c)),
        out_shape=jax.ShapeDtypeStruct((2, l2, l1, d), BF16),
        compiler_params=_params(2),
        name="ft_stage2",
    )(g, y)
    return p.reshape(2, n, d)


def _ft_positions_dense(h_ctx):
    n, d = h_ctx.shape
    idx = jnp.arange(n, dtype=jnp.int32)
    re, im = _dft_cos_sin(idx[:, None] * idx[None, :], n)
    m = (jnp.concatenate([re, im], axis=0) * (n ** -0.5)).astype(BF16)
    return _left_matmul(m, h_ctx, out_dtype=BF16, tn=min(d, 2048)).reshape(2, n, d)


def _ft_channel_kernel(pr_ref, pi_ref, c_ref, s_ref, o_ref):
    o_ref[...] = (_dot(pr_ref[...], c_ref[...]) + _dot(pi_ref[...], s_ref[...])).astype(o_ref.dtype)


def _ft_channels(p):
    _, t, d = p.shape
    gc = d // FT_GROUPS
    idx = jnp.arange(gc, dtype=jnp.int32)
    re, im = _dft_cos_sin(idx[:, None] * idx[None, :], gc)
    cc = (re * (gc ** -0.5)).astype(BF16)
    sc = (-im * (gc ** -0.5)).astype(BF16)
    tm = MM_TILE_M if t % MM_TILE_M == 0 else t
    return pl.pallas_call(
        _ft_channel_kernel,
        grid=(t // tm, FT_GROUPS),
        in_specs=[
            pl.BlockSpec((None, tm, gc), lambda i, g: (0, i, g)),
            pl.BlockSpec((None, tm, gc), lambda i, g: (1, i, g)),
            pl.BlockSpec((gc, gc), lambda i, g: (0, 0)),
            pl.BlockSpec((gc, gc), lambda i, g: (0, 0)),
        ],
        out_specs=pl.BlockSpec((tm, gc), lambda i, g: (i, g)),
        out_shape=jax.ShapeDtypeStruct((t, d), BF16),
        compiler_params=_params(2),
        name="ft_channels",
    )(p, p, cc, sc)


def _rope_tables(n_lat, n_ctx):
    half = HEAD_DIM // 2
    t = jnp.arange(n_lat)
    inv = ROPE_THETA ** (-jnp.arange(0, half, 2, dtype=F32) / half)
    ang_r = (t // GRID_W).astype(F32)[:, None] * inv[None, :]
    ang_c = (t % GRID_W).astype(F32)[:, None] * inv[None, :]
    cos = jnp.concatenate([jnp.cos(ang_r)] * 2 + [jnp.cos(ang_c)] * 2, axis=-1)
    sin = jnp.concatenate([-jnp.sin(ang_r), jnp.sin(ang_r), -jnp.sin(ang_c), jnp.sin(ang_c)], axis=-1)
    cos = jnp.concatenate([cos, jnp.ones((n_ctx, HEAD_DIM), F32)], axis=0)
    sin = jnp.concatenate([sin, jnp.zeros((n_ctx, HEAD_DIM), F32)], axis=0)
    return cos, sin


def _norm_rope_kernel(x_ref, g_ref, cos_ref, sin_ref, o_ref, *, n_heads, scale):
    quarter = HEAD_DIM // 4
    lane = lax.broadcasted_iota(jnp.int32, (x_ref.shape[0], HEAD_DIM), 1)
    first = (lane % (2 * quarter)) < quarter
    cos = cos_ref[...]
    sin = sin_ref[...]
    g = g_ref[...]
    for hh in range(n_heads):
        cols = slice(hh * HEAD_DIM, (hh + 1) * HEAD_DIM)
        x = x_ref[:, cols].astype(F32)
        xn = x * lax.rsqrt(jnp.mean(x * x, axis=-1, keepdims=True) + RMS_EPS) * g
        partner = jnp.where(first, pltpu.roll(xn, HEAD_DIM - quarter, 1), pltpu.roll(xn, quarter, 1))
        o_ref[:, cols] = ((xn * cos + partner * sin) * scale).astype(o_ref.dtype)


def _norm_rope(qkv, col_block, n_heads, gain, cos, sin, scale):
    t = qkv.shape[0]
    tm = ROW_TILE
    w = n_heads * HEAD_DIM
    return pl.pallas_call(
        functools.partial(_norm_rope_kernel, n_heads=n_heads, scale=scale),
        grid=(t // tm,),
        in_specs=[
            pl.BlockSpec((tm, w), lambda i: (i, col_block)),
            pl.BlockSpec((1, HEAD_DIM), lambda i: (0, 0)),
            pl.BlockSpec((tm, HEAD_DIM), lambda i: (i, 0)),
            pl.BlockSpec((tm, HEAD_DIM), lambda i: (i, 0)),
        ],
        out_specs=pl.BlockSpec((tm, w), lambda i: (i, 0)),
        out_shape=jax.ShapeDtypeStruct((t, w), BF16),
        compiler_params=_params(1),
        name="norm_rope",
    )(qkv, gain.reshape(1, HEAD_DIM), cos, sin)


def _lane_tiles(x):
    return [x[:, j * LANES:(j + 1) * LANES] for j in range(x.shape[1] // LANES)]


FLASH_SUB_ROWS = 64


def _flash_scores(q_sc, kt, s_sc, slot):
    s_sc[slot, :, :kt.shape[1]] = _dot(q_sc[...], kt)


def _flash_softmax(s_sc, p_sc, alpha_sc, m_sc, slot, n):
    for b in range(s_sc.shape[1] // FLASH_SUB_ROWS):
        rows = slice(b * FLASH_SUB_ROWS, (b + 1) * FLASH_SUB_ROWS)
        tiles = _lane_tiles(s_sc[slot, rows, :n])
        m_prev = m_sc[rows]
        m_new = jnp.maximum(m_prev, jnp.max(functools.reduce(jnp.maximum, tiles), axis=-1, keepdims=True))
        alpha_sc[slot, rows] = jnp.exp2(m_prev - m_new)
        p_sc[slot, rows, :n] = jnp.concatenate([jnp.exp2(t - m_new).astype(BF16) for t in tiles], axis=1)
        m_sc[rows] = m_new


def _flash_accumulate(p_sc, alpha_sc, acc_sc, slot, vs):
    n = vs.shape[0]
    v_ext = jnp.concatenate([vs, jnp.ones_like(vs)], axis=1)
    alpha = alpha_sc[slot]
    acc_sc[...] = jnp.concatenate([alpha, alpha], axis=1) * acc_sc[...] + _dot(p_sc[slot, :, :n], v_ext)


def _flash_kernel(q_ref, k_ref, v_ref, o_ref, q_sc, s_sc, p_sc, alpha_sc, m_sc, acc_sc, *,
                  rep, tq, tk, n_main, n_tail):
    for r in range(rep):
        q_sc[r * tq:(r + 1) * tq, :] = q_ref[:, r * HEAD_DIM:(r + 1) * HEAD_DIM]
    m_sc[...] = jnp.full(m_sc.shape, -jnp.inf, F32)
    acc_sc[...] = jnp.zeros(acc_sc.shape, F32)
    n_pairs = n_main // (2 * tk)

    def body(c, carry):
        base = pl.multiple_of(c * 2 * tk, 2 * tk)
        nxt = pl.multiple_of(jnp.minimum(c + 1, n_pairs - 1) * 2 * tk, 2 * tk)
        _flash_scores(q_sc, k_ref[:, pl.ds(base + tk, tk)], s_sc, 1)
        _flash_softmax(s_sc, p_sc, alpha_sc, m_sc, 0, tk)
        _flash_accumulate(p_sc, alpha_sc, acc_sc, 0, v_ref[pl.ds(base, tk), :])
        _flash_scores(q_sc, k_ref[:, pl.ds(nxt, tk)], s_sc, 0)
        _flash_softmax(s_sc, p_sc, alpha_sc, m_sc, 1, tk)
        _flash_accumulate(p_sc, alpha_sc, acc_sc, 1, v_ref[pl.ds(base + tk, tk), :])
        return carry

    _flash_scores(q_sc, k_ref[:, pl.ds(0, tk)], s_sc, 0)
    lax.fori_loop(0, n_pairs, body, 0)
    if n_tail:
        _flash_scores(q_sc, k_ref[:, pl.ds(n_main, n_tail)], s_sc, 0)
        _flash_softmax(s_sc, p_sc, alpha_sc, m_sc, 0, n_tail)
        _flash_accumulate(p_sc, alpha_sc, acc_sc, 0, v_ref[pl.ds(n_main, n_tail), :])
    for r in range(rep):
        rows = slice(r * tq, (r + 1) * tq)
        o_ref[:, r * HEAD_DIM:(r + 1) * HEAD_DIM] = (
            acc_sc[rows, :HEAD_DIM] / acc_sc[rows, HEAD_DIM:]).astype(o_ref.dtype)


def _gqa_attention(q, k_t, qkv, v_col_block0, n_q):
    t = k_t.shape[1]
    rep = N_HEADS // N_KV_HEADS
    tq, tk = GQA_TQ, GQA_TK
    n_main = t // (2 * tk) * (2 * tk)
    n_tail = t - n_main
    assert n_q % tq == 0 and n_tail % LANES == 0 and n_tail <= tk
    w = rep * HEAD_DIM
    return pl.pallas_call(
        functools.partial(_flash_kernel, rep=rep, tq=tq, tk=tk, n_main=n_main, n_tail=n_tail),
        grid=(N_KV_HEADS, n_q // tq),
        in_specs=[
            pl.BlockSpec((tq, w), lambda g, i: (i, g)),
            pl.BlockSpec((HEAD_DIM, t), lambda g, i: (g, 0)),
            pl.BlockSpec((t, HEAD_DIM), lambda g, i: (0, v_col_block0 + g)),
        ],
        out_specs=pl.BlockSpec((tq, w), lambda g, i: (i, g)),
        out_shape=jax.ShapeDtypeStruct((n_q, N_HEADS * HEAD_DIM), BF16),
        scratch_shapes=[
            pltpu.VMEM((rep * tq, HEAD_DIM), BF16),
            pltpu.VMEM((2, rep * tq, tk), F32),
            pltpu.VMEM((2, rep * tq, tk), BF16),
            pltpu.VMEM((2, rep * tq, HEAD_DIM), F32),
            pltpu.VMEM((rep * tq, HEAD_DIM), F32),
            pltpu.VMEM((rep * tq, 2 * HEAD_DIM), F32),
        ],
        compiler_params=_params(2),
        name="gqa_flash_attention",
    )(q, k_t, qkv)


def _route(logits, router_bias, block_rows):
    n_tok, n_e = logits.shape
    scores = jax.nn.sigmoid(logits)
    biased = scores + router_bias.astype(F32)
    per_g = n_e // N_EXPERT_GROUPS
    grp = biased.reshape(n_tok, N_EXPERT_GROUPS, per_g)
    g_first, g_mask1 = _first_max(grp)
    g_second, _ = _first_max(jnp.where(g_mask1, -jnp.inf, grp))
    g_score = g_first + g_second
    g_sel = jnp.zeros(g_score.shape, bool)
    for _ in range(TOP_K_GROUPS):
        _, pick = _first_max(jnp.where(g_sel, -jnp.inf, g_score))
        g_sel = g_sel | pick
    e_mask = jnp.repeat(g_sel, per_g, axis=1)
    cand = jnp.where(e_mask, biased, -jnp.inf)
    picks = []
    for _ in range(TOP_K):
        _, pick = _first_max(cand)
        picks.append(pick)
        cand = jnp.where(pick, -jnp.inf, cand)
    onehot = jnp.stack(picks, axis=1)
    w = jnp.sum(jnp.where(onehot, scores[:, None, :], 0.0), axis=-1)
    w = w / jnp.sum(w, axis=-1, keepdims=True) * ROUTED_SCALE

    n_asg = n_tok * TOP_K
    oh = onehot.reshape(n_asg, n_e).astype(jnp.int32)
    csum = jnp.cumsum(oh, axis=0)
    counts = csum[-1]
    padded = (counts + block_rows - 1) // block_rows * block_rows
    pad_end = jnp.cumsum(padded)
    pad_start = pad_end - padded
    dest = jnp.sum(oh * (csum - 1 + pad_start[None, :]), axis=1).astype(jnp.int32)
    n_blocks = -(-(n_asg + n_e * (block_rows - 1)) // block_rows)
    blk_row0 = jnp.arange(n_blocks, dtype=jnp.int32) * block_rows
    block_e = jnp.minimum(jnp.sum((pad_end[None, :] <= blk_row0[:, None]).astype(jnp.int32), axis=1), n_e - 1)
    be_oh = (block_e[:, None] == jnp.arange(n_e, dtype=jnp.int32)[None, :]).astype(jnp.int32)
    seg_left = jnp.sum(be_oh * (counts + pad_start)[None, :], axis=1) - blk_row0
    n_valid = jnp.clip(seg_left, 0, block_rows).astype(jnp.int32)
    codes = _plan_rows(dest, n_blocks * block_rows, block_rows)
    return w, block_e.astype(jnp.int32), n_valid, codes


def _first_max(x):
    m = jnp.max(x, axis=-1, keepdims=True)
    idx = lax.broadcasted_iota(jnp.int32, x.shape, x.ndim - 1)
    first = jnp.min(jnp.where(x == m, idx, x.shape[-1]), axis=-1, keepdims=True)
    return m[..., 0], idx == first


CODE_SLOT_SHIFT = 16
EXPERT_K_CHUNK = 512


def _plan_kernel(dest_hbm, codes_ref, dest_smem, sem, *, chunk, block_rows):
    j = pl.program_id(0)
    n_rows = codes_ref.shape[0]
    log_rows = block_rows.bit_length() - 1
    log_k = TOP_K.bit_length() - 1

    @pl.when(j == 0)
    def _():
        def init(r, c):
            parity = lax.shift_right_logical(r, log_rows) & 1
            codes_ref[r] = ((r & (block_rows - 1)) + parity * block_rows) | (TOP_K << CODE_SLOT_SHIFT)
            return c

        lax.fori_loop(0, n_rows, init, 0, unroll=16)

    cp = pltpu.make_async_copy(dest_hbm.at[j], dest_smem, sem)
    cp.start()
    cp.wait()

    def place(a, c):
        g = j * chunk + a
        codes_ref[dest_smem[a]] = lax.shift_right_logical(g, log_k) | ((g & (TOP_K - 1)) << CODE_SLOT_SHIFT)
        return c

    lax.fori_loop(0, chunk, place, 0, unroll=16)


def _plan_rows(dest, n_rows, block_rows):
    n_asg = dest.shape[0]
    assert block_rows & (block_rows - 1) == 0 and TOP_K & (TOP_K - 1) == 0
    chunk = max(c for c in range(LANES, 8192 + 1, LANES) if n_asg % c == 0)
    n_chunks = n_asg // chunk
    return pl.pallas_call(
        functools.partial(_plan_kernel, chunk=chunk, block_rows=block_rows),
        grid=(n_chunks,),
        in_specs=[pl.BlockSpec(memory_space=pl.ANY)],
        out_specs=pl.BlockSpec(memory_space=pltpu.SMEM),
        out_shape=jax.ShapeDtypeStruct((n_rows,), jnp.int32),
        scratch_shapes=[pltpu.SMEM((chunk,), jnp.int32), pltpu.SemaphoreType.DMA(())],
        compiler_params=_params(1),
        name="plan_rows",
    )(dest.reshape(n_chunks, chunk))


def _swiglu_rows(x, wgu_ref, wd_ref):
    hgu = _dot(x, wgu_ref[...])
    de = hgu.shape[1] // 2
    gate = hgu[:, :de]
    hb = (gate * jax.nn.sigmoid(gate) * hgu[:, de:]).astype(BF16)
    return _dot(hb, wd_ref[...])


def _expert_kernel(be_ref, nv_ref, codes_ref, t_hbm, wgu_ref, wd_ref, y_hbm,
                   xbuf, ybuf, sem_in, sem_out, *, n_tok, br):
    i = pl.program_id(0)
    last_block = pl.num_programs(0) - 2
    slot = i % 2
    active = nv_ref[i] > 0
    prev_active = (i > 0) & (nv_ref[jnp.maximum(i - 1, 0)] > 0)
    token_mask = (1 << CODE_SLOT_SHIFT) - 1
    half = xbuf.shape[2]
    n_chunks = 2 * half // EXPERT_K_CHUNK
    group = br // n_chunks

    def gather(block, buf_slot, lo, hi):
        for r in range(lo, hi):
            tok = codes_ref[block * br + r] & token_mask
            pltpu.make_async_copy(t_hbm.at[pl.ds(tok, 1)], xbuf.at[buf_slot, pl.ds(r, 1)],
                                  sem_in.at[buf_slot]).start(priority=r % 2)

    def gather_wait(buf_slot):
        pltpu.make_async_copy(t_hbm.at[pl.ds(0, br)], xbuf.at[buf_slot], sem_in.at[buf_slot]).wait()

    def scatter(block, buf_slot, lo, hi):
        for r in range(lo, hi):
            code = codes_ref[block * br + r]
            dst = lax.shift_right_logical(code, CODE_SLOT_SHIFT) * n_tok + (code & token_mask)
            pltpu.make_async_copy(ybuf.at[buf_slot, pl.ds(r, 1)], y_hbm.at[pl.ds(dst, 1)],
                                  sem_out.at[buf_slot]).start(priority=r % 2)

    def scatter_wait(buf_slot):
        pltpu.make_async_copy(ybuf.at[buf_slot], y_hbm.at[pl.ds(0, br)], sem_out.at[buf_slot]).wait()

    def compute(with_scatter):
        nxt = jnp.minimum(i + 1, last_block)
        gather_wait(slot)
        hgu = None
        for c in range(n_chunks):
            cols = slice(c * EXPERT_K_CHUNK, (c + 1) * EXPERT_K_CHUNK)
            lo = c * EXPERT_K_CHUNK % half
            words = xbuf[slot, :, lo:lo + EXPERT_K_CHUNK]
            xc = _unpack_low(words) if c * EXPERT_K_CHUNK < half else _unpack_high(words)
            part = _dot(xc.astype(BF16), wgu_ref[cols, :])
            hgu = part if hgu is None else hgu + part
            gather(nxt, 1 - slot, c * group, (c + 1) * group)
            if with_scatter:
                scatter(i - 1, 1 - slot, c * group, (c + 1) * group)
        de = hgu.shape[1] // 2
        gate = hgu[:, :de]
        hb = (gate * jax.nn.sigmoid(gate) * hgu[:, de:]).astype(BF16)
        ybuf[slot] = _pack_halves(_dot(hb, wd_ref[...]))
        if with_scatter:
            scatter_wait(1 - slot)

    @pl.when(i == 0)
    def _():
        gather(0, 0, 0, br)

    @pl.when(active & (i == 0))
    def _():
        compute(False)

    @pl.when(active & (i > 0))
    def _():
        compute(True)

    @pl.when(jnp.logical_not(active) & ((i == 0) | prev_active))
    def _():
        gather_wait(slot)

        @pl.when(prev_active)
        def _():
            scatter(i - 1, 1 - slot, 0, br)
            scatter_wait(1 - slot)


def _routed_experts(tok_packed, block_e, n_valid, codes, w_gu, w_down):
    n_tok, half = tok_packed.shape
    d = 2 * half
    n_blocks = block_e.shape[0]
    br = codes.shape[0] // n_blocks
    de2 = w_gu.shape[2]
    assert half % EXPERT_K_CHUNK == 0 and br % (d // EXPERT_K_CHUNK) == 0
    block_e = jnp.concatenate([block_e, block_e[-1:]])
    n_valid = jnp.concatenate([n_valid, jnp.zeros((1,), jnp.int32)])
    grid_spec = pltpu.PrefetchScalarGridSpec(
        num_scalar_prefetch=3,
        grid=(n_blocks + 1,),
        in_specs=[
            pl.BlockSpec(memory_space=pl.ANY),
            pl.BlockSpec((None, d, de2), lambda i, be, nv, cd: (be[i], 0, 0)),
            pl.BlockSpec((None, de2 // 2, d), lambda i, be, nv, cd: (be[i], 0, 0)),
        ],
        out_specs=pl.BlockSpec(memory_space=pl.ANY),
        scratch_shapes=[
            pltpu.VMEM((2, br, half), jnp.uint32),
            pltpu.VMEM((2, br, half), jnp.uint32),
            pltpu.SemaphoreType.DMA((2,)),
            pltpu.SemaphoreType.DMA((2,)),
        ],
    )
    return pl.pallas_call(
        functools.partial(_expert_kernel, n_tok=n_tok, br=br),
        grid_spec=grid_spec,
        out_shape=jax.ShapeDtypeStruct((TOP_K * n_tok + 2 * br, half), jnp.uint32),
        compiler_params=_params(1),
        name="routed_experts",
    )(block_e, n_valid, codes, tok_packed, w_gu, w_down)


def _combine_kernel(*refs, with_h):
    (x_ref, tok_ref, sgu_ref, sd_ref, y0_ref, y1_ref, y2_ref, y3_ref, w_ref, gate_ref, g_ref,
     b_ref) = refs[:12]
    w = w_ref[...]
    y = _swiglu_rows(_unpack_halves(tok_ref[...]).astype(BF16), sgu_ref, sd_ref)
    for k, y_ref in enumerate((y0_ref, y1_ref, y2_ref, y3_ref)):
        y = y + w[:, k:k + 1] * _unpack_halves(y_ref[...])
    z = DEEPNORM_ALPHA * x_ref[...] + gate_ref[...] * y
    xn = _layer_norm_rows(z, g_ref[...], b_ref[...])
    if with_h:
        sc_ref, shf_ref, xo_ref, h_ref = refs[12:16]
        xo_ref[...] = xn
        h_ref[...] = (xn * (1.0 + sc_ref[...]) + shf_ref[...]).astype(h_ref.dtype)
    else:
        xo_ref = refs[12]
        xo_ref[...] = xn


def _moe_combine(x, tok_packed, s_gu, s_down, y_slots, w, mod, gate_slot, ln_g, ln_b, n_lat, *,
                 next_mod=None, n_out_rows=None):
    t, d = x.shape
    n_out_rows = t if n_out_rows is None else n_out_rows
    tm = COMBINE_ROWS
    nlb = n_lat // tm
    with_h = next_mod is not None
    y3 = y_slots
    de2 = s_gu.shape[1]
    row = pl.BlockSpec((tm, d), lambda i: (i, 0))
    packed_row = pl.BlockSpec((tm, d // 2), lambda i: (i, 0))
    vec = pl.BlockSpec((1, d), lambda i: (0, 0))

    def slot_spec(k):
        return pl.BlockSpec((tm, d // 2), lambda i: (k * (t // tm) + i, 0))

    in_specs = [
        row, packed_row,
        pl.BlockSpec((d, de2), lambda i: (0, 0)),
        pl.BlockSpec((de2 // 2, d), lambda i: (0, 0)),
    ] + [slot_spec(k) for k in range(TOP_K)] + [
        pl.BlockSpec((tm, TOP_K), lambda i: (i, 0)),
        _mod_spec(gate_slot, nlb, d), vec, vec,
    ]
    args = [x, tok_packed, s_gu, s_down, y3, y3, y3, y3, w, mod, ln_g.reshape(1, d), ln_b.reshape(1, d)]
    out_specs = [row]
    out_shape = [jax.ShapeDtypeStruct((n_out_rows, d), F32)]
    if with_h:
        in_specs += [_mod_spec(1, nlb, d), _mod_spec(0, nlb, d)]
        args += [next_mod, next_mod]
        out_specs.append(row)
        out_shape.append(jax.ShapeDtypeStruct((n_out_rows, d), BF16))
    return pl.pallas_call(
        functools.partial(_combine_kernel, with_h=with_h),
        grid=(n_out_rows // tm,),
        in_specs=in_specs,
        out_specs=out_specs,
        out_shape=out_shape,
        compiler_params=_params(1),
        name="moe_combine",
    )(*args)


def kernel(x, c, ctx, c_ctx, w_mod_down, w_mod_up, b_mod, ln_g, ln_b, gm_w_in, gm_v_g, gm_v_b, gm_w_s, gm_b_s, gm_w_out, na_w_qkv, na_rpb, na_w_out, ft_w_out, ga_w_qkv, ga_q_g, ga_k_g, ga_w_out, router_w, router_bias, exp_w_gate, exp_w_up, exp_w_down, sh_w_gate, sh_w_up, sh_w_down):
    b, n, d = x.shape
    nc = ctx.shape[1]
    assert b == 1 and c.shape[0] == 1
    n_tok = n + nc
    rows = n // GRID_W

    xs = jnp.concatenate([x[0], ctx[0]], axis=0)
    cond8 = jnp.zeros((8, d), F32).at[0].set(c[0]).at[1].set(c_ctx)
    mods = _ada_mod_all(cond8, w_mod_down, w_mod_up, b_mod)
    router_w_pad = jnp.pad(router_w, ((0, 0), (0, 0), (0, LANES - N_EXPERTS)))

    h = _modulate(xs, mods[0], 1, 0, n, BF16)
    for i in range(DEPTH):
        kind = i % 4
        mod = mods[i]
        last = i == DEPTH - 1

        if kind == 0:
            uv = _matmul(h, gm_w_in[0].astype(BF16), out_dtype=BF16, act="gelu")
            gated = _gmlp_gate(uv, gm_v_g[0], gm_v_b[0], gm_w_s[0], gm_b_s[0])
            o = _matmul(gated, gm_w_out[0].astype(BF16), out_dtype=F32)
        elif kind == 1:
            qkv = _matmul(h, na_w_qkv[0].astype(BF16), out_dtype=BF16)
            a_lat = _na_attention(qkv, _na_bias_table(na_rpb[0], rows), n, nc)
            a_ctx = _ctx_attention(qkv, n, nc)
            o = _matmul(jnp.concatenate([a_lat, a_ctx], axis=0), na_w_out[0].astype(BF16), out_dtype=F32)
        elif kind == 2:
            p = jnp.concatenate([_ft_positions_latent(h[:n]), _ft_positions_dense(h[n:])], axis=1)
            o = _matmul(_ft_channels(p), ft_w_out[0].astype(BF16), out_dtype=F32)
        else:
            qkv = _matmul(h, ga_w_qkv[0].astype(BF16), out_dtype=BF16, tn=1024)
            cos, sin = _rope_tables(n, nc)
            qn = _norm_rope(qkv, 0, N_HEADS, ga_q_g[0], cos, sin, HEAD_DIM ** -0.5 * math.log2(math.e))
            kn = _norm_rope(qkv, N_HEADS // N_KV_HEADS, N_KV_HEADS, ga_k_g[0], cos, sin, 1.0)
            att = _gqa_attention(qn, kn.T, qkv, (N_HEADS + N_KV_HEADS), n)
            att = jnp.concatenate([att, jnp.zeros((nc, d), BF16)], axis=0)
            o = _matmul(att, ga_w_out[0].astype(BF16), out_dtype=F32)

        xs, h2, logits = _resln(xs, o, mod, 2, ln_g[i, 0], ln_b[i, 0], n, next_mod=mod,
                                scale_slot=4, shift_slot=3, h_dtype=jnp.uint32, router_w=router_w_pad[i])

        w, block_e, n_valid, codes = _route(logits[:, :N_EXPERTS], router_bias[i], MOE_ROWS)
        w_gu = jnp.concatenate([exp_w_gate[i], exp_w_up[i]], axis=-1).astype(BF16)
        y_slots = _routed_experts(h2, block_e, n_valid, codes, w_gu, exp_w_down[i].astype(BF16))
        s_gu = jnp.concatenate([sh_w_gate[i], sh_w_up[i]], axis=-1).astype(BF16)
        s_down = sh_w_down[i].astype(BF16)
        if last:
            (xs,) = _moe_combine(xs, h2, s_gu, s_down, y_slots, w, mod, 5, ln_g[i, 1], ln_b[i, 1], n,
                                 n_out_rows=n)
        else:
            xs, h = _moe_combine(xs, h2, s_gu, s_down, y_slots, w, mod, 5, ln_g[i, 1], ln_b[i, 1], n,
                                 next_mod=mods[i + 1])
    return xs.reshape(b, n, d)
```

```python
import functools
import math

import jax
import jax.numpy as jnp
import numpy as np
from jax import lax
from jax.experimental import pallas as pl
from jax.experimental.pallas import tpu as pltpu

F32 = jnp.float32
BF16 = jnp.bfloat16

D_MODEL = 4096
DEPTH = 4
GRID_W = 64
N_MOD = 6
LN_EPS = 1e-5
RMS_EPS = 1e-6
DEEPNORM_ALPHA = (2 * DEPTH) ** 0.25
GM_CHUNK = 128
GM_GROUPS = 16
N_HEADS = 32
HEAD_DIM = 128
N_KV_HEADS = 8
ROPE_THETA = 10000.0
NA_WIN_H = 8
NA_WIN_W = 16
FT_GROUPS = 8
N_EXPERTS = 32
N_EXPERT_GROUPS = 4
TOP_K_GROUPS = 2
TOP_K = 4
D_EXPERT = 384
ROUTED_SCALE = 2.5

LANES = 128
VMEM_LIMIT_BYTES = 56 * 1024 * 1024
ROW_TILE = 256
MM_TILE_M = 1280
MM_TILE_N = 1024
MOE_ROWS = 256
COMBINE_ROWS = 128
NA_Q_ROWS = 4
NA_KEY_ROWS = 12
GQA_TQ = 256
GQA_TK = 2048
MASK_VALUE = -1e30


def _params(n_axes):
    return pltpu.CompilerParams(
        dimension_semantics=("arbitrary",) * n_axes,
        vmem_limit_bytes=VMEM_LIMIT_BYTES,
    )


def _dot(a, b, precision=None):
    return jnp.dot(a, b, preferred_element_type=F32, precision=precision)


def _dot_nt(a, b):
    return lax.dot_general(a, b, (((1,), (1,)), ((), ())), preferred_element_type=F32)


def _mm_kernel(a_ref, b_ref, o_ref, *, act):
    acc = _dot(a_ref[...], b_ref[...])
    if act == "gelu":
        acc = jax.nn.gelu(acc)
    o_ref[...] = acc.astype(o_ref.dtype)


def _matmul(a, b, *, out_dtype, act=None, tm=MM_TILE_M, tn=MM_TILE_N):
    m, k = a.shape
    n = b.shape[1]
    tm = min(tm, m)
    tn = min(tn, n)
    if jnp.dtype(out_dtype).itemsize == 4:
        tn = min(tn, MM_TILE_N // 2)
    while n % tn:
        tn //= 2
    assert m % tm == 0 and tn % LANES == 0
    return pl.pallas_call(
        functools.partial(_mm_kernel, act=act),
        grid=(m // tm, n // tn),
        in_specs=[
            pl.BlockSpec((tm, k), lambda i, j: (i, 0)),
            pl.BlockSpec((k, tn), lambda i, j: (0, j)),
        ],
        out_specs=pl.BlockSpec((tm, tn), lambda i, j: (i, j)),
        out_shape=jax.ShapeDtypeStruct((m, n), out_dtype),
        compiler_params=_params(2),
        name="matmul",
    )(a, b)


def _lmm_kernel(m_ref, x_ref, o_ref):
    o_ref[...] = _dot(m_ref[...], x_ref[...]).astype(o_ref.dtype)


def _left_matmul(mat, x, *, out_dtype, tn):
    r, k = mat.shape
    n = x.shape[1]
    assert n % tn == 0
    return pl.pallas_call(
        _lmm_kernel,
        grid=(n // tn,),
        in_specs=[
            pl.BlockSpec((r, k), lambda j: (0, 0)),
            pl.BlockSpec((k, tn), lambda j: (0, j)),
        ],
        out_specs=pl.BlockSpec((r, tn), lambda j: (0, j)),
        out_shape=jax.ShapeDtypeStruct((r, n), out_dtype),
        compiler_params=_params(1),
        name="left_matmul",
    )(mat, x)


def _mod_kernel(cond_ref, wd_ref, wu_ref, b_ref, o_ref):
    cond = cond_ref[...]
    t = _dot(cond * jax.nn.sigmoid(cond), wd_ref[...], precision=lax.Precision.HIGHEST)
    o_ref[...] = _dot(t, wu_ref[...], precision=lax.Precision.HIGHEST) + b_ref[...]


def _ada_mod_all(cond8, w_mod_down, w_mod_up, b_mod):
    depth, d, rank = w_mod_down.shape
    out = pl.pallas_call(
        _mod_kernel,
        grid=(depth, N_MOD),
        in_specs=[
            pl.BlockSpec((8, d), lambda l, j: (0, 0)),
            pl.BlockSpec((None, d, rank), lambda l, j: (l, 0, 0)),
            pl.BlockSpec((None, rank, d), lambda l, j: (l, 0, j)),
            pl.BlockSpec((None, None, 1, d), lambda l, j: (l, j, 0, 0)),
        ],
        out_specs=pl.BlockSpec((None, None, 8, d), lambda l, j: (l, j, 0, 0)),
        out_shape=jax.ShapeDtypeStruct((depth, N_MOD, 8, d), F32),
        compiler_params=_params(2),
        name="ada_mod",
    )(cond8, w_mod_down, w_mod_up, b_mod.reshape(depth, N_MOD, 1, d))
    return jnp.transpose(out[:, :, :2], (0, 2, 1, 3))[:, :, :, None, :]


def _mod_spec(slot, n_lat_blocks, d):
    return pl.BlockSpec((None, None, 1, d), lambda i: (i // n_lat_blocks, slot, 0, 0))


def _modulate_kernel(x_ref, sc_ref, sh_ref, h_ref):
    h_ref[...] = (x_ref[...] * (1.0 + sc_ref[...]) + sh_ref[...]).astype(h_ref.dtype)


def _modulate(x, mod, scale_slot, shift_slot, n_lat, out_dtype):
    t, d = x.shape
    tm = ROW_TILE
    nlb = n_lat // tm
    return pl.pallas_call(
        _modulate_kernel,
        grid=(t // tm,),
        in_specs=[
            pl.BlockSpec((tm, d), lambda i: (i, 0)),
            _mod_spec(scale_slot, nlb, d),
            _mod_spec(shift_slot, nlb, d),
        ],
        out_specs=pl.BlockSpec((tm, d), lambda i: (i, 0)),
        out_shape=jax.ShapeDtypeStruct((t, d), out_dtype),
        compiler_params=_params(1),
        name="modulate",
    )(x, mod, mod)


def _layer_norm_rows(z, g, b):
    mu = jnp.mean(z, axis=-1, keepdims=True)
    zc = z - mu
    var = jnp.mean(zc * zc, axis=-1, keepdims=True)
    return zc * lax.rsqrt(var + LN_EPS) * g + b


def _pack_halves(x):
    c = x.shape[1] // 2
    lo = lax.bitcast_convert_type(x[:, :c].astype(BF16).astype(F32), jnp.uint32)
    hi = lax.bitcast_convert_type(x[:, c:].astype(BF16).astype(F32), jnp.uint32)
    return hi | lax.shift_right_logical(lo, jnp.uint32(16))


def _unpack_low(u):
    return lax.bitcast_convert_type(lax.shift_left(u, jnp.uint32(16)), F32)


def _unpack_high(u):
    return lax.bitcast_convert_type(u & jnp.uint32(0xFFFF0000), F32)


def _unpack_halves(u):
    return jnp.concatenate([_unpack_low(u), _unpack_high(u)], axis=1)


def _resln_kernel(*refs, with_h, with_router):
    x_ref, o_ref, gate_ref, g_ref, b_ref = refs[:5]
    pos = 5
    if with_h:
        sc_ref, sh_ref = refs[pos:pos + 2]
        pos += 2
    if with_router:
        rw_ref = refs[pos]
        pos += 1
    xo_ref = refs[pos]
    pos += 1
    z = DEEPNORM_ALPHA * x_ref[...] + gate_ref[...] * o_ref[...].astype(F32)
    xn = _layer_norm_rows(z, g_ref[...], b_ref[...])
    xo_ref[...] = xn
    if with_h:
        h_ref = refs[pos]
        pos += 1
        h = xn * (1.0 + sc_ref[...]) + sh_ref[...]
        if h_ref.dtype == jnp.uint32:
            h_ref[...] = _pack_halves(h)
        else:
            h_ref[...] = h.astype(h_ref.dtype)
        if with_router:
            lg_ref = refs[pos]
            lg_ref[...] = _dot(h, rw_ref[...], precision=lax.Precision.HIGHEST)


def _resln(x, o, mod, gate_slot, ln_g, ln_b, n_lat, *, next_mod=None, scale_slot=None,
           shift_slot=None, h_dtype=None, router_w=None, n_out_rows=None):
    t, d = x.shape
    n_out_rows = t if n_out_rows is None else n_out_rows
    tm = ROW_TILE
    nlb = n_lat // tm
    with_h = next_mod is not None
    with_router = router_w is not None
    row = pl.BlockSpec((tm, d), lambda i: (i, 0))
    vec = pl.BlockSpec((1, d), lambda i: (0, 0))
    in_specs = [row, row, _mod_spec(gate_slot, nlb, d), vec, vec]
    args = [x, o, mod, ln_g.reshape(1, d), ln_b.reshape(1, d)]
    out_specs = [row]
    out_shape = [jax.ShapeDtypeStruct((n_out_rows, d), F32)]
    if with_h:
        in_specs += [_mod_spec(scale_slot, nlb, d), _mod_spec(shift_slot, nlb, d)]
        args += [next_mod, next_mod]
        h_cols = d // 2 if h_dtype == jnp.uint32 else d
        out_specs.append(pl.BlockSpec((tm, h_cols), lambda i: (i, 0)))
        out_shape.append(jax.ShapeDtypeStruct((n_out_rows, h_cols), h_dtype))
    if with_router:
        ne = router_w.shape[1]
        in_specs.append(pl.BlockSpec((d, ne), lambda i: (0, 0)))
        args.append(router_w)
        out_specs.append(pl.BlockSpec((tm, ne), lambda i: (i, 0)))
        out_shape.append(jax.ShapeDtypeStruct((n_out_rows, ne), F32))
    return pl.pallas_call(
        functools.partial(_resln_kernel, with_h=with_h, with_router=with_router),
        grid=(n_out_rows // tm,),
        in_specs=in_specs,
        out_specs=out_specs,
        out_shape=out_shape,
        compiler_params=_params(1),
        name="residual_layernorm",
    )(*args)


def _gmlp_gate_kernel(u_ref, v_ref, vg_ref, vb_ref, ws_ref, bs_ref, o_ref, *, n_groups, gw):
    vn = _layer_norm_rows(v_ref[...].astype(F32), vg_ref[...], vb_ref[...]).astype(BF16)
    for g in range(n_groups):
        cols = slice(g * gw, (g + 1) * gw)
        sv = _dot(ws_ref[g], vn[:, cols]) + bs_ref[:, cols]
        o_ref[:, cols] = (u_ref[:, cols].astype(F32) * sv).astype(o_ref.dtype)


def _gmlp_gate(uv, v_g, v_b, w_s, b_s):
    t, two_w = uv.shape
    width = two_w // 2
    n_groups, chunk, _ = w_s.shape
    gw = width // n_groups
    bias_full = jnp.repeat(b_s.T, gw, axis=1).astype(F32)
    return pl.pallas_call(
        functools.partial(_gmlp_gate_kernel, n_groups=n_groups, gw=gw),
        grid=(t // chunk,),
        in_specs=[
            pl.BlockSpec((chunk, width), lambda i: (i, 0)),
            pl.BlockSpec((chunk, width), lambda i: (i, 1)),
            pl.BlockSpec((1, width), lambda i: (0, 0)),
            pl.BlockSpec((1, width), lambda i: (0, 0)),
            pl.BlockSpec((n_groups, chunk, chunk), lambda i: (0, 0, 0)),
            pl.BlockSpec((chunk, width), lambda i: (0, 0)),
        ],
        out_specs=pl.BlockSpec((chunk, width), lambda i: (i, 0)),
        out_shape=jax.ShapeDtypeStruct((t, width), BF16),
        compiler_params=_params(1),
        name="gmlp_gate",
    )(uv, uv, v_g.reshape(1, width), v_b.reshape(1, width), w_s.astype(BF16), bias_full)


def _na_bias_table(rpb, rows):
    nb = rows // NA_Q_ROWS
    h = rpb.shape[0]
    n_dr = 2 * NA_WIN_H - 1
    c = np.arange(GRID_W)
    dc = c[None, :] - c[:, None] + NA_WIN_W - 1
    cs = np.clip(c - NA_WIN_W // 2, 0, GRID_W - NA_WIN_W)
    ok_c = (c[None, :] >= cs[:, None]) & (c[None, :] < cs[:, None] + NA_WIN_W)
    sel = ((np.arange(2 * NA_WIN_W - 1)[:, None, None] == dc[None]) & ok_c[None]).astype(np.float32)
    planes = jnp.einsum("hrd,dqk->hrqk", rpb.astype(F32), jnp.asarray(sel), precision=lax.Precision.HIGHEST)
    planes = jnp.where(ok_c[None, None], planes, MASK_VALUE)
    planes = jnp.concatenate([planes, jnp.full((h, 1, GRID_W, GRID_W), MASK_VALUE, F32)], axis=1)
    tables = []
    for b in (0, 1, nb - 1):
        r0 = b * NA_Q_ROWS
        kw0 = int(np.clip(r0 - NA_WIN_H // 2, 0, rows - NA_KEY_ROWS))
        r = r0 + np.arange(NA_Q_ROWS)[:, None]
        kr = kw0 + np.arange(NA_KEY_ROWS)[None, :]
        rs = np.clip(r - NA_WIN_H // 2, 0, rows - NA_WIN_H)
        ok_r = (kr >= rs) & (kr < rs + NA_WIN_H)
        dr = np.where(ok_r, kr - r + NA_WIN_H - 1, n_dr)
        tbl = jnp.stack([planes[:, int(d)] for d in dr.reshape(-1)], axis=1)
        tbl = tbl.reshape(h, NA_Q_ROWS, NA_KEY_ROWS, GRID_W, GRID_W).transpose(0, 1, 3, 2, 4)
        tables.append(tbl.reshape(h, NA_Q_ROWS * GRID_W, NA_KEY_ROWS * GRID_W))
    return jnp.stack(tables, axis=0)


def _na_kernel(q_ref, k_ref, v_ref, bias_ref, o_ref, *, rows, n_lat, n_ctx):
    b = pl.program_id(1)
    kw0 = jnp.clip(b * NA_Q_ROWS - NA_WIN_H // 2, 0, rows - NA_KEY_ROWS)
    start = pl.multiple_of(kw0 * GRID_W, GRID_W)
    nk = NA_KEY_ROWS * GRID_W
    scale = HEAD_DIM ** -0.5
    q = q_ref[...]
    kw = k_ref[pl.ds(start, nk), :]
    vw = v_ref[pl.ds(start, nk), :]
    kc = k_ref[pl.ds(n_lat, n_ctx), :]
    vc = v_ref[pl.ds(n_lat, n_ctx), :]
    s_loc = _dot_nt(q, kw) * scale + bias_ref[...]
    s_ctx = _dot_nt(q, kc) * scale
    m = jnp.maximum(jnp.max(s_loc, axis=-1, keepdims=True), jnp.max(s_ctx, axis=-1, keepdims=True))
    p_loc = jnp.exp(s_loc - m)
    p_ctx = jnp.exp(s_ctx - m)
    l = jnp.sum(p_loc, axis=-1, keepdims=True) + jnp.sum(p_ctx, axis=-1, keepdims=True)
    o = _dot(p_loc.astype(BF16), vw) + _dot(p_ctx.astype(BF16), vc)
    o_ref[...] = (o / l).astype(o_ref.dtype)


def _na_attention(qkv, bias_table, n_lat, n_ctx):
    t = qkv.shape[0]
    rows = n_lat // GRID_W
    nb = rows // NA_Q_ROWS
    tq = NA_Q_ROWS * GRID_W
    nk = NA_KEY_ROWS * GRID_W
    h = N_HEADS
    return pl.pallas_call(
        functools.partial(_na_kernel, rows=rows, n_lat=n_lat, n_ctx=n_ctx),
        grid=(h, nb),
        in_specs=[
            pl.BlockSpec((tq, HEAD_DIM), lambda hh, b: (b, hh)),
            pl.BlockSpec((t, HEAD_DIM), lambda hh, b: (0, h + hh)),
            pl.BlockSpec((t, HEAD_DIM), lambda hh, b: (0, 2 * h + hh)),
            pl.BlockSpec((None, None, tq, nk),
                         lambda hh, b: (jnp.where(b == 0, 0, jnp.where(b == nb - 1, 2, 1)), hh, 0, 0)),
        ],
        out_specs=pl.BlockSpec((tq, HEAD_DIM), lambda hh, b: (b, hh)),
        out_shape=jax.ShapeDtypeStruct((n_lat, h * HEAD_DIM), BF16),
        compiler_params=_params(2),
        name="neighbourhood_attention",
    )(qkv, qkv, qkv, bias_table)


def _ctx_attn_kernel(q_ref, k_ref, v_ref, o_ref):
    s = _dot_nt(q_ref[...], k_ref[...]) * (HEAD_DIM ** -0.5)
    m = jnp.max(s, axis=-1, keepdims=True)
    p = jnp.exp(s - m)
    l = jnp.sum(p, axis=-1, keepdims=True)
    o_ref[...] = (_dot(p.astype(BF16), v_ref[...]) / l).astype(o_ref.dtype)


def _ctx_attention(qkv, n_lat, n_ctx):
    h = N_HEADS
    rb = n_lat // n_ctx
    return pl.pallas_call(
        _ctx_attn_kernel,
        grid=(h,),
        in_specs=[
            pl.BlockSpec((n_ctx, HEAD_DIM), lambda hh: (rb, hh)),
            pl.BlockSpec((n_ctx, HEAD_DIM), lambda hh: (rb, h + hh)),
            pl.BlockSpec((n_ctx, HEAD_DIM), lambda hh: (rb, 2 * h + hh)),
        ],
        out_specs=pl.BlockSpec((n_ctx, HEAD_DIM), lambda hh: (0, hh)),
        out_shape=jax.ShapeDtypeStruct((n_ctx, h * HEAD_DIM), BF16),
        compiler_params=_params(1),
        name="context_attention",
    )(qkv, qkv, qkv)


def _dft_cos_sin(k_times_n, period):
    ang = (2.0 * math.pi / period) * (k_times_n % period).astype(F32)
    return jnp.cos(ang), -jnp.sin(ang)


def _ft_stage2_kernel(g_ref, y_ref, o_ref, *, nb):
    for i in range(nb):
        rhs = jnp.concatenate([y_ref[0, i], y_ref[1, i]], axis=0)
        res = _dot(g_ref[i], rhs)
        half = res.shape[0] // 2
        o_ref[0, :, i, :] = res[:half].astype(o_ref.dtype)
        o_ref[1, :, i, :] = res[half:].astype(o_ref.dtype)


def _ft_positions_latent(h_lat):
    n, d = h_lat.shape
    l1 = l2 = int(round(math.sqrt(n)))
    assert l1 * l2 == n
    idx = jnp.arange(l1, dtype=jnp.int32)
    re, im = _dft_cos_sin(idx[:, None] * idx[None, :], l1)
    m1 = (jnp.concatenate([re, im], axis=0) * (l1 ** -0.5)).astype(BF16)
    y = _left_matmul(m1, h_lat.reshape(l1, l2 * d), out_dtype=BF16, tn=8192)
    y = y.reshape(2, l1, l2, d)
    k1 = idx[:, None, None]
    k2 = idx[None, :, None]
    n2 = idx[None, None, :]
    gr, gi = _dft_cos_sin(n2 * k2 * l1 + n2 * k1, n)
    g = jnp.concatenate(
        [jnp.concatenate([gr, -gi], axis=2), jnp.concatenate([gi, gr], axis=2)], axis=1
    )
    g = (g * (l2 ** -0.5)).astype(BF16)
    nb, cb = 8, min(1024, d)
    p = pl.pallas_call(
        functools.partial(_ft_stage2_kernel, nb=nb),
        grid=(l1 // nb, d // cb),
        in_specs=[
            pl.BlockSpec((nb, 2 * l2, 2 * l2), lambda a, c: (a, 0, 0)),
            pl.BlockSpec((2, nb, l2, cb), lambda a, c: (0, a, 0, c)),
        ],
        out_specs=pl.BlockSpec((2, l2, nb, cb), lambda a, c: (0, 0, a, c)),
        out_shape=jax.ShapeDtypeStruct((2, l2, l1, d), BF16),
        compiler_params=_params(2),
        name="ft_stage2",
    )(g, y)
    return p.reshape(2, n, d)


def _ft_positions_dense(h_ctx):
    n, d = h_ctx.shape
    idx = jnp.arange(n, dtype=jnp.int32)
    re, im = _dft_cos_sin(idx[:, None] * idx[None, :], n)
    m = (jnp.concatenate([re, im], axis=0) * (n ** -0.5)).astype(BF16)
    return _left_matmul(m, h_ctx, out_dtype=BF16, tn=min(d, 2048)).reshape(2, n, d)


def _ft_channel_kernel(pr_ref, pi_ref, c_ref, s_ref, o_ref):
    o_ref[...] = (_dot(pr_ref[...], c_ref[...]) + _dot(pi_ref[...], s_ref[...])).astype(o_ref.dtype)


def _ft_channels(p):
    _, t, d = p.shape
    gc = d // FT_GROUPS
    idx = jnp.arange(gc, dtype=jnp.int32)
    re, im = _dft_cos_sin(idx[:, None] * idx[None, :], gc)
    cc = (re * (gc ** -0.5)).astype(BF16)
    sc = (-im * (gc ** -0.5)).astype(BF16)
    tm = MM_TILE_M if t % MM_TILE_M == 0 else t
    return pl.pallas_call(
        _ft_channel_kernel,
        grid=(t // tm, FT_GROUPS),
        in_specs=[
            pl.BlockSpec((None, tm, gc), lambda i, g: (0, i, g)),
            pl.BlockSpec((None, tm, gc), lambda i, g: (1, i, g)),
            pl.BlockSpec((gc, gc), lambda i, g: (0, 0)),
            pl.BlockSpec((gc, gc), lambda i, g: (0, 0)),
        ],
        out_specs=pl.BlockSpec((tm, gc), lambda i, g: (i, g)),
        out_shape=jax.ShapeDtypeStruct((t, d), BF16),
        compiler_params=_params(2),
        name="ft_channels",
    )(p, p, cc, sc)


def _rope_tables(n_lat, n_ctx):
    half = HEAD_DIM // 2
    t = jnp.arange(n_lat)
    inv = ROPE_THETA ** (-jnp.arange(0, half, 2, dtype=F32) / half)
    ang_r = (t // GRID_W).astype(F32)[:, None] * inv[None, :]
    ang_c = (t % GRID_W).astype(F32)[:, None] * inv[None, :]
    cos = jnp.concatenate([jnp.cos(ang_r)] * 2 + [jnp.cos(ang_c)] * 2, axis=-1)
    sin = jnp.concatenate([-jnp.sin(ang_r), jnp.sin(ang_r), -jnp.sin(ang_c), jnp.sin(ang_c)], axis=-1)
    cos = jnp.concatenate([cos, jnp.ones((n_ctx, HEAD_DIM), F32)], axis=0)
    sin = jnp.concatenate([sin, jnp.zeros((n_ctx, HEAD_DIM), F32)], axis=0)
    return cos, sin


def _norm_rope_kernel(x_ref, g_ref, cos_ref, sin_ref, o_ref, *, n_heads, scale):
    quarter = HEAD_DIM // 4
    lane = lax.broadcasted_iota(jnp.int32, (x_ref.shape[0], HEAD_DIM), 1)
    first = (lane % (2 * quarter)) < quarter
    cos = cos_ref[...]
    sin = sin_ref[...]
    g = g_ref[...]
    for hh in range(n_heads):
        cols = slice(hh * HEAD_DIM, (hh + 1) * HEAD_DIM)
        x = x_ref[:, cols].astype(F32)
        xn = x * lax.rsqrt(jnp.mean(x * x, axis=-1, keepdims=True) + RMS_EPS) * g
        partner = jnp.where(first, pltpu.roll(xn, HEAD_DIM - quarter, 1), pltpu.roll(xn, quarter, 1))
        o_ref[:, cols] = ((xn * cos + partner * sin) * scale).astype(o_ref.dtype)


def _norm_rope(qkv, col_block, n_heads, gain, cos, sin, scale):
    t = qkv.shape[0]
    tm = ROW_TILE
    w = n_heads * HEAD_DIM
    return pl.pallas_call(
        functools.partial(_norm_rope_kernel, n_heads=n_heads, scale=scale),
        grid=(t // tm,),
        in_specs=[
            pl.BlockSpec((tm, w), lambda i: (i, col_block)),
            pl.BlockSpec((1, HEAD_DIM), lambda i: (0, 0)),
            pl.BlockSpec((tm, HEAD_DIM), lambda i: (i, 0)),
            pl.BlockSpec((tm, HEAD_DIM), lambda i: (i, 0)),
        ],
        out_specs=pl.BlockSpec((tm, w), lambda i: (i, 0)),
        out_shape=jax.ShapeDtypeStruct((t, w), BF16),
        compiler_params=_params(1),
        name="norm_rope",
    )(qkv, gain.reshape(1, HEAD_DIM), cos, sin)


def _lane_tiles(x):
    return [x[:, j * LANES:(j + 1) * LANES] for j in range(x.shape[1] // LANES)]


FLASH_SUB_ROWS = 64


def _flash_scores(q_sc, kt, s_sc, slot):
    s_sc[slot, :, :kt.shape[1]] = _dot(q_sc[...], kt)


def _flash_softmax(s_sc, p_sc, alpha_sc, m_sc, slot, n):
    for b in range(s_sc.shape[1] // FLASH_SUB_ROWS):
        rows = slice(b * FLASH_SUB_ROWS, (b + 1) * FLASH_SUB_ROWS)
        tiles = _lane_tiles(s_sc[slot, rows, :n])
        m_prev = m_sc[rows]
        m_new = jnp.maximum(m_prev, jnp.max(functools.reduce(jnp.maximum, tiles), axis=-1, keepdims=True))
        alpha_sc[slot, rows] = jnp.exp2(m_prev - m_new)
        p_sc[slot, rows, :n] = jnp.concatenate([jnp.exp2(t - m_new).astype(BF16) for t in tiles], axis=1)
        m_sc[rows] = m_new


def _flash_accumulate(p_sc, alpha_sc, acc_sc, slot, vs):
    n = vs.shape[0]
    v_ext = jnp.concatenate([vs, jnp.ones_like(vs)], axis=1)
    alpha = alpha_sc[slot]
    acc_sc[...] = jnp.concatenate([alpha, alpha], axis=1) * acc_sc[...] + _dot(p_sc[slot, :, :n], v_ext)


def _flash_kernel(q_ref, k_ref, v_ref, o_ref, q_sc, s_sc, p_sc, alpha_sc, m_sc, acc_sc, *,
                  rep, tq, tk, n_main, n_tail):
    for r in range(rep):
        q_sc[r * tq:(r + 1) * tq, :] = q_ref[:, r * HEAD_DIM:(r + 1) * HEAD_DIM]
    m_sc[...] = jnp.full(m_sc.shape, -jnp.inf, F32)
    acc_sc[...] = jnp.zeros(acc_sc.shape, F32)
    n_pairs = n_main // (2 * tk)

    def body(c, carry):
        base = pl.multiple_of(c * 2 * tk, 2 * tk)
        nxt = pl.multiple_of(jnp.minimum(c + 1, n_pairs - 1) * 2 * tk, 2 * tk)
        _flash_scores(q_sc, k_ref[:, pl.ds(base + tk, tk)], s_sc, 1)
        _flash_softmax(s_sc, p_sc, alpha_sc, m_sc, 0, tk)
        _flash_accumulate(p_sc, alpha_sc, acc_sc, 0, v_ref[pl.ds(base, tk), :])
        _flash_scores(q_sc, k_ref[:, pl.ds(nxt, tk)], s_sc, 0)
        _flash_softmax(s_sc, p_sc, alpha_sc, m_sc, 1, tk)
        _flash_accumulate(p_sc, alpha_sc, acc_sc, 1, v_ref[pl.ds(base + tk, tk), :])
        return carry

    _flash_scores(q_sc, k_ref[:, pl.ds(0, tk)], s_sc, 0)
    lax.fori_loop(0, n_pairs, body, 0)
    if n_tail:
        _flash_scores(q_sc, k_ref[:, pl.ds(n_main, n_tail)], s_sc, 0)
        _flash_softmax(s_sc, p_sc, alpha_sc, m_sc, 0, n_tail)
        _flash_accumulate(p_sc, alpha_sc, acc_sc, 0, v_ref[pl.ds(n_main, n_tail), :])
    for r in range(rep):
        rows = slice(r * tq, (r + 1) * tq)
        o_ref[:, r * HEAD_DIM:(r + 1) * HEAD_DIM] = (
            acc_sc[rows, :HEAD_DIM] / acc_sc[rows, HEAD_DIM:]).astype(o_ref.dtype)


def _gqa_attention(q, k_t, qkv, v_col_block0, n_q):
    t = k_t.shape[1]
    rep = N_HEADS // N_KV_HEADS
    tq, tk = GQA_TQ, GQA_TK
    n_main = t // (2 * tk) * (2 * tk)
    n_tail = t - n_main
    assert n_q % tq == 0 and n_tail % LANES == 0 and n_tail <= tk
    w = rep * HEAD_DIM
    return pl.pallas_call(
        functools.partial(_flash_kernel, rep=rep, tq=tq, tk=tk, n_main=n_main, n_tail=n_tail),
        grid=(N_KV_HEADS, n_q // tq),
        in_specs=[
            pl.BlockSpec((tq, w), lambda g, i: (i, g)),
            pl.BlockSpec((HEAD_DIM, t), lambda g, i: (g, 0)),
            pl.BlockSpec((t, HEAD_DIM), lambda g, i: (0, v_col_block0 + g)),
        ],
        out_specs=pl.BlockSpec((tq, w), lambda g, i: (i, g)),
        out_shape=jax.ShapeDtypeStruct((n_q, N_HEADS * HEAD_DIM), BF16),
        scratch_shapes=[
            pltpu.VMEM((rep * tq, HEAD_DIM), BF16),
            pltpu.VMEM((2, rep * tq, tk), F32),
            pltpu.VMEM((2, rep * tq, tk), BF16),
            pltpu.VMEM((2, rep * tq, HEAD_DIM), F32),
            pltpu.VMEM((rep * tq, HEAD_DIM), F32),
            pltpu.VMEM((rep * tq, 2 * HEAD_DIM), F32),
        ],
        compiler_params=_params(2),
        name="gqa_flash_attention",
    )(q, k_t, qkv)


def _route(logits, router_bias, block_rows):
    n_tok, n_e = logits.shape
    scores = jax.nn.sigmoid(logits)
    biased = scores + router_bias.astype(F32)
    per_g = n_e // N_EXPERT_GROUPS
    grp = biased.reshape(n_tok, N_EXPERT_GROUPS, per_g)
    g_first, g_mask1 = _first_max(grp)
    g_second, _ = _first_max(jnp.where(g_mask1, -jnp.inf, grp))
    g_score = g_first + g_second
    g_sel = jnp.zeros(g_score.shape, bool)
    for _ in range(TOP_K_GROUPS):
        _, pick = _first_max(jnp.where(g_sel, -jnp.inf, g_score))
        g_sel = g_sel | pick
    e_mask = jnp.repeat(g_sel, per_g, axis=1)
    cand = jnp.where(e_mask, biased, -jnp.inf)
    picks = []
    for _ in range(TOP_K):
        _, pick = _first_max(cand)
        picks.append(pick)
        cand = jnp.where(pick, -jnp.inf, cand)
    onehot = jnp.stack(picks, axis=1)
    w = jnp.sum(jnp.where(onehot, scores[:, None, :], 0.0), axis=-1)
    w = w / jnp.sum(w, axis=-1, keepdims=True) * ROUTED_SCALE

    n_asg = n_tok * TOP_K
    oh = onehot.reshape(n_asg, n_e).astype(jnp.int32)
    csum = jnp.cumsum(oh, axis=0)
    counts = csum[-1]
    padded = (counts + block_rows - 1) // block_rows * block_rows
    pad_end = jnp.cumsum(padded)
    pad_start = pad_end - padded
    dest = jnp.sum(oh * (csum - 1 + pad_start[None, :]), axis=1).astype(jnp.int32)
    n_blocks = -(-(n_asg + n_e * (block_rows - 1)) // block_rows)
    blk_row0 = jnp.arange(n_blocks, dtype=jnp.int32) * block_rows
    block_e = jnp.minimum(jnp.sum((pad_end[None, :] <= blk_row0[:, None]).astype(jnp.int32), axis=1), n_e - 1)
    be_oh = (block_e[:, None] == jnp.arange(n_e, dtype=jnp.int32)[None, :]).astype(jnp.int32)
    seg_left = jnp.sum(be_oh * (counts + pad_start)[None, :], axis=1) - blk_row0
    n_valid = jnp.clip(seg_left, 0, block_rows).astype(jnp.int32)
    codes = _plan_rows(dest, n_blocks * block_rows, block_rows)
    return w, block_e.astype(jnp.int32), n_valid, codes


def _first_max(x):
    m = jnp.max(x, axis=-1, keepdims=True)
    idx = lax.broadcasted_iota(jnp.int32, x.shape, x.ndim - 1)
    first = jnp.min(jnp.where(x == m, idx, x.shape[-1]), axis=-1, keepdims=True)
    return m[..., 0], idx == first


CODE_SLOT_SHIFT = 16
EXPERT_K_CHUNK = 512


def _plan_kernel(dest_hbm, codes_ref, dest_smem, sem, *, chunk, block_rows):
    j = pl.program_id(0)
    n_rows = codes_ref.shape[0]
    log_rows = block_rows.bit_length() - 1
    log_k = TOP_K.bit_length() - 1

    @pl.when(j == 0)
    def _():
        def init(r, c):
            parity = lax.shift_right_logical(r, log_rows) & 1
            codes_ref[r] = ((r & (block_rows - 1)) + parity * block_rows) | (TOP_K << CODE_SLOT_SHIFT)
            return c

        lax.fori_loop(0, n_rows, init, 0, unroll=16)

    cp = pltpu.make_async_copy(dest_hbm.at[j], dest_smem, sem)
    cp.start()
    cp.wait()

    def place(a, c):
        g = j * chunk + a
        codes_ref[dest_smem[a]] = lax.shift_right_logical(g, log_k) | ((g & (TOP_K - 1)) << CODE_SLOT_SHIFT)
        return c

    lax.fori_loop(0, chunk, place, 0, unroll=16)


def _plan_rows(dest, n_rows, block_rows):
    n_asg = dest.shape[0]
    assert block_rows & (block_rows - 1) == 0 and TOP_K & (TOP_K - 1) == 0
    chunk = max(c for c in range(LANES, 8192 + 1, LANES) if n_asg % c == 0)
    n_chunks = n_asg // chunk
    return pl.pallas_call(
        functools.partial(_plan_kernel, chunk=chunk, block_rows=block_rows),
        grid=(n_chunks,),
        in_specs=[pl.BlockSpec(memory_space=pl.ANY)],
        out_specs=pl.BlockSpec(memory_space=pltpu.SMEM),
        out_shape=jax.ShapeDtypeStruct((n_rows,), jnp.int32),
        scratch_shapes=[pltpu.SMEM((chunk,), jnp.int32), pltpu.SemaphoreType.DMA(())],
        compiler_params=_params(1),
        name="plan_rows",
    )(dest.reshape(n_chunks, chunk))


def _swiglu_rows(x, wgu_ref, wd_ref):
    hgu = _dot(x, wgu_ref[...])
    de = hgu.shape[1] // 2
    gate = hgu[:, :de]
    hb = (gate * jax.nn.sigmoid(gate) * hgu[:, de:]).astype(BF16)
    return _dot(hb, wd_ref[...])


def _expert_kernel(be_ref, nv_ref, codes_ref, t_hbm, wg_ref, wu_ref, wd_ref, y_hbm,
                   xbuf, ybuf, sem_in, sem_out, *, n_tok, br):
    i = pl.program_id(0)
    last_block = pl.num_programs(0) - 2
    slot = i % 2
    active = nv_ref[i] > 0
    prev_active = (i > 0) & (nv_ref[jnp.maximum(i - 1, 0)] > 0)
    token_mask = (1 << CODE_SLOT_SHIFT) - 1
    half = xbuf.shape[2]
    n_chunks = 2 * half // EXPERT_K_CHUNK
    group = br // n_chunks

    def gather(block, buf_slot, lo, hi):
        for r in range(lo, hi):
            tok = codes_ref[block * br + r] & token_mask
            pltpu.make_async_copy(t_hbm.at[pl.ds(tok, 1)], xbuf.at[buf_slot, pl.ds(r, 1)],
                                  sem_in.at[buf_slot]).start(priority=r % 2)

    def gather_wait(buf_slot):
        pltpu.make_async_copy(t_hbm.at[pl.ds(0, br)], xbuf.at[buf_slot], sem_in.at[buf_slot]).wait()

    def scatter(block, buf_slot, lo, hi):
        for r in range(lo, hi):
            code = codes_ref[block * br + r]
            dst = lax.shift_right_logical(code, CODE_SLOT_SHIFT) * n_tok + (code & token_mask)
            pltpu.make_async_copy(ybuf.at[buf_slot, pl.ds(r, 1)], y_hbm.at[pl.ds(dst, 1)],
                                  sem_out.at[buf_slot]).start(priority=r % 2)

    def scatter_wait(buf_slot):
        pltpu.make_async_copy(ybuf.at[buf_slot], y_hbm.at[pl.ds(0, br)], sem_out.at[buf_slot]).wait()

    def compute(with_scatter):
        nxt = jnp.minimum(i + 1, last_block)
        gather_wait(slot)
        hgu = None
        for c in range(n_chunks):
            cols = slice(c * EXPERT_K_CHUNK, (c + 1) * EXPERT_K_CHUNK)
            lo = c * EXPERT_K_CHUNK % half
            words = xbuf[slot, :, lo:lo + EXPERT_K_CHUNK]
            xc = _unpack_low(words) if c * EXPERT_K_CHUNK < half else _unpack_high(words)
            w_chunk = jnp.concatenate([wg_ref[cols, :].astype(BF16), wu_ref[cols, :].astype(BF16)], axis=1)
            part = _dot(xc.astype(BF16), w_chunk)
            hgu = part if hgu is None else hgu + part
            gather(nxt, 1 - slot, c * group, (c + 1) * group)
            if with_scatter:
                scatter(i - 1, 1 - slot, c * group, (c + 1) * group)
        de = hgu.shape[1] // 2
        gate = hgu[:, :de]
        hb = (gate * jax.nn.sigmoid(gate) * hgu[:, de:]).astype(BF16)
        ybuf[slot] = _pack_halves(_dot(hb, wd_ref[...].astype(BF16)))
        if with_scatter:
            scatter_wait(1 - slot)

    @pl.when(i == 0)
    def _():
        gather(0, 0, 0, br)

    @pl.when(active & (i == 0))
    def _():
        compute(False)

    @pl.when(active & (i > 0))
    def _():
        compute(True)

    @pl.when(jnp.logical_not(active) & ((i == 0) | prev_active))
    def _():
        gather_wait(slot)

        @pl.when(prev_active)
        def _():
            scatter(i - 1, 1 - slot, 0, br)
            scatter_wait(1 - slot)


def _routed_experts(tok_packed, block_e, n_valid, codes, w_gate, w_up, w_down, layer):
    n_tok, half = tok_packed.shape
    d = 2 * half
    n_blocks = block_e.shape[0]
    br = codes.shape[0] // n_blocks
    de = w_gate.shape[3]
    assert half % EXPERT_K_CHUNK == 0 and br % (d // EXPERT_K_CHUNK) == 0
    block_e = jnp.concatenate([block_e, block_e[-1:]])
    n_valid = jnp.concatenate([n_valid, jnp.zeros((1,), jnp.int32)])
    grid_spec = pltpu.PrefetchScalarGridSpec(
        num_scalar_prefetch=3,
        grid=(n_blocks + 1,),
        in_specs=[
            pl.BlockSpec(memory_space=pl.ANY),
            pl.BlockSpec((None, None, d, de), lambda i, be, nv, cd: (layer, be[i], 0, 0)),
            pl.BlockSpec((None, None, d, de), lambda i, be, nv, cd: (layer, be[i], 0, 0)),
            pl.BlockSpec((None, None, de, d), lambda i, be, nv, cd: (layer, be[i], 0, 0)),
        ],
        out_specs=pl.BlockSpec(memory_space=pl.ANY),
        scratch_shapes=[
            pltpu.VMEM((2, br, half), jnp.uint32),
            pltpu.VMEM((2, br, half), jnp.uint32),
            pltpu.SemaphoreType.DMA((2,)),
            pltpu.SemaphoreType.DMA((2,)),
        ],
    )
    return pl.pallas_call(
        functools.partial(_expert_kernel, n_tok=n_tok, br=br),
        grid_spec=grid_spec,
        out_shape=jax.ShapeDtypeStruct((TOP_K * n_tok + 2 * br, half), jnp.uint32),
        compiler_params=_params(1),
        name="routed_experts",
    )(block_e, n_valid, codes, tok_packed, w_gate, w_up, w_down)


def _combine_kernel(*refs, with_h):
    (x_ref, tok_ref, sgu_ref, sd_ref, y0_ref, y1_ref, y2_ref, y3_ref, w_ref, gate_ref, g_ref,
     b_ref) = refs[:12]
    w = w_ref[...]
    y = _swiglu_rows(_unpack_halves(tok_ref[...]).astype(BF16), sgu_ref, sd_ref)
    for k, y_ref in enumerate((y0_ref, y1_ref, y2_ref, y3_ref)):
        y = y + w[:, k:k + 1] * _unpack_halves(y_ref[...])
    z = DEEPNORM_ALPHA * x_ref[...] + gate_ref[...] * y
    xn = _layer_norm_rows(z, g_ref[...], b_ref[...])
    if with_h:
        sc_ref, shf_ref, xo_ref, h_ref = refs[12:16]
        xo_ref[...] = xn
        h_ref[...] = (xn * (1.0 + sc_ref[...]) + shf_ref[...]).astype(h_ref.dtype)
    else:
        xo_ref = refs[12]
        xo_ref[...] = xn


def _moe_combine(x, tok_packed, s_gu, s_down, y_slots, w, mod, gate_slot, ln_g, ln_b, n_lat, *,
                 next_mod=None, n_out_rows=None):
    t, d = x.shape
    n_out_rows = t if n_out_rows is None else n_out_rows
    tm = COMBINE_ROWS
    nlb = n_lat // tm
    with_h = next_mod is not None
    y3 = y_slots
    de2 = s_gu.shape[1]
    row = pl.BlockSpec((tm, d), lambda i: (i, 0))
    packed_row = pl.BlockSpec((tm, d // 2), lambda i: (i, 0))
    vec = pl.BlockSpec((1, d), lambda i: (0, 0))

    def slot_spec(k):
        return pl.BlockSpec((tm, d // 2), lambda i: (k * (t // tm) + i, 0))

    in_specs = [
        row, packed_row,
        pl.BlockSpec((d, de2), lambda i: (0, 0)),
        pl.BlockSpec((de2 // 2, d), lambda i: (0, 0)),
    ] + [slot_spec(k) for k in range(TOP_K)] + [
        pl.BlockSpec((tm, TOP_K), lambda i: (i, 0)),
        _mod_spec(gate_slot, nlb, d), vec, vec,
    ]
    args = [x, tok_packed, s_gu, s_down, y3, y3, y3, y3, w, mod, ln_g.reshape(1, d), ln_b.reshape(1, d)]
    out_specs = [row]
    out_shape = [jax.ShapeDtypeStruct((n_out_rows, d), F32)]
    if with_h:
        in_specs += [_mod_spec(1, nlb, d), _mod_spec(0, nlb, d)]
        args += [next_mod, next_mod]
        out_specs.append(row)
        out_shape.append(jax.ShapeDtypeStruct((n_out_rows, d), BF16))
    return pl.pallas_call(
        functools.partial(_combine_kernel, with_h=with_h),
        grid=(n_out_rows // tm,),
        in_specs=in_specs,
        out_specs=out_specs,
        out_shape=out_shape,
        compiler_params=_params(1),
        name="moe_combine",
    )(*args)


def kernel(x, c, ctx, c_ctx, w_mod_down, w_mod_up, b_mod, ln_g, ln_b, gm_w_in, gm_v_g, gm_v_b, gm_w_s, gm_b_s, gm_w_out, na_w_qkv, na_rpb, na_w_out, ft_w_out, ga_w_qkv, ga_q_g, ga_k_g, ga_w_out, router_w, router_bias, exp_w_gate, exp_w_up, exp_w_down, sh_w_gate, sh_w_up, sh_w_down):
    b, n, d = x.shape
    nc = ctx.shape[1]
    assert b == 1 and c.shape[0] == 1
    n_tok = n + nc
    rows = n // GRID_W

    xs = jnp.concatenate([x[0], ctx[0]], axis=0)
    cond8 = jnp.zeros((8, d), F32).at[0].set(c[0]).at[1].set(c_ctx)
    mods = _ada_mod_all(cond8, w_mod_down, w_mod_up, b_mod)
    router_w_pad = jnp.pad(router_w, ((0, 0), (0, 0), (0, LANES - N_EXPERTS)))

    h = _modulate(xs, mods[0], 1, 0, n, BF16)
    for i in range(DEPTH):
        kind = i % 4
        mod = mods[i]
        last = i == DEPTH - 1

        if kind == 0:
            uv = _matmul(h, gm_w_in[0].astype(BF16), out_dtype=BF16, act="gelu")
            gated = _gmlp_gate(uv, gm_v_g[0], gm_v_b[0], gm_w_s[0], gm_b_s[0])
            o = _matmul(gated, gm_w_out[0].astype(BF16), out_dtype=F32)
        elif kind == 1:
            qkv = _matmul(h, na_w_qkv[0].astype(BF16), out_dtype=BF16)
            a_lat = _na_attention(qkv, _na_bias_table(na_rpb[0], rows), n, nc)
            a_ctx = _ctx_attention(qkv, n, nc)
            o = _matmul(jnp.concatenate([a_lat, a_ctx], axis=0), na_w_out[0].astype(BF16), out_dtype=F32)
        elif kind == 2:
            p = jnp.concatenate([_ft_positions_latent(h[:n]), _ft_positions_dense(h[n:])], axis=1)
            o = _matmul(_ft_channels(p), ft_w_out[0].astype(BF16), out_dtype=F32)
        else:
            qkv = _matmul(h, ga_w_qkv[0].astype(BF16), out_dtype=BF16, tn=1024)
            cos, sin = _rope_tables(n, nc)
            qn = _norm_rope(qkv, 0, N_HEADS, ga_q_g[0], cos, sin, HEAD_DIM ** -0.5 * math.log2(math.e))
            kn = _norm_rope(qkv, N_HEADS // N_KV_HEADS, N_KV_HEADS, ga_k_g[0], cos, sin, 1.0)
            att = _gqa_attention(qn, kn.T, qkv, (N_HEADS + N_KV_HEADS), n)
            att = jnp.concatenate([att, jnp.zeros((nc, d), BF16)], axis=0)
            o = _matmul(att, ga_w_out[0].astype(BF16), out_dtype=F32)

        xs, h2, logits = _resln(xs, o, mod, 2, ln_g[i, 0], ln_b[i, 0], n, next_mod=mod,
                                scale_slot=4, shift_slot=3, h_dtype=jnp.uint32, router_w=router_w_pad[i])

        w, block_e, n_valid, codes = _route(logits[:, :N_EXPERTS], router_bias[i], MOE_ROWS)
        y_slots = _routed_experts(h2, block_e, n_valid, codes, exp_w_gate, exp_w_up, exp_w_down, i)
        s_gu = jnp.concatenate([sh_w_gate[i], sh_w_up[i]], axis=-1).astype(BF16)
        s_down = sh_w_down[i].astype(BF16)
        if last:
            (xs,) = _moe_combine(xs, h2, s_gu, s_down, y_slots, w, mod, 5, ln_g[i, 1], ln_b[i, 1], n,
                                 n_out_rows=n)
        else:
            xs, h = _moe_combine(xs, h2, s_gu, s_down, y_slots, w, mod, 5, ln_g[i, 1], ln_b[i, 1], n,
                                 next_mod=mods[i + 1])
    return xs.reshape(b, n, d)
```

```python
import functools
import math

import jax
import jax.numpy as jnp
import numpy as np
from jax import lax
from jax.experimental import pallas as pl
from jax.experimental.pallas import tpu as pltpu

F32 = jnp.float32
BF16 = jnp.bfloat16

D_MODEL = 4096
DEPTH = 4
GRID_W = 64
N_MOD = 6
LN_EPS = 1e-5
RMS_EPS = 1e-6
DEEPNORM_ALPHA = (2 * DEPTH) ** 0.25
GM_CHUNK = 128
GM_GROUPS = 16
N_HEADS = 32
HEAD_DIM = 128
N_KV_HEADS = 8
ROPE_THETA = 10000.0
NA_WIN_H = 8
NA_WIN_W = 16
FT_GROUPS = 8
N_EXPERTS = 32
N_EXPERT_GROUPS = 4
TOP_K_GROUPS = 2
TOP_K = 4
D_EXPERT = 384
ROUTED_SCALE = 2.5

LANES = 128
VMEM_LIMIT_BYTES = 56 * 1024 * 1024
ROW_TILE = 256
MM_TILE_M = 1280
MM_TILE_N = 1024
MOE_ROWS = 256
COMBINE_ROWS = 128
NA_Q_ROWS = 4
NA_KEY_ROWS = 12
GQA_TQ = 256
GQA_TK = 2048
MASK_VALUE = -1e30


def _params(n_axes):
    return pltpu.CompilerParams(
        dimension_semantics=("arbitrary",) * n_axes,
        vmem_limit_bytes=VMEM_LIMIT_BYTES,
    )


def _dot(a, b, precision=None):
    return jnp.dot(a, b, preferred_element_type=F32, precision=precision)


def _dot_nt(a, b):
    return lax.dot_general(a, b, (((1,), (1,)), ((), ())), preferred_element_type=F32)


def _mm_kernel(a_ref, b_ref, o_ref, *, act):
    acc = _dot(a_ref[...], b_ref[...])
    if act == "gelu":
        acc = jax.nn.gelu(acc)
    o_ref[...] = acc.astype(o_ref.dtype)


def _matmul(a, b, *, out_dtype, act=None, tm=MM_TILE_M, tn=MM_TILE_N):
    m, k = a.shape
    n = b.shape[1]
    tm = min(tm, m)
    tn = min(tn, n)
    if jnp.dtype(out_dtype).itemsize == 4:
        tn = min(tn, MM_TILE_N // 2)
    while n % tn:
        tn //= 2
    assert m % tm == 0 and tn % LANES == 0
    return pl.pallas_call(
        functools.partial(_mm_kernel, act=act),
        grid=(m // tm, n // tn),
        in_specs=[
            pl.BlockSpec((tm, k), lambda i, j: (i, 0)),
            pl.BlockSpec((k, tn), lambda i, j: (0, j)),
        ],
        out_specs=pl.BlockSpec((tm, tn), lambda i, j: (i, j)),
        out_shape=jax.ShapeDtypeStruct((m, n), out_dtype),
        compiler_params=_params(2),
        name="matmul",
    )(a, b)


def _lmm_kernel(m_ref, x_ref, o_ref):
    o_ref[...] = _dot(m_ref[...], x_ref[...]).astype(o_ref.dtype)


def _left_matmul(mat, x, *, out_dtype, tn):
    r, k = mat.shape
    n = x.shape[1]
    assert n % tn == 0
    return pl.pallas_call(
        _lmm_kernel,
        grid=(n // tn,),
        in_specs=[
            pl.BlockSpec((r, k), lambda j: (0, 0)),
            pl.BlockSpec((k, tn), lambda j: (0, j)),
        ],
        out_specs=pl.BlockSpec((r, tn), lambda j: (0, j)),
        out_shape=jax.ShapeDtypeStruct((r, n), out_dtype),
        compiler_params=_params(1),
        name="left_matmul",
    )(mat, x)


def _mod_kernel(cond_ref, wd_ref, wu_ref, b_ref, o_ref):
    cond = cond_ref[...]
    t = _dot(cond * jax.nn.sigmoid(cond), wd_ref[...], precision=lax.Precision.HIGHEST)
    o_ref[...] = _dot(t, wu_ref[...], precision=lax.Precision.HIGHEST) + b_ref[...]


def _ada_mod_all(cond8, w_mod_down, w_mod_up, b_mod):
    depth, d, rank = w_mod_down.shape
    out = pl.pallas_call(
        _mod_kernel,
        grid=(depth, N_MOD),
        in_specs=[
            pl.BlockSpec((8, d), lambda l, j: (0, 0)),
            pl.BlockSpec((None, d, rank), lambda l, j: (l, 0, 0)),
            pl.BlockSpec((None, rank, d), lambda l, j: (l, 0, j)),
            pl.BlockSpec((None, None, 1, d), lambda l, j: (l, j, 0, 0)),
        ],
        out_specs=pl.BlockSpec((None, None, 8, d), lambda l, j: (l, j, 0, 0)),
        out_shape=jax.ShapeDtypeStruct((depth, N_MOD, 8, d), F32),
        compiler_params=_params(2),
        name="ada_mod",
    )(cond8, w_mod_down, w_mod_up, b_mod.reshape(depth, N_MOD, 1, d))
    return jnp.transpose(out[:, :, :2], (0, 2, 1, 3))[:, :, :, None, :]


def _mod_spec(slot, n_lat_blocks, d):
    return pl.BlockSpec((None, None, 1, d), lambda i: (i // n_lat_blocks, slot, 0, 0))


def _modulate_kernel(x_ref, sc_ref, sh_ref, h_ref):
    h_ref[...] = (x_ref[...] * (1.0 + sc_ref[...]) + sh_ref[...]).astype(h_ref.dtype)


def _modulate(x, mod, scale_slot, shift_slot, n_lat, out_dtype):
    t, d = x.shape
    tm = ROW_TILE
    nlb = n_lat // tm
    return pl.pallas_call(
        _modulate_kernel,
        grid=(t // tm,),
        in_specs=[
            pl.BlockSpec((tm, d), lambda i: (i, 0)),
            _mod_spec(scale_slot, nlb, d),
            _mod_spec(shift_slot, nlb, d),
        ],
        out_specs=pl.BlockSpec((tm, d), lambda i: (i, 0)),
        out_shape=jax.ShapeDtypeStruct((t, d), out_dtype),
        compiler_params=_params(1),
        name="modulate",
    )(x, mod, mod)


def _layer_norm_rows(z, g, b):
    mu = jnp.mean(z, axis=-1, keepdims=True)
    zc = z - mu
    var = jnp.mean(zc * zc, axis=-1, keepdims=True)
    return zc * lax.rsqrt(var + LN_EPS) * g + b


def _pack_halves(x):
    c = x.shape[1] // 2
    lo = lax.bitcast_convert_type(x[:, :c].astype(BF16).astype(F32), jnp.uint32)
    hi = lax.bitcast_convert_type(x[:, c:].astype(BF16).astype(F32), jnp.uint32)
    return hi | lax.shift_right_logical(lo, jnp.uint32(16))


def _unpack_low(u):
    return lax.bitcast_convert_type(lax.shift_left(u, jnp.uint32(16)), F32)


def _unpack_high(u):
    return lax.bitcast_convert_type(u & jnp.uint32(0xFFFF0000), F32)


def _unpack_halves(u):
    return jnp.concatenate([_unpack_low(u), _unpack_high(u)], axis=1)


def _resln_kernel(*refs, with_h, with_router):
    x_ref, o_ref, gate_ref, g_ref, b_ref = refs[:5]
    pos = 5
    if with_h:
        sc_ref, sh_ref = refs[pos:pos + 2]
        pos += 2
    if with_router:
        rw_ref = refs[pos]
        pos += 1
    xo_ref = refs[pos]
    pos += 1
    z = DEEPNORM_ALPHA * x_ref[...] + gate_ref[...] * o_ref[...].astype(F32)
    xn = _layer_norm_rows(z, g_ref[...], b_ref[...])
    xo_ref[...] = xn
    if with_h:
        h_ref = refs[pos]
        pos += 1
        h = xn * (1.0 + sc_ref[...]) + sh_ref[...]
        if h_ref.dtype == jnp.uint32:
            h_ref[...] = _pack_halves(h)
        else:
            h_ref[...] = h.astype(h_ref.dtype)
        if with_router:
            lg_ref = refs[pos]
            lg_ref[...] = _dot(h, rw_ref[...], precision=lax.Precision.HIGHEST)


def _resln(x, o, mod, gate_slot, ln_g, ln_b, n_lat, *, next_mod=None, scale_slot=None,
           shift_slot=None, h_dtype=None, router_w=None, n_out_rows=None):
    t, d = x.shape
    n_out_rows = t if n_out_rows is None else n_out_rows
    tm = ROW_TILE
    nlb = n_lat // tm
    with_h = next_mod is not None
    with_router = router_w is not None
    row = pl.BlockSpec((tm, d), lambda i: (i, 0))
    vec = pl.BlockSpec((1, d), lambda i: (0, 0))
    in_specs = [row, row, _mod_spec(gate_slot, nlb, d), vec, vec]
    args = [x, o, mod, ln_g.reshape(1, d), ln_b.reshape(1, d)]
    out_specs = [row]
    out_shape = [jax.ShapeDtypeStruct((n_out_rows, d), F32)]
    if with_h:
        in_specs += [_mod_spec(scale_slot, nlb, d), _mod_spec(shift_slot, nlb, d)]
        args += [next_mod, next_mod]
        h_cols = d // 2 if h_dtype == jnp.uint32 else d
        out_specs.append(pl.BlockSpec((tm, h_cols), lambda i: (i, 0)))
        out_shape.append(jax.ShapeDtypeStruct((n_out_rows, h_cols), h_dtype))
    if with_router:
        ne = router_w.shape[1]
        in_specs.append(pl.BlockSpec((d, ne), lambda i: (0, 0)))
        args.append(router_w)
        out_specs.append(pl.BlockSpec((tm, ne), lambda i: (i, 0)))
        out_shape.append(jax.ShapeDtypeStruct((n_out_rows, ne), F32))
    return pl.pallas_call(
        functools.partial(_resln_kernel, with_h=with_h, with_router=with_router),
        grid=(n_out_rows // tm,),
        in_specs=in_specs,
        out_specs=out_specs,
        out_shape=out_shape,
        compiler_params=_params(1),
        name="residual_layernorm",
    )(*args)


def _gmlp_gate_kernel(u_ref, v_ref, vg_ref, vb_ref, ws_ref, bs_ref, o_ref, *, n_groups, gw):
    vn = _layer_norm_rows(v_ref[...].astype(F32), vg_ref[...], vb_ref[...]).astype(BF16)
    for g in range(n_groups):
        cols = slice(g * gw, (g + 1) * gw)
        sv = _dot(ws_ref[g], vn[:, cols]) + bs_ref[:, cols]
        o_ref[:, cols] = (u_ref[:, cols].astype(F32) * sv).astype(o_ref.dtype)


def _gmlp_gate(uv, v_g, v_b, w_s, b_s):
    t, two_w = uv.shape
    width = two_w // 2
    n_groups, chunk, _ = w_s.shape
    gw = width // n_groups
    bias_full = jnp.repeat(b_s.T, gw, axis=1).astype(F32)
    return pl.pallas_call(
        functools.partial(_gmlp_gate_kernel, n_groups=n_groups, gw=gw),
        grid=(t // chunk,),
        in_specs=[
            pl.BlockSpec((chunk, width), lambda i: (i, 0)),
            pl.BlockSpec((chunk, width), lambda i: (i, 1)),
            pl.BlockSpec((1, width), lambda i: (0, 0)),
            pl.BlockSpec((1, width), lambda i: (0, 0)),
            pl.BlockSpec((n_groups, chunk, chunk), lambda i: (0, 0, 0)),
            pl.BlockSpec((chunk, width), lambda i: (0, 0)),
        ],
        out_specs=pl.BlockSpec((chunk, width), lambda i: (i, 0)),
        out_shape=jax.ShapeDtypeStruct((t, width), BF16),
        compiler_params=_params(1),
        name="gmlp_gate",
    )(uv, uv, v_g.reshape(1, width), v_b.reshape(1, width), w_s.astype(BF16), bias_full)


def _na_bias_table(rpb, rows):
    nb = rows // NA_Q_ROWS
    h = rpb.shape[0]
    n_dr = 2 * NA_WIN_H - 1
    c = np.arange(GRID_W)
    dc = c[None, :] - c[:, None] + NA_WIN_W - 1
    cs = np.clip(c - NA_WIN_W // 2, 0, GRID_W - NA_WIN_W)
    ok_c = (c[None, :] >= cs[:, None]) & (c[None, :] < cs[:, None] + NA_WIN_W)
    sel = ((np.arange(2 * NA_WIN_W - 1)[:, None, None] == dc[None]) & ok_c[None]).astype(np.float32)
    planes = jnp.einsum("hrd,dqk->hrqk", rpb.astype(F32), jnp.asarray(sel), precision=lax.Precision.HIGHEST)
    planes = jnp.where(ok_c[None, None], planes, MASK_VALUE)
    planes = jnp.concatenate([planes, jnp.full((h, 1, GRID_W, GRID_W), MASK_VALUE, F32)], axis=1)
    tables = []
    for b in (0, 1, nb - 1):
        r0 = b * NA_Q_ROWS
        kw0 = int(np.clip(r0 - NA_WIN_H // 2, 0, rows - NA_KEY_ROWS))
        r = r0 + np.arange(NA_Q_ROWS)[:, None]
        kr = kw0 + np.arange(NA_KEY_ROWS)[None, :]
        rs = np.clip(r - NA_WIN_H // 2, 0, rows - NA_WIN_H)
        ok_r = (kr >= rs) & (kr < rs + NA_WIN_H)
        dr = np.where(ok_r, kr - r + NA_WIN_H - 1, n_dr)
        tbl = jnp.stack([planes[:, int(d)] for d in dr.reshape(-1)], axis=1)
        tbl = tbl.reshape(h, NA_Q_ROWS, NA_KEY_ROWS, GRID_W, GRID_W).transpose(0, 1, 3, 2, 4)
        tables.append(tbl.reshape(h, NA_Q_ROWS * GRID_W, NA_KEY_ROWS * GRID_W))
    return jnp.stack(tables, axis=0)


def _na_kernel(q_ref, k_ref, v_ref, bias_ref, o_ref, *, rows, n_lat, n_ctx):
    b = pl.program_id(1)
    kw0 = jnp.clip(b * NA_Q_ROWS - NA_WIN_H // 2, 0, rows - NA_KEY_ROWS)
    start = pl.multiple_of(kw0 * GRID_W, GRID_W)
    nk = NA_KEY_ROWS * GRID_W
    scale = HEAD_DIM ** -0.5
    q = q_ref[...]
    kw = k_ref[pl.ds(start, nk), :]
    vw = v_ref[pl.ds(start, nk), :]
    kc = k_ref[pl.ds(n_lat, n_ctx), :]
    vc = v_ref[pl.ds(n_lat, n_ctx), :]
    s_loc = _dot_nt(q, kw) * scale + bias_ref[...]
    s_ctx = _dot_nt(q, kc) * scale
    m = jnp.maximum(jnp.max(s_loc, axis=-1, keepdims=True), jnp.max(s_ctx, axis=-1, keepdims=True))
    p_loc = jnp.exp(s_loc - m)
    p_ctx = jnp.exp(s_ctx - m)
    l = jnp.sum(p_loc, axis=-1, keepdims=True) + jnp.sum(p_ctx, axis=-1, keepdims=True)
    o = _dot(p_loc.astype(BF16), vw) + _dot(p_ctx.astype(BF16), vc)
    o_ref[...] = (o / l).astype(o_ref.dtype)


def _na_attention(qkv, bias_table, n_lat, n_ctx):
    t = qkv.shape[0]
    rows = n_lat // GRID_W
    nb = rows // NA_Q_ROWS
    tq = NA_Q_ROWS * GRID_W
    nk = NA_KEY_ROWS * GRID_W
    h = N_HEADS
    return pl.pallas_call(
        functools.partial(_na_kernel, rows=rows, n_lat=n_lat, n_ctx=n_ctx),
        grid=(h, nb),
        in_specs=[
            pl.BlockSpec((tq, HEAD_DIM), lambda hh, b: (b, hh)),
            pl.BlockSpec((t, HEAD_DIM), lambda hh, b: (0, h + hh)),
            pl.BlockSpec((t, HEAD_DIM), lambda hh, b: (0, 2 * h + hh)),
            pl.BlockSpec((None, None, tq, nk),
                         lambda hh, b: (jnp.where(b == 0, 0, jnp.where(b == nb - 1, 2, 1)), hh, 0, 0)),
        ],
        out_specs=pl.BlockSpec((tq, HEAD_DIM), lambda hh, b: (b, hh)),
        out_shape=jax.ShapeDtypeStruct((n_lat, h * HEAD_DIM), BF16),
        compiler_params=_params(2),
        name="neighbourhood_attention",
    )(qkv, qkv, qkv, bias_table)


def _ctx_attn_kernel(q_ref, k_ref, v_ref, o_ref):
    s = _dot_nt(q_ref[...], k_ref[...]) * (HEAD_DIM ** -0.5)
    m = jnp.max(s, axis=-1, keepdims=True)
    p = jnp.exp(s - m)
    l = jnp.sum(p, axis=-1, keepdims=True)
    o_ref[...] = (_dot(p.astype(BF16), v_ref[...]) / l).astype(o_ref.dtype)


def _ctx_attention(qkv, n_lat, n_ctx):
    h = N_HEADS
    rb = n_lat // n_ctx
    return pl.pallas_call(
        _ctx_attn_kernel,
        grid=(h,),
        in_specs=[
            pl.BlockSpec((n_ctx, HEAD_DIM), lambda hh: (rb, hh)),
            pl.BlockSpec((n_ctx, HEAD_DIM), lambda hh: (rb, h + hh)),
            pl.BlockSpec((n_ctx, HEAD_DIM), lambda hh: (rb, 2 * h + hh)),
        ],
        out_specs=pl.BlockSpec((n_ctx, HEAD_DIM), lambda hh: (0, hh)),
        out_shape=jax.ShapeDtypeStruct((n_ctx, h * HEAD_DIM), BF16),
        compiler_params=_params(1),
        name="context_attention",
    )(qkv, qkv, qkv)


def _dft_cos_sin(k_times_n, period):
    ang = (2.0 * math.pi / period) * (k_times_n % period).astype(F32)
    return jnp.cos(ang), -jnp.sin(ang)


def _ft_stage2_kernel(g_ref, y_ref, o_ref, *, nb):
    for i in range(nb):
        rhs = jnp.concatenate([y_ref[0, i], y_ref[1, i]], axis=0)
        res = _dot(g_ref[i], rhs)
        half = res.shape[0] // 2
        o_ref[0, :, i, :] = res[:half].astype(o_ref.dtype)
        o_ref[1, :, i, :] = res[half:].astype(o_ref.dtype)


def _ft_positions_latent(h_lat):
    n, d = h_lat.shape
    l1 = l2 = int(round(math.sqrt(n)))
    assert l1 * l2 == n
    idx = jnp.arange(l1, dtype=jnp.int32)
    re, im = _dft_cos_sin(idx[:, None] * idx[None, :], l1)
    m1 = (jnp.concatenate([re, im], axis=0) * (l1 ** -0.5)).astype(BF16)
    y = _left_matmul(m1, h_lat.reshape(l1, l2 * d), out_dtype=BF16, tn=8192)
    y = y.reshape(2, l1, l2, d)
    k1 = idx[:, None, None]
    k2 = idx[None, :, None]
    n2 = idx[None, None, :]
    gr, gi = _dft_cos_sin(n2 * k2 * l1 + n2 * k1, n)
    g = jnp.concatenate(
        [jnp.concatenate([gr, -gi], axis=2), jnp.concatenate([gi, gr], axis=2)], axis=1
    )
    g = (g * (l2 ** -0.5)).astype(BF16)
    nb, cb = 8, min(1024, d)
    p = pl.pallas_call(
        functools.partial(_ft_stage2_kernel, nb=nb),
        grid=(l1 // nb, d // cb),
        in_specs=[
            pl.BlockSpec((nb, 2 * l2, 2 * l2), lambda a, c: (a, 0, 0)),
            pl.BlockSpec((2, nb, l2, cb), lambda a, c: (0, a, 0, c)),
        ],
        out_specs=pl.BlockSpec((2, l2, nb, cb), lambda a, c: (0, 0, a, c)),
        out_shape=jax.ShapeDtypeStruct((2, l2, l1, d), BF16),
        compiler_params=_params(2),
        name="ft_stage2",
    )(g, y)
    return p.reshape(2, n, d)


def _ft_positions_dense(h_ctx):
    n, d = h_ctx.shape
    idx = jnp.arange(n, dtype=jnp.int32)
    re, im = _dft_cos_sin(idx[:, None] * idx[None, :], n)
    m = (jnp.concatenate([re, im], axis=0) * (n ** -0.5)).astype(BF16)
    return _left_matmul(m, h_ctx, out_dtype=BF16, tn=min(d, 2048)).reshape(2, n, d)


def _ft_channel_kernel(pr_ref, pi_ref, c_ref, s_ref, o_ref):
    o_ref[...] = (_dot(pr_ref[...], c_ref[...]) + _dot(pi_ref[...], s_ref[...])).astype(o_ref.dtype)


def _ft_channels(p):
    _, t, d = p.shape
    gc = d // FT_GROUPS
    idx = jnp.arange(gc, dtype=jnp.int32)
    re, im = _dft_cos_sin(idx[:, None] * idx[None, :], gc)
    cc = (re * (gc ** -0.5)).astype(BF16)
    sc = (-im * (gc ** -0.5)).astype(BF16)
    tm = MM_TILE_M if t % MM_TILE_M == 0 else t
    return pl.pallas_call(
        _ft_channel_kernel,
        grid=(t // tm, FT_GROUPS),
        in_specs=[
            pl.BlockSpec((None, tm, gc), lambda i, g: (0, i, g)),
            pl.BlockSpec((None, tm, gc), lambda i, g: (1, i, g)),
            pl.BlockSpec((gc, gc), lambda i, g: (0, 0)),
            pl.BlockSpec((gc, gc), lambda i, g: (0, 0)),
        ],
        out_specs=pl.BlockSpec((tm, gc), lambda i, g: (i, g)),
        out_shape=jax.ShapeDtypeStruct((t, d), BF16),
        compiler_params=_params(2),
        name="ft_channels",
    )(p, p, cc, sc)


def _rope_tables(n_lat, n_ctx):
    half = HEAD_DIM // 2
    t = jnp.arange(n_lat)
    inv = ROPE_THETA ** (-jnp.arange(0, half, 2, dtype=F32) / half)
    ang_r = (t // GRID_W).astype(F32)[:, None] * inv[None, :]
    ang_c = (t % GRID_W).astype(F32)[:, None] * inv[None, :]
    cos = jnp.concatenate([jnp.cos(ang_r)] * 2 + [jnp.cos(ang_c)] * 2, axis=-1)
    sin = jnp.concatenate([-jnp.sin(ang_r), jnp.sin(ang_r), -jnp.sin(ang_c), jnp.sin(ang_c)], axis=-1)
    cos = jnp.concatenate([cos, jnp.ones((n_ctx, HEAD_DIM), F32)], axis=0)
    sin = jnp.concatenate([sin, jnp.zeros((n_ctx, HEAD_DIM), F32)], axis=0)
    return cos, sin


def _norm_rope_kernel(x_ref, g_ref, cos_ref, sin_ref, o_ref, *, n_heads, scale):
    quarter = HEAD_DIM // 4
    lane = lax.broadcasted_iota(jnp.int32, (x_ref.shape[0], HEAD_DIM), 1)
    first = (lane % (2 * quarter)) < quarter
    cos = cos_ref[...]
    sin = sin_ref[...]
    g = g_ref[...]
    for hh in range(n_heads):
        cols = slice(hh * HEAD_DIM, (hh + 1) * HEAD_DIM)
        x = x_ref[:, cols].astype(F32)
        xn = x * lax.rsqrt(jnp.mean(x * x, axis=-1, keepdims=True) + RMS_EPS) * g
        partner = jnp.where(first, pltpu.roll(xn, HEAD_DIM - quarter, 1), pltpu.roll(xn, quarter, 1))
        o_ref[:, cols] = ((xn * cos + partner * sin) * scale).astype(o_ref.dtype)


def _norm_rope(qkv, col_block, n_heads, gain, cos, sin, scale):
    t = qkv.shape[0]
    tm = ROW_TILE
    w = n_heads * HEAD_DIM
    return pl.pallas_call(
        functools.partial(_norm_rope_kernel, n_heads=n_heads, scale=scale),
        grid=(t // tm,),
        in_specs=[
            pl.BlockSpec((tm, w), lambda i: (i, col_block)),
            pl.BlockSpec((1, HEAD_DIM), lambda i: (0, 0)),
            pl.BlockSpec((tm, HEAD_DIM), lambda i: (i, 0)),
            pl.BlockSpec((tm, HEAD_DIM), lambda i: (i, 0)),
        ],
        out_specs=pl.BlockSpec((tm, w), lambda i: (i, 0)),
        out_shape=jax.ShapeDtypeStruct((t, w), BF16),
        compiler_params=_params(1),
        name="norm_rope",
    )(qkv, gain.reshape(1, HEAD_DIM), cos, sin)


def _lane_tiles(x):
    return [x[:, j * LANES:(j + 1) * LANES] for j in range(x.shape[1] // LANES)]


FLASH_SUB_ROWS = 64


def _flash_scores(q_sc, kt, s_sc, slot):
    s_sc[slot, :, :kt.shape[1]] = _dot(q_sc[...], kt)


def _flash_softmax(s_sc, p_sc, alpha_sc, m_sc, slot, n):
    for b in range(s_sc.shape[1] // FLASH_SUB_ROWS):
        rows = slice(b * FLASH_SUB_ROWS, (b + 1) * FLASH_SUB_ROWS)
        tiles = _lane_tiles(s_sc[slot, rows, :n])
        m_prev = m_sc[rows]
        m_new = jnp.maximum(m_prev, jnp.max(functools.reduce(jnp.maximum, tiles), axis=-1, keepdims=True))
        alpha_sc[slot, rows] = jnp.exp2(m_prev - m_new)
        p_sc[slot, rows, :n] = jnp.concatenate([jnp.exp2(t - m_new).astype(BF16) for t in tiles], axis=1)
        m_sc[rows] = m_new


def _flash_accumulate(p_sc, alpha_sc, acc_sc, slot, vs):
    n = vs.shape[0]
    v_ext = jnp.concatenate([vs, jnp.ones_like(vs)], axis=1)
    alpha = alpha_sc[slot]
    acc_sc[...] = jnp.concatenate([alpha, alpha], axis=1) * acc_sc[...] + _dot(p_sc[slot, :, :n], v_ext)


def _flash_kernel(q_ref, k_ref, v_ref, o_ref, q_sc, s_sc, p_sc, alpha_sc, m_sc, acc_sc, *,
                  rep, tq, tk, n_main, n_tail):
    for r in range(rep):
        q_sc[r * tq:(r + 1) * tq, :] = q_ref[:, r * HEAD_DIM:(r + 1) * HEAD_DIM]
    m_sc[...] = jnp.full(m_sc.shape, -jnp.inf, F32)
    acc_sc[...] = jnp.zeros(acc_sc.shape, F32)
    n_pairs = n_main // (2 * tk)

    def body(c, carry):
        base = pl.multiple_of(c * 2 * tk, 2 * tk)
        nxt = pl.multiple_of(jnp.minimum(c + 1, n_pairs - 1) * 2 * tk, 2 * tk)
        _flash_scores(q_sc, k_ref[:, pl.ds(base + tk, tk)], s_sc, 1)
        _flash_softmax(s_sc, p_sc, alpha_sc, m_sc, 0, tk)
        _flash_accumulate(p_sc, alpha_sc, acc_sc, 0, v_ref[pl.ds(base, tk), :])
        _flash_scores(q_sc, k_ref[:, pl.ds(nxt, tk)], s_sc, 0)
        _flash_softmax(s_sc, p_sc, alpha_sc, m_sc, 1, tk)
        _flash_accumulate(p_sc, alpha_sc, acc_sc, 1, v_ref[pl.ds(base + tk, tk), :])
        return carry

    _flash_scores(q_sc, k_ref[:, pl.ds(0, tk)], s_sc, 0)
    lax.fori_loop(0, n_pairs, body, 0)
    if n_tail:
        _flash_scores(q_sc, k_ref[:, pl.ds(n_main, n_tail)], s_sc, 0)
        _flash_softmax(s_sc, p_sc, alpha_sc, m_sc, 0, n_tail)
        _flash_accumulate(p_sc, alpha_sc, acc_sc, 0, v_ref[pl.ds(n_main, n_tail), :])
    for r in range(rep):
        rows = slice(r * tq, (r + 1) * tq)
        o_ref[:, r * HEAD_DIM:(r + 1) * HEAD_DIM] = (
            acc_sc[rows, :HEAD_DIM] / acc_sc[rows, HEAD_DIM:]).astype(o_ref.dtype)


def _gqa_attention(q, k_t, qkv, v_col_block0, n_q):
    t = k_t.shape[1]
    rep = N_HEADS // N_KV_HEADS
    tq, tk = GQA_TQ, GQA_TK
    n_main = t // (2 * tk) * (2 * tk)
    n_tail = t - n_main
    assert n_q % tq == 0 and n_tail % LANES == 0 and n_tail <= tk
    w = rep * HEAD_DIM
    return pl.pallas_call(
        functools.partial(_flash_kernel, rep=rep, tq=tq, tk=tk, n_main=n_main, n_tail=n_tail),
        grid=(N_KV_HEADS, n_q // tq),
        in_specs=[
            pl.BlockSpec((tq, w), lambda g, i: (i, g)),
            pl.BlockSpec((HEAD_DIM, t), lambda g, i: (g, 0)),
            pl.BlockSpec((t, HEAD_DIM), lambda g, i: (0, v_col_block0 + g)),
        ],
        out_specs=pl.BlockSpec((tq, w), lambda g, i: (i, g)),
        out_shape=jax.ShapeDtypeStruct((n_q, N_HEADS * HEAD_DIM), BF16),
        scratch_shapes=[
            pltpu.VMEM((rep * tq, HEAD_DIM), BF16),
            pltpu.VMEM((2, rep * tq, tk), F32),
            pltpu.VMEM((2, rep * tq, tk), BF16),
            pltpu.VMEM((2, rep * tq, HEAD_DIM), F32),
            pltpu.VMEM((rep * tq, HEAD_DIM), F32),
            pltpu.VMEM((rep * tq, 2 * HEAD_DIM), F32),
        ],
        compiler_params=_params(2),
        name="gqa_flash_attention",
    )(q, k_t, qkv)


def _route(logits, router_bias, block_rows):
    n_tok, n_e = logits.shape
    scores = jax.nn.sigmoid(logits)
    biased = scores + router_bias.astype(F32)
    per_g = n_e // N_EXPERT_GROUPS
    grp = biased.reshape(n_tok, N_EXPERT_GROUPS, per_g)
    g_first, g_mask1 = _first_max(grp)
    g_second, _ = _first_max(jnp.where(g_mask1, -jnp.inf, grp))
    g_score = g_first + g_second
    g_sel = jnp.zeros(g_score.shape, bool)
    for _ in range(TOP_K_GROUPS):
        _, pick = _first_max(jnp.where(g_sel, -jnp.inf, g_score))
        g_sel = g_sel | pick
    e_mask = jnp.repeat(g_sel, per_g, axis=1)
    cand = jnp.where(e_mask, biased, -jnp.inf)
    picks = []
    for _ in range(TOP_K):
        _, pick = _first_max(cand)
        picks.append(pick)
        cand = jnp.where(pick, -jnp.inf, cand)
    onehot = jnp.stack(picks, axis=1)
    w = jnp.sum(jnp.where(onehot, scores[:, None, :], 0.0), axis=-1)
    w = w / jnp.sum(w, axis=-1, keepdims=True) * ROUTED_SCALE

    n_asg = n_tok * TOP_K
    oh = onehot.reshape(n_asg, n_e).astype(jnp.int32)
    csum = jnp.cumsum(oh, axis=0)
    counts = csum[-1]
    padded = (counts + block_rows - 1) // block_rows * block_rows
    pad_end = jnp.cumsum(padded)
    pad_start = pad_end - padded
    dest = jnp.sum(oh * (csum - 1 + pad_start[None, :]), axis=1).astype(jnp.int32)
    n_blocks = -(-(n_asg + n_e * (block_rows - 1)) // block_rows)
    blk_row0 = jnp.arange(n_blocks, dtype=jnp.int32) * block_rows
    block_e = jnp.minimum(jnp.sum((pad_end[None, :] <= blk_row0[:, None]).astype(jnp.int32), axis=1), n_e - 1)
    be_oh = (block_e[:, None] == jnp.arange(n_e, dtype=jnp.int32)[None, :]).astype(jnp.int32)
    seg_left = jnp.sum(be_oh * (counts + pad_start)[None, :], axis=1) - blk_row0
    n_valid = jnp.clip(seg_left, 0, block_rows).astype(jnp.int32)
    codes = _plan_rows(dest, n_blocks * block_rows, block_rows)
    return w, block_e.astype(jnp.int32), n_valid, codes


def _first_max(x):
    m = jnp.max(x, axis=-1, keepdims=True)
    idx = lax.broadcasted_iota(jnp.int32, x.shape, x.ndim - 1)
    first = jnp.min(jnp.where(x == m, idx, x.shape[-1]), axis=-1, keepdims=True)
    return m[..., 0], idx == first


CODE_SLOT_SHIFT = 16
EXPERT_K_CHUNK = 512


def _plan_kernel(dest_hbm, pad_hbm, codes_ref, dest_smem, sem, *, chunk):
    j = pl.program_id(0)
    log_k = TOP_K.bit_length() - 1

    @pl.when(j == 0)
    def _():
        init = pltpu.make_async_copy(pad_hbm, codes_ref, sem)
        init.start()
        init.wait()

    cp = pltpu.make_async_copy(dest_hbm.at[j], dest_smem, sem)
    cp.start()
    cp.wait()

    def place(a, c):
        g = j * chunk + a
        codes_ref[dest_smem[a]] = lax.shift_right_logical(g, log_k) | ((g & (TOP_K - 1)) << CODE_SLOT_SHIFT)
        return c

    lax.fori_loop(0, chunk, place, 0, unroll=16)


def _plan_rows(dest, n_rows, block_rows):
    n_asg = dest.shape[0]
    assert block_rows & (block_rows - 1) == 0 and TOP_K & (TOP_K - 1) == 0
    chunk = max(c for c in range(LANES, 8192 + 1, LANES) if n_asg % c == 0)
    n_chunks = n_asg // chunk
    r = jnp.arange(n_rows, dtype=jnp.int32)
    pad_codes = ((r % block_rows) + ((r // block_rows) % 2) * block_rows) | (TOP_K << CODE_SLOT_SHIFT)
    return pl.pallas_call(
        functools.partial(_plan_kernel, chunk=chunk),
        grid=(n_chunks,),
        in_specs=[pl.BlockSpec(memory_space=pl.ANY), pl.BlockSpec(memory_space=pl.ANY)],
        out_specs=pl.BlockSpec(memory_space=pltpu.SMEM),
        out_shape=jax.ShapeDtypeStruct((n_rows,), jnp.int32),
        scratch_shapes=[pltpu.SMEM((chunk,), jnp.int32), pltpu.SemaphoreType.DMA(())],
        compiler_params=_params(1),
        name="plan_rows",
    )(dest.reshape(n_chunks, chunk), pad_codes)


def _swiglu_rows(x, wgu_ref, wd_ref):
    hgu = _dot(x, wgu_ref[...])
    de = hgu.shape[1] // 2
    gate = hgu[:, :de]
    hb = (gate * jax.nn.sigmoid(gate) * hgu[:, de:]).astype(BF16)
    return _dot(hb, wd_ref[...])


def _expert_kernel(be_ref, nv_ref, codes_ref, t_hbm, wg_ref, wu_ref, wd_ref, y_hbm,
                   xbuf, ybuf, sem_in, sem_out, *, n_tok, br):
    i = pl.program_id(0)
    last_block = pl.num_programs(0) - 2
    slot = i % 2
    active = nv_ref[i] > 0
    prev_active = (i > 0) & (nv_ref[jnp.maximum(i - 1, 0)] > 0)
    token_mask = (1 << CODE_SLOT_SHIFT) - 1
    half = xbuf.shape[2]
    n_chunks = 2 * half // EXPERT_K_CHUNK
    group = br // n_chunks

    def gather(block, buf_slot, lo, hi):
        for r in range(lo, hi):
            tok = codes_ref[block * br + r] & token_mask
            pltpu.make_async_copy(t_hbm.at[pl.ds(tok, 1)], xbuf.at[buf_slot, pl.ds(r, 1)],
                                  sem_in.at[buf_slot]).start(priority=r % 2)

    def gather_wait(buf_slot):
        pltpu.make_async_copy(t_hbm.at[pl.ds(0, br)], xbuf.at[buf_slot], sem_in.at[buf_slot]).wait()

    def scatter(block, buf_slot, lo, hi):
        for r in range(lo, hi):
            code = codes_ref[block * br + r]
            dst = lax.shift_right_logical(code, CODE_SLOT_SHIFT) * n_tok + (code & token_mask)
            pltpu.make_async_copy(ybuf.at[buf_slot, pl.ds(r, 1)], y_hbm.at[pl.ds(dst, 1)],
                                  sem_out.at[buf_slot]).start(priority=r % 2)

    def scatter_wait(buf_slot):
        pltpu.make_async_copy(ybuf.at[buf_slot], y_hbm.at[pl.ds(0, br)], sem_out.at[buf_slot]).wait()

    def compute(with_scatter):
        nxt = jnp.minimum(i + 1, last_block)
        gather_wait(slot)
        hgu = None
        for c in range(n_chunks):
            cols = slice(c * EXPERT_K_CHUNK, (c + 1) * EXPERT_K_CHUNK)
            lo = c * EXPERT_K_CHUNK % half
            words = xbuf[slot, :, lo:lo + EXPERT_K_CHUNK]
            xc = _unpack_low(words) if c * EXPERT_K_CHUNK < half else _unpack_high(words)
            w_chunk = jnp.concatenate([wg_ref[cols, :].astype(BF16), wu_ref[cols, :].astype(BF16)], axis=1)
            part = _dot(xc.astype(BF16), w_chunk)
            hgu = part if hgu is None else hgu + part
            gather(nxt, 1 - slot, c * group, (c + 1) * group)
            if with_scatter:
                scatter(i - 1, 1 - slot, c * group, (c + 1) * group)
        de = hgu.shape[1] // 2
        gate = hgu[:, :de]
        hb = (gate * jax.nn.sigmoid(gate) * hgu[:, de:]).astype(BF16)
        ybuf[slot] = _pack_halves(_dot(hb, wd_ref[...].astype(BF16)))
        if with_scatter:
            scatter_wait(1 - slot)

    @pl.when(i == 0)
    def _():
        gather(0, 0, 0, br)

    @pl.when(active & (i == 0))
    def _():
        compute(False)

    @pl.when(active & (i > 0))
    def _():
        compute(True)

    @pl.when(jnp.logical_not(active) & ((i == 0) | prev_active))
    def _():
        gather_wait(slot)

        @pl.when(prev_active)
        def _():
            scatter(i - 1, 1 - slot, 0, br)
            scatter_wait(1 - slot)


def _routed_experts(tok_packed, block_e, n_valid, codes, w_gate, w_up, w_down, layer):
    n_tok, half = tok_packed.shape
    d = 2 * half
    n_blocks = block_e.shape[0]
    br = codes.shape[0] // n_blocks
    de = w_gate.shape[3]
    assert half % EXPERT_K_CHUNK == 0 and br % (d // EXPERT_K_CHUNK) == 0
    block_e = jnp.concatenate([block_e, block_e[-1:]])
    n_valid = jnp.concatenate([n_valid, jnp.zeros((1,), jnp.int32)])
    grid_spec = pltpu.PrefetchScalarGridSpec(
        num_scalar_prefetch=3,
        grid=(n_blocks + 1,),
        in_specs=[
            pl.BlockSpec(memory_space=pl.ANY),
            pl.BlockSpec((None, None, d, de), lambda i, be, nv, cd: (layer, be[i], 0, 0)),
            pl.BlockSpec((None, None, d, de), lambda i, be, nv, cd: (layer, be[i], 0, 0)),
            pl.BlockSpec((None, None, de, d), lambda i, be, nv, cd: (layer, be[i], 0, 0)),
        ],
        out_specs=pl.BlockSpec(memory_space=pl.ANY),
        scratch_shapes=[
            pltpu.VMEM((2, br, half), jnp.uint32),
            pltpu.VMEM((2, br, half), jnp.uint32),
            pltpu.SemaphoreType.DMA((2,)),
            pltpu.SemaphoreType.DMA((2,)),
        ],
    )
    return pl.pallas_call(
        functools.partial(_expert_kernel, n_tok=n_tok, br=br),
        grid_spec=grid_spec,
        out_shape=jax.ShapeDtypeStruct((TOP_K * n_tok + 2 * br, half), jnp.uint32),
        compiler_params=_params(1),
        name="routed_experts",
    )(block_e, n_valid, codes, tok_packed, w_gate, w_up, w_down)


def _combine_kernel(*refs, with_h):
    (x_ref, tok_ref, sgu_ref, sd_ref, y0_ref, y1_ref, y2_ref, y3_ref, w_ref, gate_ref, g_ref,
     b_ref) = refs[:12]
    w = w_ref[...]
    y = _swiglu_rows(_unpack_halves(tok_ref[...]).astype(BF16), sgu_ref, sd_ref)
    for k, y_ref in enumerate((y0_ref, y1_ref, y2_ref, y3_ref)):
        y = y + w[:, k:k + 1] * _unpack_halves(y_ref[...])
    z = DEEPNORM_ALPHA * x_ref[...] + gate_ref[...] * y
    xn = _layer_norm_rows(z, g_ref[...], b_ref[...])
    if with_h:
        sc_ref, shf_ref, xo_ref, h_ref = refs[12:16]
        xo_ref[...] = xn
        h_ref[...] = (xn * (1.0 + sc_ref[...]) + shf_ref[...]).astype(h_ref.dtype)
    else:
        xo_ref = refs[12]
        xo_ref[...] = xn


def _moe_combine(x, tok_packed, s_gu, s_down, y_slots, w, mod, gate_slot, ln_g, ln_b, n_lat, *,
                 next_mod=None, n_out_rows=None):
    t, d = x.shape
    n_out_rows = t if n_out_rows is None else n_out_rows
    tm = COMBINE_ROWS
    nlb = n_lat // tm
    with_h = next_mod is not None
    y3 = y_slots
    de2 = s_gu.shape[1]
    row = pl.BlockSpec((tm, d), lambda i: (i, 0))
    packed_row = pl.BlockSpec((tm, d // 2), lambda i: (i, 0))
    vec = pl.BlockSpec((1, d), lambda i: (0, 0))

    def slot_spec(k):
        return pl.BlockSpec((tm, d // 2), lambda i: (k * (t // tm) + i, 0))

    in_specs = [
        row, packed_row,
        pl.BlockSpec((d, de2), lambda i: (0, 0)),
        pl.BlockSpec((de2 // 2, d), lambda i: (0, 0)),
    ] + [slot_spec(k) for k in range(TOP_K)] + [
        pl.BlockSpec((tm, TOP_K), lambda i: (i, 0)),
        _mod_spec(gate_slot, nlb, d), vec, vec,
    ]
    args = [x, tok_packed, s_gu, s_down, y3, y3, y3, y3, w, mod, ln_g.reshape(1, d), ln_b.reshape(1, d)]
    out_specs = [row]
    out_shape = [jax.ShapeDtypeStruct((n_out_rows, d), F32)]
    if with_h:
        in_specs += [_mod_spec(1, nlb, d), _mod_spec(0, nlb, d)]
        args += [next_mod, next_mod]
        out_specs.append(row)
        out_shape.append(jax.ShapeDtypeStruct((n_out_rows, d), BF16))
    return pl.pallas_call(
        functools.partial(_combine_kernel, with_h=with_h),
        grid=(n_out_rows // tm,),
        in_specs=in_specs,
        out_specs=out_specs,
        out_shape=out_shape,
        compiler_params=_params(1),
        name="moe_combine",
    )(*args)


def kernel(x, c, ctx, c_ctx, w_mod_down, w_mod_up, b_mod, ln_g, ln_b, gm_w_in, gm_v_g, gm_v_b, gm_w_s, gm_b_s, gm_w_out, na_w_qkv, na_rpb, na_w_out, ft_w_out, ga_w_qkv, ga_q_g, ga_k_g, ga_w_out, router_w, router_bias, exp_w_gate, exp_w_up, exp_w_down, sh_w_gate, sh_w_up, sh_w_down):
    b, n, d = x.shape
    nc = ctx.shape[1]
    assert b == 1 and c.shape[0] == 1
    n_tok = n + nc
    rows = n // GRID_W

    xs = jnp.concatenate([x[0], ctx[0]], axis=0)
    cond8 = jnp.zeros((8, d), F32).at[0].set(c[0]).at[1].set(c_ctx)
    mods = _ada_mod_all(cond8, w_mod_down, w_mod_up, b_mod)
    router_w_pad = jnp.pad(router_w, ((0, 0), (0, 0), (0, LANES - N_EXPERTS)))

    h = _modulate(xs, mods[0], 1, 0, n, BF16)
    for i in range(DEPTH):
        kind = i % 4
        mod = mods[i]
        last = i == DEPTH - 1

        if kind == 0:
            uv = _matmul(h, gm_w_in[0].astype(BF16), out_dtype=BF16, act="gelu")
            gated = _gmlp_gate(uv, gm_v_g[0], gm_v_b[0], gm_w_s[0], gm_b_s[0])
            o = _matmul(gated, gm_w_out[0].astype(BF16), out_dtype=BF16)
        elif kind == 1:
            qkv = _matmul(h, na_w_qkv[0].astype(BF16), out_dtype=BF16)
            a_lat = _na_attention(qkv, _na_bias_table(na_rpb[0], rows), n, nc)
            a_ctx = _ctx_attention(qkv, n, nc)
            o = _matmul(jnp.concatenate([a_lat, a_ctx], axis=0), na_w_out[0].astype(BF16), out_dtype=BF16)
        elif kind == 2:
            p = jnp.concatenate([_ft_positions_latent(h[:n]), _ft_positions_dense(h[n:])], axis=1)
            o = _matmul(_ft_channels(p), ft_w_out[0].astype(BF16), out_dtype=BF16)
        else:
            qkv = _matmul(h, ga_w_qkv[0].astype(BF16), out_dtype=BF16, tn=1024)
            cos, sin = _rope_tables(n, nc)
            qn = _norm_rope(qkv, 0, N_HEADS, ga_q_g[0], cos, sin, HEAD_DIM ** -0.5 * math.log2(math.e))
            kn = _norm_rope(qkv, N_HEADS // N_KV_HEADS, N_KV_HEADS, ga_k_g[0], cos, sin, 1.0)
            att = _gqa_attention(qn, kn.T, qkv, (N_HEADS + N_KV_HEADS), n)
            att = jnp.concatenate([att, jnp.zeros((nc, d), BF16)], axis=0)
            o = _matmul(att, ga_w_out[0].astype(BF16), out_dtype=BF16)

        xs, h2, logits = _resln(xs, o, mod, 2, ln_g[i, 0], ln_b[i, 0], n, next_mod=mod,
                                scale_slot=4, shift_slot=3, h_dtype=jnp.uint32, router_w=router_w_pad[i])

        w, block_e, n_valid, codes = _route(logits[:, :N_EXPERTS], router_bias[i], MOE_ROWS)
        y_slots = _routed_experts(h2, block_e, n_valid, codes, exp_w_gate, exp_w_up, exp_w_down, i)
        s_gu = jnp.concatenate([sh_w_gate[i], sh_w_up[i]], axis=-1).astype(BF16)
        s_down = sh_w_down[i].astype(BF16)
        if last:
            (xs,) = _moe_combine(xs, h2, s_gu, s_down, y_slots, w, mod, 5, ln_g[i, 1], ln_b[i, 1], n,
                                 n_out_rows=n)
        else:
            xs, h = _moe_combine(xs, h2, s_gu, s_down, y_slots, w, mod, 5, ln_g[i, 1], ln_b[i, 1], n,
                                 next_mod=mods[i + 1])
    return xs.reshape(b, n, d)
```

```python
import functools
import math

import jax
import jax.numpy as jnp
import numpy as np
from jax import lax
from jax.experimental import pallas as pl
from jax.experimental.pallas import tpu as pltpu

F32 = jnp.float32
BF16 = jnp.bfloat16

D_MODEL = 4096
DEPTH = 4
GRID_W = 64
N_MOD = 6
LN_EPS = 1e-5
RMS_EPS = 1e-6
DEEPNORM_ALPHA = (2 * DEPTH) ** 0.25
GM_CHUNK = 128
GM_GROUPS = 16
N_HEADS = 32
HEAD_DIM = 128
N_KV_HEADS = 8
ROPE_THETA = 10000.0
NA_WIN_H = 8
NA_WIN_W = 16
FT_GROUPS = 8
N_EXPERTS = 32
N_EXPERT_GROUPS = 4
TOP_K_GROUPS = 2
TOP_K = 4
D_EXPERT = 384
ROUTED_SCALE = 2.5

LANES = 128
VMEM_LIMIT_BYTES = 56 * 1024 * 1024
ROW_TILE = 256
MM_TILE_M = 1280
MM_TILE_N = 1024
MOE_ROWS = 256
COMBINE_ROWS = 128
NA_Q_ROWS = 4
NA_KEY_ROWS = 12
NA_HEADS_PER_STEP = 2
GQA_TQ = 256
GQA_TK = 2048
MASK_VALUE = -1e30


def _params(n_axes):
    return pltpu.CompilerParams(
        dimension_semantics=("arbitrary",) * n_axes,
        vmem_limit_bytes=VMEM_LIMIT_BYTES,
    )


def _dot(a, b, precision=None):
    return jnp.dot(a, b, preferred_element_type=F32, precision=precision)


def _dot_nt(a, b):
    return lax.dot_general(a, b, (((1,), (1,)), ((), ())), preferred_element_type=F32)


def _mm_kernel(a_ref, b_ref, o_ref, *, act):
    acc = _dot(a_ref[...], b_ref[...])
    if act == "gelu":
        acc = jax.nn.gelu(acc)
    o_ref[...] = acc.astype(o_ref.dtype)


def _matmul(a, b, *, out_dtype, act=None, tm=MM_TILE_M, tn=MM_TILE_N):
    m, k = a.shape
    n = b.shape[1]
    tm = min(tm, m)
    tn = min(tn, n)
    if jnp.dtype(out_dtype).itemsize == 4:
        tn = min(tn, MM_TILE_N // 2)
    while n % tn:
        tn //= 2
    assert m % tm == 0 and tn % LANES == 0
    return pl.pallas_call(
        functools.partial(_mm_kernel, act=act),
        grid=(m // tm, n // tn),
        in_specs=[
            pl.BlockSpec((tm, k), lambda i, j: (i, 0)),
            pl.BlockSpec((k, tn), lambda i, j: (0, j)),
        ],
        out_specs=pl.BlockSpec((tm, tn), lambda i, j: (i, j)),
        out_shape=jax.ShapeDtypeStruct((m, n), out_dtype),
        compiler_params=_params(2),
        name="matmul",
    )(a, b)


def _lmm_kernel(m_ref, x_ref, o_ref):
    o_ref[...] = _dot(m_ref[...], x_ref[...]).astype(o_ref.dtype)


def _left_matmul(mat, x, *, out_dtype, tn):
    r, k = mat.shape
    n = x.shape[1]
    assert n % tn == 0
    return pl.pallas_call(
        _lmm_kernel,
        grid=(n // tn,),
        in_specs=[
            pl.BlockSpec((r, k), lambda j: (0, 0)),
            pl.BlockSpec((k, tn), lambda j: (0, j)),
        ],
        out_specs=pl.BlockSpec((r, tn), lambda j: (0, j)),
        out_shape=jax.ShapeDtypeStruct((r, n), out_dtype),
        compiler_params=_params(1),
        name="left_matmul",
    )(mat, x)


def _mod_kernel(cond_ref, wd_ref, wu_ref, b_ref, o_ref):
    cond = cond_ref[...]
    t = _dot(cond * jax.nn.sigmoid(cond), wd_ref[...], precision=lax.Precision.HIGHEST)
    o_ref[...] = _dot(t, wu_ref[...], precision=lax.Precision.HIGHEST) + b_ref[...]


def _ada_mod_all(cond8, w_mod_down, w_mod_up, b_mod):
    depth, d, rank = w_mod_down.shape
    out = pl.pallas_call(
        _mod_kernel,
        grid=(depth, N_MOD),
        in_specs=[
            pl.BlockSpec((8, d), lambda l, j: (0, 0)),
            pl.BlockSpec((None, d, rank), lambda l, j: (l, 0, 0)),
            pl.BlockSpec((None, rank, d), lambda l, j: (l, 0, j)),
            pl.BlockSpec((None, None, 1, d), lambda l, j: (l, j, 0, 0)),
        ],
        out_specs=pl.BlockSpec((None, None, 8, d), lambda l, j: (l, j, 0, 0)),
        out_shape=jax.ShapeDtypeStruct((depth, N_MOD, 8, d), F32),
        compiler_params=_params(2),
        name="ada_mod",
    )(cond8, w_mod_down, w_mod_up, b_mod.reshape(depth, N_MOD, 1, d))
    return jnp.transpose(out[:, :, :2], (0, 2, 1, 3))[:, :, :, None, :]


def _mod_spec(slot, n_lat_blocks, d):
    return pl.BlockSpec((None, None, 1, d), lambda i: (i // n_lat_blocks, slot, 0, 0))


def _modulate_kernel(x_ref, sc_ref, sh_ref, h_ref):
    h_ref[...] = (x_ref[...] * (1.0 + sc_ref[...]) + sh_ref[...]).astype(h_ref.dtype)


def _modulate(x, mod, scale_slot, shift_slot, n_lat, out_dtype):
    t, d = x.shape
    tm = ROW_TILE
    nlb = n_lat // tm
    return pl.pallas_call(
        _modulate_kernel,
        grid=(t // tm,),
        in_specs=[
            pl.BlockSpec((tm, d), lambda i: (i, 0)),
            _mod_spec(scale_slot, nlb, d),
            _mod_spec(shift_slot, nlb, d),
        ],
        out_specs=pl.BlockSpec((tm, d), lambda i: (i, 0)),
        out_shape=jax.ShapeDtypeStruct((t, d), out_dtype),
        compiler_params=_params(1),
        name="modulate",
    )(x, mod, mod)


def _layer_norm_rows(z, g, b):
    mu = jnp.mean(z, axis=-1, keepdims=True)
    zc = z - mu
    var = jnp.mean(zc * zc, axis=-1, keepdims=True)
    return zc * lax.rsqrt(var + LN_EPS) * g + b


def _pack_halves(x):
    c = x.shape[1] // 2
    lo = lax.bitcast_convert_type(x[:, :c].astype(BF16).astype(F32), jnp.uint32)
    hi = lax.bitcast_convert_type(x[:, c:].astype(BF16).astype(F32), jnp.uint32)
    return hi | lax.shift_right_logical(lo, jnp.uint32(16))


def _unpack_low(u):
    return lax.bitcast_convert_type(lax.shift_left(u, jnp.uint32(16)), F32)


def _unpack_high(u):
    return lax.bitcast_convert_type(u & jnp.uint32(0xFFFF0000), F32)


def _unpack_halves(u):
    return jnp.concatenate([_unpack_low(u), _unpack_high(u)], axis=1)


def _resln_kernel(*refs, with_h, with_router):
    x_ref, o_ref, gate_ref, g_ref, b_ref = refs[:5]
    pos = 5
    if with_h:
        sc_ref, sh_ref = refs[pos:pos + 2]
        pos += 2
    if with_router:
        rw_ref = refs[pos]
        pos += 1
    xo_ref = refs[pos]
    pos += 1
    z = DEEPNORM_ALPHA * x_ref[...] + gate_ref[...] * o_ref[...].astype(F32)
    xn = _layer_norm_rows(z, g_ref[...], b_ref[...])
    xo_ref[...] = xn
    if with_h:
        h_ref = refs[pos]
        pos += 1
        h = xn * (1.0 + sc_ref[...]) + sh_ref[...]
        if h_ref.dtype == jnp.uint32:
            h_ref[...] = _pack_halves(h)
        else:
            h_ref[...] = h.astype(h_ref.dtype)
        if with_router:
            lg_ref = refs[pos]
            lg_ref[...] = _dot(h, rw_ref[...], precision=lax.Precision.HIGHEST)


def _resln(x, o, mod, gate_slot, ln_g, ln_b, n_lat, *, next_mod=None, scale_slot=None,
           shift_slot=None, h_dtype=None, router_w=None, n_out_rows=None):
    t, d = x.shape
    n_out_rows = t if n_out_rows is None else n_out_rows
    tm = ROW_TILE
    nlb = n_lat // tm
    with_h = next_mod is not None
    with_router = router_w is not None
    row = pl.BlockSpec((tm, d), lambda i: (i, 0))
    vec = pl.BlockSpec((1, d), lambda i: (0, 0))
    in_specs = [row, row, _mod_spec(gate_slot, nlb, d), vec, vec]
    args = [x, o, mod, ln_g.reshape(1, d), ln_b.reshape(1, d)]
    out_specs = [row]
    out_shape = [jax.ShapeDtypeStruct((n_out_rows, d), F32)]
    if with_h:
        in_specs += [_mod_spec(scale_slot, nlb, d), _mod_spec(shift_slot, nlb, d)]
        args += [next_mod, next_mod]
        h_cols = d // 2 if h_dtype == jnp.uint32 else d
        out_specs.append(pl.BlockSpec((tm, h_cols), lambda i: (i, 0)))
        out_shape.append(jax.ShapeDtypeStruct((n_out_rows, h_cols), h_dtype))
    if with_router:
        ne = router_w.shape[1]
        in_specs.append(pl.BlockSpec((d, ne), lambda i: (0, 0)))
        args.append(router_w)
        out_specs.append(pl.BlockSpec((tm, ne), lambda i: (i, 0)))
        out_shape.append(jax.ShapeDtypeStruct((n_out_rows, ne), F32))
    return pl.pallas_call(
        functools.partial(_resln_kernel, with_h=with_h, with_router=with_router),
        grid=(n_out_rows // tm,),
        in_specs=in_specs,
        out_specs=out_specs,
        out_shape=out_shape,
        compiler_params=_params(1),
        name="residual_layernorm",
    )(*args)


def _gmlp_gate_kernel(u_ref, v_ref, vg_ref, vb_ref, ws_ref, bs_ref, o_ref, *, n_groups, gw):
    vn = _layer_norm_rows(v_ref[...].astype(F32), vg_ref[...], vb_ref[...]).astype(BF16)
    for g in range(n_groups):
        cols = slice(g * gw, (g + 1) * gw)
        sv = _dot(ws_ref[g], vn[:, cols]) + bs_ref[:, cols]
        o_ref[:, cols] = (u_ref[:, cols].astype(F32) * sv).astype(o_ref.dtype)


def _gmlp_gate(uv, v_g, v_b, w_s, b_s):
    t, two_w = uv.shape
    width = two_w // 2
    n_groups, chunk, _ = w_s.shape
    gw = width // n_groups
    bias_full = jnp.repeat(b_s.T, gw, axis=1).astype(F32)
    return pl.pallas_call(
        functools.partial(_gmlp_gate_kernel, n_groups=n_groups, gw=gw),
        grid=(t // chunk,),
        in_specs=[
            pl.BlockSpec((chunk, width), lambda i: (i, 0)),
            pl.BlockSpec((chunk, width), lambda i: (i, 1)),
            pl.BlockSpec((1, width), lambda i: (0, 0)),
            pl.BlockSpec((1, width), lambda i: (0, 0)),
            pl.BlockSpec((n_groups, chunk, chunk), lambda i: (0, 0, 0)),
            pl.BlockSpec((chunk, width), lambda i: (0, 0)),
        ],
        out_specs=pl.BlockSpec((chunk, width), lambda i: (i, 0)),
        out_shape=jax.ShapeDtypeStruct((t, width), BF16),
        compiler_params=_params(1),
        name="gmlp_gate",
    )(uv, uv, v_g.reshape(1, width), v_b.reshape(1, width), w_s.astype(BF16), bias_full)


def _na_bias_table(rpb, rows):
    nb = rows // NA_Q_ROWS
    h = rpb.shape[0]
    n_dr = 2 * NA_WIN_H - 1
    c = np.arange(GRID_W)
    dc = c[None, :] - c[:, None] + NA_WIN_W - 1
    cs = np.clip(c - NA_WIN_W // 2, 0, GRID_W - NA_WIN_W)
    ok_c = (c[None, :] >= cs[:, None]) & (c[None, :] < cs[:, None] + NA_WIN_W)
    sel = ((np.arange(2 * NA_WIN_W - 1)[:, None, None] == dc[None]) & ok_c[None]).astype(np.float32)
    planes = jnp.einsum("hrd,dqk->hrqk", rpb.astype(F32), jnp.asarray(sel), precision=lax.Precision.HIGHEST)
    planes = jnp.where(ok_c[None, None], planes, MASK_VALUE)
    planes = jnp.concatenate([planes, jnp.full((h, 1, GRID_W, GRID_W), MASK_VALUE, F32)], axis=1)
    tables = []
    for b in (0, 1, nb - 1):
        r0 = b * NA_Q_ROWS
        kw0 = int(np.clip(r0 - NA_WIN_H // 2, 0, rows - NA_KEY_ROWS))
        r = r0 + np.arange(NA_Q_ROWS)[:, None]
        kr = kw0 + np.arange(NA_KEY_ROWS)[None, :]
        rs = np.clip(r - NA_WIN_H // 2, 0, rows - NA_WIN_H)
        ok_r = (kr >= rs) & (kr < rs + NA_WIN_H)
        dr = np.where(ok_r, kr - r + NA_WIN_H - 1, n_dr)
        tbl = jnp.stack([planes[:, int(d)] for d in dr.reshape(-1)], axis=1)
        tbl = tbl.reshape(h, NA_Q_ROWS, NA_KEY_ROWS, GRID_W, GRID_W).transpose(0, 1, 3, 2, 4)
        tables.append(tbl.reshape(h, NA_Q_ROWS * GRID_W, NA_KEY_ROWS * GRID_W))
    return jnp.stack(tables, axis=0)


def _na_kernel(q_ref, k_ref, v_ref, bias_ref, o_ref, *, rows, n_lat, n_ctx):
    b = pl.program_id(1)
    kw0 = jnp.clip(b * NA_Q_ROWS - NA_WIN_H // 2, 0, rows - NA_KEY_ROWS)
    start = pl.multiple_of(kw0 * GRID_W, GRID_W)
    nk = NA_KEY_ROWS * GRID_W
    scale = HEAD_DIM ** -0.5
    for j in range(NA_HEADS_PER_STEP):
        cols = slice(j * HEAD_DIM, (j + 1) * HEAD_DIM)
        q = q_ref[:, cols]
        kw = k_ref[pl.ds(start, nk), cols]
        vw = v_ref[pl.ds(start, nk), cols]
        kc = k_ref[pl.ds(n_lat, n_ctx), cols]
        vc = v_ref[pl.ds(n_lat, n_ctx), cols]
        s_loc = _dot_nt(q, kw) * scale + bias_ref[j]
        s_ctx = _dot_nt(q, kc) * scale
        m = jnp.maximum(jnp.max(s_loc, axis=-1, keepdims=True), jnp.max(s_ctx, axis=-1, keepdims=True))
        p_loc = jnp.exp(s_loc - m)
        p_ctx = jnp.exp(s_ctx - m)
        l = jnp.sum(p_loc, axis=-1, keepdims=True) + jnp.sum(p_ctx, axis=-1, keepdims=True)
        o = _dot(p_loc.astype(BF16), vw) + _dot(p_ctx.astype(BF16), vc)
        o_ref[:, cols] = (o / l).astype(o_ref.dtype)


def _na_attention(qkv, bias_table, n_lat, n_ctx):
    t = qkv.shape[0]
    rows = n_lat // GRID_W
    nb = rows // NA_Q_ROWS
    tq = NA_Q_ROWS * GRID_W
    nk = NA_KEY_ROWS * GRID_W
    h = N_HEADS
    hs = NA_HEADS_PER_STEP
    hg = h // hs
    w = hs * HEAD_DIM
    return pl.pallas_call(
        functools.partial(_na_kernel, rows=rows, n_lat=n_lat, n_ctx=n_ctx),
        grid=(hg, nb),
        in_specs=[
            pl.BlockSpec((tq, w), lambda hh, b: (b, hh)),
            pl.BlockSpec((t, w), lambda hh, b: (0, hg + hh)),
            pl.BlockSpec((t, w), lambda hh, b: (0, 2 * hg + hh)),
            pl.BlockSpec((None, hs, tq, nk),
                         lambda hh, b: (jnp.where(b == 0, 0, jnp.where(b == nb - 1, 2, 1)), hh, 0, 0)),
        ],
        out_specs=pl.BlockSpec((tq, w), lambda hh, b: (b, hh)),
        out_shape=jax.ShapeDtypeStruct((n_lat, h * HEAD_DIM), BF16),
        compiler_params=_params(2),
        name="neighbourhood_attention",
    )(qkv, qkv, qkv, bias_table)


def _ctx_attn_kernel(q_ref, k_ref, v_ref, o_ref):
    s = _dot_nt(q_ref[...], k_ref[...]) * (HEAD_DIM ** -0.5)
    m = jnp.max(s, axis=-1, keepdims=True)
    p = jnp.exp(s - m)
    l = jnp.sum(p, axis=-1, keepdims=True)
    o_ref[...] = (_dot(p.astype(BF16), v_ref[...]) / l).astype(o_ref.dtype)


def _ctx_attention(qkv, n_lat, n_ctx):
    h = N_HEADS
    rb = n_lat // n_ctx
    return pl.pallas_call(
        _ctx_attn_kernel,
        grid=(h,),
        in_specs=[
            pl.BlockSpec((n_ctx, HEAD_DIM), lambda hh: (rb, hh)),
            pl.BlockSpec((n_ctx, HEAD_DIM), lambda hh: (rb, h + hh)),
            pl.BlockSpec((n_ctx, HEAD_DIM), lambda hh: (rb, 2 * h + hh)),
        ],
        out_specs=pl.BlockSpec((n_ctx, HEAD_DIM), lambda hh: (0, hh)),
        out_shape=jax.ShapeDtypeStruct((n_ctx, h * HEAD_DIM), BF16),
        compiler_params=_params(1),
        name="context_attention",
    )(qkv, qkv, qkv)


def _dft_cos_sin(k_times_n, period):
    ang = (2.0 * math.pi / period) * (k_times_n % period).astype(F32)
    return jnp.cos(ang), -jnp.sin(ang)


def _ft_stage2_kernel(g_ref, y_ref, o_ref, *, nb):
    for i in range(nb):
        rhs = jnp.concatenate([y_ref[0, i], y_ref[1, i]], axis=0)
        res = _dot(g_ref[i], rhs)
        half = res.shape[0] // 2
        o_ref[0, :, i, :] = res[:half].astype(o_ref.dtype)
        o_ref[1, :, i, :] = res[half:].astype(o_ref.dtype)


def _ft_positions_latent(h_lat):
    n, d = h_lat.shape
    l1 = l2 = int(round(math.sqrt(n)))
    assert l1 * l2 == n
    idx = jnp.arange(l1, dtype=jnp.int32)
    re, im = _dft_cos_sin(idx[:, None] * idx[None, :], l1)
    m1 = (jnp.concatenate([re, im], axis=0) * (l1 ** -0.5)).astype(BF16)
    y = _left_matmul(m1, h_lat.reshape(l1, l2 * d), out_dtype=BF16, tn=8192)
    y = y.reshape(2, l1, l2, d)
    k1 = idx[:, None, None]
    k2 = idx[None, :, None]
    n2 = idx[None, None, :]
    gr, gi = _dft_cos_sin(n2 * k2 * l1 + n2 * k1, n)
    g = jnp.concatenate(
        [jnp.concatenate([gr, -gi], axis=2), jnp.concatenate([gi, gr], axis=2)], axis=1
    )
    g = (g * (l2 ** -0.5)).astype(BF16)
    nb, cb = 8, min(1024, d)
    p = pl.pallas_call(
        functools.partial(_ft_stage2_kernel, nb=nb),
        grid=(l1 // nb, d // cb),
        in_specs=[
            pl.BlockSpec((nb, 2 * l2, 2 * l2), lambda a, c: (a, 0, 0)),
            pl.BlockSpec((2, nb, l2, cb), lambda a, c: (0, a, 0, c)),
        ],
        out_specs=pl.BlockSpec((2, l2, nb, cb), lambda a, c: (0, 0, a, c)),
        out_shape=jax.ShapeDtypeStruct((2, l2, l1, d), BF16),
        compiler_params=_params(2),
        name="ft_stage2",
    )(g, y)
    return p.reshape(2, n, d)


def _ft_positions_dense(h_ctx):
    n, d = h_ctx.shape
    idx = jnp.arange(n, dtype=jnp.int32)
    re, im = _dft_cos_sin(idx[:, None] * idx[None, :], n)
    m = (jnp.concatenate([re, im], axis=0) * (n ** -0.5)).astype(BF16)
    return _left_matmul(m, h_ctx, out_dtype=BF16, tn=min(d, 2048)).reshape(2, n, d)


def _ft_channel_kernel(pr_ref, pi_ref, c_ref, s_ref, o_ref):
    o_ref[...] = (_dot(pr_ref[...], c_ref[...]) + _dot(pi_ref[...], s_ref[...])).astype(o_ref.dtype)


def _ft_channels(p):
    _, t, d = p.shape
    gc = d // FT_GROUPS
    idx = jnp.arange(gc, dtype=jnp.int32)
    re, im = _dft_cos_sin(idx[:, None] * idx[None, :], gc)
    cc = (re * (gc ** -0.5)).astype(BF16)
    sc = (-im * (gc ** -0.5)).astype(BF16)
    tm = MM_TILE_M if t % MM_TILE_M == 0 else t
    return pl.pallas_call(
        _ft_channel_kernel,
        grid=(t // tm, FT_GROUPS),
        in_specs=[
            pl.BlockSpec((None, tm, gc), lambda i, g: (0, i, g)),
            pl.BlockSpec((None, tm, gc), lambda i, g: (1, i, g)),
            pl.BlockSpec((gc, gc), lambda i, g: (0, 0)),
            pl.BlockSpec((gc, gc), lambda i, g: (0, 0)),
        ],
        out_specs=pl.BlockSpec((tm, gc), lambda i, g: (i, g)),
        out_shape=jax.ShapeDtypeStruct((t, d), BF16),
        compiler_params=_params(2),
        name="ft_channels",
    )(p, p, cc, sc)


def _rope_tables(n_lat, n_ctx):
    half = HEAD_DIM // 2
    t = jnp.arange(n_lat)
    inv = ROPE_THETA ** (-jnp.arange(0, half, 2, dtype=F32) / half)
    ang_r = (t // GRID_W).astype(F32)[:, None] * inv[None, :]
    ang_c = (t % GRID_W).astype(F32)[:, None] * inv[None, :]
    cos = jnp.concatenate([jnp.cos(ang_r)] * 2 + [jnp.cos(ang_c)] * 2, axis=-1)
    sin = jnp.concatenate([-jnp.sin(ang_r), jnp.sin(ang_r), -jnp.sin(ang_c), jnp.sin(ang_c)], axis=-1)
    cos = jnp.concatenate([cos, jnp.ones((n_ctx, HEAD_DIM), F32)], axis=0)
    sin = jnp.concatenate([sin, jnp.zeros((n_ctx, HEAD_DIM), F32)], axis=0)
    return cos, sin


def _norm_rope_kernel(x_ref, g_ref, cos_ref, sin_ref, o_ref, *, n_heads, scale):
    quarter = HEAD_DIM // 4
    lane = lax.broadcasted_iota(jnp.int32, (x_ref.shape[0], HEAD_DIM), 1)
    first = (lane % (2 * quarter)) < quarter
    cos = cos_ref[...]
    sin = sin_ref[...]
    g = g_ref[...]
    for hh in range(n_heads):
        cols = slice(hh * HEAD_DIM, (hh + 1) * HEAD_DIM)
        x = x_ref[:, cols].astype(F32)
        xn = x * lax.rsqrt(jnp.mean(x * x, axis=-1, keepdims=True) + RMS_EPS) * g
        partner = jnp.where(first, pltpu.roll(xn, HEAD_DIM - quarter, 1), pltpu.roll(xn, quarter, 1))
        o_ref[:, cols] = ((xn * cos + partner * sin) * scale).astype(o_ref.dtype)


def _norm_rope(qkv, col_block, n_heads, gain, cos, sin, scale):
    t = qkv.shape[0]
    tm = ROW_TILE
    w = n_heads * HEAD_DIM
    return pl.pallas_call(
        functools.partial(_norm_rope_kernel, n_heads=n_heads, scale=scale),
        grid=(t // tm,),
        in_specs=[
            pl.BlockSpec((tm, w), lambda i: (i, col_block)),
            pl.BlockSpec((1, HEAD_DIM), lambda i: (0, 0)),
            pl.BlockSpec((tm, HEAD_DIM), lambda i: (i, 0)),
            pl.BlockSpec((tm, HEAD_DIM), lambda i: (i, 0)),
        ],
        out_specs=pl.BlockSpec((tm, w), lambda i: (i, 0)),
        out_shape=jax.ShapeDtypeStruct((t, w), BF16),
        compiler_params=_params(1),
        name="norm_rope",
    )(qkv, gain.reshape(1, HEAD_DIM), cos, sin)


def _lane_tiles(x):
    return [x[:, j * LANES:(j + 1) * LANES] for j in range(x.shape[1] // LANES)]


FLASH_SUB_ROWS = 64


def _flash_scores(q_sc, kt, s_sc, slot):
    s_sc[slot, :, :kt.shape[1]] = _dot(q_sc[...], kt)


def _flash_softmax(s_sc, p_sc, alpha_sc, m_sc, slot, n):
    for b in range(s_sc.shape[1] // FLASH_SUB_ROWS):
        rows = slice(b * FLASH_SUB_ROWS, (b + 1) * FLASH_SUB_ROWS)
        tiles = _lane_tiles(s_sc[slot, rows, :n])
        m_prev = m_sc[rows]
        m_new = jnp.maximum(m_prev, jnp.max(functools.reduce(jnp.maximum, tiles), axis=-1, keepdims=True))
        alpha_sc[slot, rows] = jnp.exp2(m_prev - m_new)
        p_sc[slot, rows, :n] = jnp.concatenate([jnp.exp2(t - m_new).astype(BF16) for t in tiles], axis=1)
        m_sc[rows] = m_new


def _flash_accumulate(p_sc, alpha_sc, acc_sc, slot, vs):
    n = vs.shape[0]
    v_ext = jnp.concatenate([vs, jnp.ones_like(vs)], axis=1)
    alpha = alpha_sc[slot]
    acc_sc[...] = jnp.concatenate([alpha, alpha], axis=1) * acc_sc[...] + _dot(p_sc[slot, :, :n], v_ext)


def _flash_kernel(q_ref, k_ref, v_ref, o_ref, q_sc, s_sc, p_sc, alpha_sc, m_sc, acc_sc, *,
                  rep, tq, tk, n_main, n_tail):
    for r in range(rep):
        q_sc[r * tq:(r + 1) * tq, :] = q_ref[:, r * HEAD_DIM:(r + 1) * HEAD_DIM]
    m_sc[...] = jnp.full(m_sc.shape, -jnp.inf, F32)
    acc_sc[...] = jnp.zeros(acc_sc.shape, F32)
    n_pairs = n_main // (2 * tk)

    def body(c, carry):
        base = pl.multiple_of(c * 2 * tk, 2 * tk)
        nxt = pl.multiple_of(jnp.minimum(c + 1, n_pairs - 1) * 2 * tk, 2 * tk)
        _flash_scores(q_sc, k_ref[:, pl.ds(base + tk, tk)], s_sc, 1)
        _flash_softmax(s_sc, p_sc, alpha_sc, m_sc, 0, tk)
        _flash_accumulate(p_sc, alpha_sc, acc_sc, 0, v_ref[pl.ds(base, tk), :])
        _flash_scores(q_sc, k_ref[:, pl.ds(nxt, tk)], s_sc, 0)
        _flash_softmax(s_sc, p_sc, alpha_sc, m_sc, 1, tk)
        _flash_accumulate(p_sc, alpha_sc, acc_sc, 1, v_ref[pl.ds(base + tk, tk), :])
        return carry

    _flash_scores(q_sc, k_ref[:, pl.ds(0, tk)], s_sc, 0)
    lax.fori_loop(0, n_pairs, body, 0)
    if n_tail:
        _flash_scores(q_sc, k_ref[:, pl.ds(n_main, n_tail)], s_sc, 0)
        _flash_softmax(s_sc, p_sc, alpha_sc, m_sc, 0, n_tail)
        _flash_accumulate(p_sc, alpha_sc, acc_sc, 0, v_ref[pl.ds(n_main, n_tail), :])
    for r in range(rep):
        rows = slice(r * tq, (r + 1) * tq)
        o_ref[:, r * HEAD_DIM:(r + 1) * HEAD_DIM] = (
            acc_sc[rows, :HEAD_DIM] / acc_sc[rows, HEAD_DIM:]).astype(o_ref.dtype)


def _gqa_attention(q, k_t, qkv, v_col_block0, n_q):
    t = k_t.shape[1]
    rep = N_HEADS // N_KV_HEADS
    tq, tk = GQA_TQ, GQA_TK
    n_main = t // (2 * tk) * (2 * tk)
    n_tail = t - n_main
    assert n_q % tq == 0 and n_tail % LANES == 0 and n_tail <= tk
    w = rep * HEAD_DIM
    return pl.pallas_call(
        functools.partial(_flash_kernel, rep=rep, tq=tq, tk=tk, n_main=n_main, n_tail=n_tail),
        grid=(N_KV_HEADS, n_q // tq),
        in_specs=[
            pl.BlockSpec((tq, w), lambda g, i: (i, g)),
            pl.BlockSpec((HEAD_DIM, t), lambda g, i: (g, 0)),
            pl.BlockSpec((t, HEAD_DIM), lambda g, i: (0, v_col_block0 + g)),
        ],
        out_specs=pl.BlockSpec((tq, w), lambda g, i: (i, g)),
        out_shape=jax.ShapeDtypeStruct((n_q, N_HEADS * HEAD_DIM), BF16),
        scratch_shapes=[
            pltpu.VMEM((rep * tq, HEAD_DIM), BF16),
            pltpu.VMEM((2, rep * tq, tk), F32),
            pltpu.VMEM((2, rep * tq, tk), BF16),
            pltpu.VMEM((2, rep * tq, HEAD_DIM), F32),
            pltpu.VMEM((rep * tq, HEAD_DIM), F32),
            pltpu.VMEM((rep * tq, 2 * HEAD_DIM), F32),
        ],
        compiler_params=_params(2),
        name="gqa_flash_attention",
    )(q, k_t, qkv)


def _route(logits, router_bias, block_rows):
    n_tok, n_e = logits.shape
    scores = jax.nn.sigmoid(logits)
    biased = scores + router_bias.astype(F32)
    per_g = n_e // N_EXPERT_GROUPS
    grp = biased.reshape(n_tok, N_EXPERT_GROUPS, per_g)
    g_first, g_mask1 = _first_max(grp)
    g_second, _ = _first_max(jnp.where(g_mask1, -jnp.inf, grp))
    g_score = g_first + g_second
    g_sel = jnp.zeros(g_score.shape, bool)
    for _ in range(TOP_K_GROUPS):
        _, pick = _first_max(jnp.where(g_sel, -jnp.inf, g_score))
        g_sel = g_sel | pick
    e_mask = jnp.repeat(g_sel, per_g, axis=1)
    cand = jnp.where(e_mask, biased, -jnp.inf)
    picks = []
    for _ in range(TOP_K):
        _, pick = _first_max(cand)
        picks.append(pick)
        cand = jnp.where(pick, -jnp.inf, cand)
    onehot = jnp.stack(picks, axis=1)
    w = jnp.sum(jnp.where(onehot, scores[:, None, :], 0.0), axis=-1)
    w = w / jnp.sum(w, axis=-1, keepdims=True) * ROUTED_SCALE

    n_asg = n_tok * TOP_K
    oh = onehot.reshape(n_asg, n_e).astype(jnp.int32)
    csum = jnp.cumsum(oh, axis=0)
    counts = csum[-1]
    padded = (counts + block_rows - 1) // block_rows * block_rows
    pad_end = jnp.cumsum(padded)
    pad_start = pad_end - padded
    dest = jnp.sum(oh * (csum - 1 + pad_start[None, :]), axis=1).astype(jnp.int32)
    n_blocks = -(-(n_asg + n_e * (block_rows - 1)) // block_rows)
    blk_row0 = jnp.arange(n_blocks, dtype=jnp.int32) * block_rows
    block_e = jnp.minimum(jnp.sum((pad_end[None, :] <= blk_row0[:, None]).astype(jnp.int32), axis=1), n_e - 1)
    be_oh = (block_e[:, None] == jnp.arange(n_e, dtype=jnp.int32)[None, :]).astype(jnp.int32)
    seg_left = jnp.sum(be_oh * (counts + pad_start)[None, :], axis=1) - blk_row0
    n_valid = jnp.clip(seg_left, 0, block_rows).astype(jnp.int32)
    codes = _plan_rows(dest, n_blocks * block_rows, block_rows)
    return w, block_e.astype(jnp.int32), n_valid, codes


def _first_max(x):
    m = jnp.max(x, axis=-1, keepdims=True)
    idx = lax.broadcasted_iota(jnp.int32, x.shape, x.ndim - 1)
    first = jnp.min(jnp.where(x == m, idx, x.shape[-1]), axis=-1, keepdims=True)
    return m[..., 0], idx == first


CODE_SLOT_SHIFT = 16
EXPERT_K_CHUNK = 512


def _plan_kernel(dest_hbm, pad_hbm, codes_ref, dest_smem, sem, *, chunk):
    j = pl.program_id(0)
    log_k = TOP_K.bit_length() - 1

    @pl.when(j == 0)
    def _():
        init = pltpu.make_async_copy(pad_hbm, codes_ref, sem)
        init.start()
        init.wait()

    cp = pltpu.make_async_copy(dest_hbm.at[j], dest_smem, sem)
    cp.start()
    cp.wait()

    def place(a, c):
        g = j * chunk + a
        codes_ref[dest_smem[a]] = lax.shift_right_logical(g, log_k) | ((g & (TOP_K - 1)) << CODE_SLOT_SHIFT)
        return c

    lax.fori_loop(0, chunk, place, 0, unroll=16)


def _plan_rows(dest, n_rows, block_rows):
    n_asg = dest.shape[0]
    assert block_rows & (block_rows - 1) == 0 and TOP_K & (TOP_K - 1) == 0
    chunk = max(c for c in range(LANES, 8192 + 1, LANES) if n_asg % c == 0)
    n_chunks = n_asg // chunk
    r = jnp.arange(n_rows, dtype=jnp.int32)
    pad_codes = ((r % block_rows) + ((r // block_rows) % 2) * block_rows) | (TOP_K << CODE_SLOT_SHIFT)
    return pl.pallas_call(
        functools.partial(_plan_kernel, chunk=chunk),
        grid=(n_chunks,),
        in_specs=[pl.BlockSpec(memory_space=pl.ANY), pl.BlockSpec(memory_space=pl.ANY)],
        out_specs=pl.BlockSpec(memory_space=pltpu.SMEM),
        out_shape=jax.ShapeDtypeStruct((n_rows,), jnp.int32),
        scratch_shapes=[pltpu.SMEM((chunk,), jnp.int32), pltpu.SemaphoreType.DMA(())],
        compiler_params=_params(1),
        name="plan_rows",
    )(dest.reshape(n_chunks, chunk), pad_codes)


def _swiglu_rows(x, wgu_ref, wd_ref):
    hgu = _dot(x, wgu_ref[...])
    de = hgu.shape[1] // 2
    gate = hgu[:, :de]
    hb = (gate * jax.nn.sigmoid(gate) * hgu[:, de:]).astype(BF16)
    return _dot(hb, wd_ref[...])


def _expert_kernel(be_ref, nv_ref, codes_ref, t_hbm, wg_ref, wu_ref, wd_ref, y_hbm,
                   xbuf, ybuf, sem_in, sem_out, *, n_tok, br):
    i = pl.program_id(0)
    last_block = pl.num_programs(0) - 2
    slot = i % 2
    active = nv_ref[i] > 0
    prev_active = (i > 0) & (nv_ref[jnp.maximum(i - 1, 0)] > 0)
    token_mask = (1 << CODE_SLOT_SHIFT) - 1
    half = xbuf.shape[2]
    n_chunks = 2 * half // EXPERT_K_CHUNK
    group = br // n_chunks

    def gather(block, buf_slot, lo, hi):
        for r in range(lo, hi):
            tok = codes_ref[block * br + r] & token_mask
            pltpu.make_async_copy(t_hbm.at[pl.ds(tok, 1)], xbuf.at[buf_slot, pl.ds(r, 1)],
                                  sem_in.at[buf_slot]).start(priority=r % 2)

    def gather_wait(buf_slot):
        pltpu.make_async_copy(t_hbm.at[pl.ds(0, br)], xbuf.at[buf_slot], sem_in.at[buf_slot]).wait()

    def scatter(block, buf_slot, lo, hi):
        for r in range(lo, hi):
            code = codes_ref[block * br + r]
            dst = lax.shift_right_logical(code, CODE_SLOT_SHIFT) * n_tok + (code & token_mask)
            pltpu.make_async_copy(ybuf.at[buf_slot, pl.ds(r, 1)], y_hbm.at[pl.ds(dst, 1)],
                                  sem_out.at[buf_slot]).start(priority=r % 2)

    def scatter_wait(buf_slot):
        pltpu.make_async_copy(ybuf.at[buf_slot], y_hbm.at[pl.ds(0, br)], sem_out.at[buf_slot]).wait()

    def compute(with_scatter):
        nxt = jnp.minimum(i + 1, last_block)
        gather_wait(slot)
        hgu = None
        for c in range(n_chunks):
            cols = slice(c * EXPERT_K_CHUNK, (c + 1) * EXPERT_K_CHUNK)
            lo = c * EXPERT_K_CHUNK % half
            words = xbuf[slot, :, lo:lo + EXPERT_K_CHUNK]
            xc = _unpack_low(words) if c * EXPERT_K_CHUNK < half else _unpack_high(words)
            w_chunk = jnp.concatenate([wg_ref[cols, :].astype(BF16), wu_ref[cols, :].astype(BF16)], axis=1)
            part = _dot(xc.astype(BF16), w_chunk)
            hgu = part if hgu is None else hgu + part
            gather(nxt, 1 - slot, c * group, (c + 1) * group)
            if with_scatter:
                scatter(i - 1, 1 - slot, c * group, (c + 1) * group)
        de = hgu.shape[1] // 2
        gate = hgu[:, :de]
        hb = (gate * jax.nn.sigmoid(gate) * hgu[:, de:]).astype(BF16)
        ybuf[slot] = _pack_halves(_dot(hb, wd_ref[...].astype(BF16)))
        if with_scatter:
            scatter_wait(1 - slot)

    @pl.when(i == 0)
    def _():
        gather(0, 0, 0, br)

    @pl.when(active & (i == 0))
    def _():
        compute(False)

    @pl.when(active & (i > 0))
    def _():
        compute(True)

    @pl.when(jnp.logical_not(active) & ((i == 0) | prev_active))
    def _():
        gather_wait(slot)

        @pl.when(prev_active)
        def _():
            scatter(i - 1, 1 - slot, 0, br)
            scatter_wait(1 - slot)


def _routed_experts(tok_packed, block_e, n_valid, codes, w_gate, w_up, w_down, layer):
    n_tok, half = tok_packed.shape
    d = 2 * half
    n_blocks = block_e.shape[0]
    br = codes.shape[0] // n_blocks
    de = w_gate.shape[3]
    assert half % EXPERT_K_CHUNK == 0 and br % (d // EXPERT_K_CHUNK) == 0
    block_e = jnp.concatenate([block_e, block_e[-1:]])
    n_valid = jnp.concatenate([n_valid, jnp.zeros((1,), jnp.int32)])
    grid_spec = pltpu.PrefetchScalarGridSpec(
        num_scalar_prefetch=3,
        grid=(n_blocks + 1,),
        in_specs=[
            pl.BlockSpec(memory_space=pl.ANY),
            pl.BlockSpec((None, None, d, de), lambda i, be, nv, cd: (layer, be[i], 0, 0)),
            pl.BlockSpec((None, None, d, de), lambda i, be, nv, cd: (layer, be[i], 0, 0)),
            pl.BlockSpec((None, None, de, d), lambda i, be, nv, cd: (layer, be[i], 0, 0)),
        ],
        out_specs=pl.BlockSpec(memory_space=pl.ANY),
        scratch_shapes=[
            pltpu.VMEM((2, br, half), jnp.uint32),
            pltpu.VMEM((2, br, half), jnp.uint32),
            pltpu.SemaphoreType.DMA((2,)),
            pltpu.SemaphoreType.DMA((2,)),
        ],
    )
    return pl.pallas_call(
        functools.partial(_expert_kernel, n_tok=n_tok, br=br),
        grid_spec=grid_spec,
        out_shape=jax.ShapeDtypeStruct((TOP_K * n_tok + 2 * br, half), jnp.uint32),
        compiler_params=_params(1),
        name="routed_experts",
    )(block_e, n_valid, codes, tok_packed, w_gate, w_up, w_down)


def _combine_kernel(*refs, with_h):
    (x_ref, tok_ref, sgu_ref, sd_ref, y0_ref, y1_ref, y2_ref, y3_ref, w_ref, gate_ref, g_ref,
     b_ref) = refs[:12]
    w = w_ref[...]
    y = _swiglu_rows(_unpack_halves(tok_ref[...]).astype(BF16), sgu_ref, sd_ref)
    for k, y_ref in enumerate((y0_ref, y1_ref, y2_ref, y3_ref)):
        y = y + w[:, k:k + 1] * _unpack_halves(y_ref[...])
    z = DEEPNORM_ALPHA * x_ref[...] + gate_ref[...] * y
    xn = _layer_norm_rows(z, g_ref[...], b_ref[...])
    if with_h:
        sc_ref, shf_ref, xo_ref, h_ref = refs[12:16]
        xo_ref[...] = xn
        h_ref[...] = (xn * (1.0 + sc_ref[...]) + shf_ref[...]).astype(h_ref.dtype)
    else:
        xo_ref = refs[12]
        xo_ref[...] = xn


def _moe_combine(x, tok_packed, s_gu, s_down, y_slots, w, mod, gate_slot, ln_g, ln_b, n_lat, *,
                 next_mod=None, n_out_rows=None):
    t, d = x.shape
    n_out_rows = t if n_out_rows is None else n_out_rows
    tm = COMBINE_ROWS
    nlb = n_lat // tm
    with_h = next_mod is not None
    y3 = y_slots
    de2 = s_gu.shape[1]
    row = pl.BlockSpec((tm, d), lambda i: (i, 0))
    packed_row = pl.BlockSpec((tm, d // 2), lambda i: (i, 0))
    vec = pl.BlockSpec((1, d), lambda i: (0, 0))

    def slot_spec(k):
        return pl.BlockSpec((tm, d // 2), lambda i: (k * (t // tm) + i, 0))

    in_specs = [
        row, packed_row,
        pl.BlockSpec((d, de2), lambda i: (0, 0)),
        pl.BlockSpec((de2 // 2, d), lambda i: (0, 0)),
    ] + [slot_spec(k) for k in range(TOP_K)] + [
        pl.BlockSpec((tm, TOP_K), lambda i: (i, 0)),
        _mod_spec(gate_slot, nlb, d), vec, vec,
    ]
    args = [x, tok_packed, s_gu, s_down, y3, y3, y3, y3, w, mod, ln_g.reshape(1, d), ln_b.reshape(1, d)]
    out_specs = [row]
    out_shape = [jax.ShapeDtypeStruct((n_out_rows, d), F32)]
    if with_h:
        in_specs += [_mod_spec(1, nlb, d), _mod_spec(0, nlb, d)]
        args += [next_mod, next_mod]
        out_specs.append(row)
        out_shape.append(jax.ShapeDtypeStruct((n_out_rows, d), BF16))
    return pl.pallas_call(
        functools.partial(_combine_kernel, with_h=with_h),
        grid=(n_out_rows // tm,),
        in_specs=in_specs,
        out_specs=out_specs,
        out_shape=out_shape,
        compiler_params=_params(1),
        name="moe_combine",
    )(*args)


def kernel(x, c, ctx, c_ctx, w_mod_down, w_mod_up, b_mod, ln_g, ln_b, gm_w_in, gm_v_g, gm_v_b, gm_w_s, gm_b_s, gm_w_out, na_w_qkv, na_rpb, na_w_out, ft_w_out, ga_w_qkv, ga_q_g, ga_k_g, ga_w_out, router_w, router_bias, exp_w_gate, exp_w_up, exp_w_down, sh_w_gate, sh_w_up, sh_w_down):
    b, n, d = x.shape
    nc = ctx.shape[1]
    assert b == 1 and c.shape[0] == 1
    n_tok = n + nc
    rows = n // GRID_W

    xs = jnp.concatenate([x[0], ctx[0]], axis=0)
    cond8 = jnp.zeros((8, d), F32).at[0].set(c[0]).at[1].set(c_ctx)
    mods = _ada_mod_all(cond8, w_mod_down, w_mod_up, b_mod)
    router_w_pad = jnp.pad(router_w, ((0, 0), (0, 0), (0, LANES - N_EXPERTS)))

    h = _modulate(xs, mods[0], 1, 0, n, BF16)
    for i in range(DEPTH):
        kind = i % 4
        mod = mods[i]
        last = i == DEPTH - 1

        if kind == 0:
            uv = _matmul(h, gm_w_in[0].astype(BF16), out_dtype=BF16, act="gelu")
            gated = _gmlp_gate(uv, gm_v_g[0], gm_v_b[0], gm_w_s[0], gm_b_s[0])
            o = _matmul(gated, gm_w_out[0].astype(BF16), out_dtype=BF16)
        elif kind == 1:
            qkv = _matmul(h, na_w_qkv[0].astype(BF16), out_dtype=BF16)
            a_lat = _na_attention(qkv, _na_bias_table(na_rpb[0], rows), n, nc)
            a_ctx = _ctx_attention(qkv, n, nc)
            o = _matmul(jnp.concatenate([a_lat, a_ctx], axis=0), na_w_out[0].astype(BF16), out_dtype=BF16)
        elif kind == 2:
            p = jnp.concatenate([_ft_positions_latent(h[:n]), _ft_positions_dense(h[n:])], axis=1)
            o = _matmul(_ft_channels(p), ft_w_out[0].astype(BF16), out_dtype=BF16)
        else:
            qkv = _matmul(h, ga_w_qkv[0].astype(BF16), out_dtype=BF16, tn=1024)
            cos, sin = _rope_tables(n, nc)
            qn = _norm_rope(qkv, 0, N_HEADS, ga_q_g[0], cos, sin, HEAD_DIM ** -0.5 * math.log2(math.e))
            kn = _norm_rope(qkv, N_HEADS // N_KV_HEADS, N_KV_HEADS, ga_k_g[0], cos, sin, 1.0)
            att = _gqa_attention(qn, kn.T, qkv, (N_HEADS + N_KV_HEADS), n)
            att = jnp.concatenate([att, jnp.zeros((nc, d), BF16)], axis=0)
            o = _matmul(att, ga_w_out[0].astype(BF16), out_dtype=BF16)

        xs, h2, logits = _resln(xs, o, mod, 2, ln_g[i, 0], ln_b[i, 0], n, next_mod=mod,
                                scale_slot=4, shift_slot=3, h_dtype=jnp.uint32, router_w=router_w_pad[i])

        w, block_e, n_valid, codes = _route(logits[:, :N_EXPERTS], router_bias[i], MOE_ROWS)
        y_slots = _routed_experts(h2, block_e, n_valid, codes, exp_w_gate, exp_w_up, exp_w_down, i)
        s_gu = jnp.concatenate([sh_w_gate[i], sh_w_up[i]], axis=-1).astype(BF16)
        s_down = sh_w_down[i].astype(BF16)
        if last:
            (xs,) = _moe_combine(xs, h2, s_gu, s_down, y_slots, w, mod, 5, ln_g[i, 1], ln_b[i, 1], n,
                                 n_out_rows=n)
        else:
            xs, h = _moe_combine(xs, h2, s_gu, s_down, y_slots, w, mod, 5, ln_g[i, 1], ln_b[i, 1], n,
                                 next_mod=mods[i + 1])
    return xs.reshape(b, n, d)
```

```python
import functools
import math

import jax
import jax.numpy as jnp
import numpy as np
from jax import lax
from jax.experimental import pallas as pl
from jax.experimental.pallas import tpu as pltpu

F32 = jnp.float32
BF16 = jnp.bfloat16

D_MODEL = 4096
DEPTH = 4
GRID_W = 64
N_MOD = 6
LN_EPS = 1e-5
RMS_EPS = 1e-6
DEEPNORM_ALPHA = (2 * DEPTH) ** 0.25
GM_CHUNK = 128
GM_GROUPS = 16
N_HEADS = 32
HEAD_DIM = 128
N_KV_HEADS = 8
ROPE_THETA = 10000.0
NA_WIN_H = 8
NA_WIN_W = 16
FT_GROUPS = 8
N_EXPERTS = 32
N_EXPERT_GROUPS = 4
TOP_K_GROUPS = 2
TOP_K = 4
D_EXPERT = 384
ROUTED_SCALE = 2.5

LANES = 128
VMEM_LIMIT_BYTES = 56 * 1024 * 1024
ROW_TILE = 256
MM_TILE_M = 1280
MM_TILE_N = 1024
MOE_ROWS = 256
COMBINE_ROWS = 128
NA_Q_ROWS = 4
NA_KEY_ROWS = 12
NA_HEADS_PER_STEP = 2
GQA_TQ = 256
GQA_TK = 2048
MASK_VALUE = -1e30


def _params(n_axes):
    return pltpu.CompilerParams(
        dimension_semantics=("arbitrary",) * n_axes,
        vmem_limit_bytes=VMEM_LIMIT_BYTES,
    )


def _dot(a, b, precision=None):
    return jnp.dot(a, b, preferred_element_type=F32, precision=precision)


def _dot_nt(a, b):
    return lax.dot_general(a, b, (((1,), (1,)), ((), ())), preferred_element_type=F32)


def _mm_kernel(a_ref, b_ref, o_ref, *, act):
    acc = _dot(a_ref[...], b_ref[...])
    if act == "gelu":
        acc = jax.nn.gelu(acc)
    o_ref[...] = acc.astype(o_ref.dtype)


def _matmul(a, b, *, out_dtype, act=None, tm=MM_TILE_M, tn=MM_TILE_N):
    m, k = a.shape
    n = b.shape[1]
    tm = min(tm, m)
    tn = min(tn, n)
    if jnp.dtype(out_dtype).itemsize == 4:
        tn = min(tn, MM_TILE_N // 2)
    while n % tn:
        tn //= 2
    assert m % tm == 0 and tn % LANES == 0
    return pl.pallas_call(
        functools.partial(_mm_kernel, act=act),
        grid=(m // tm, n // tn),
        in_specs=[
            pl.BlockSpec((tm, k), lambda i, j: (i, 0)),
            pl.BlockSpec((k, tn), lambda i, j: (0, j)),
        ],
        out_specs=pl.BlockSpec((tm, tn), lambda i, j: (i, j)),
        out_shape=jax.ShapeDtypeStruct((m, n), out_dtype),
        compiler_params=_params(2),
        name="matmul",
    )(a, b)


def _lmm_kernel(m_ref, x_ref, o_ref):
    o_ref[...] = _dot(m_ref[...], x_ref[...]).astype(o_ref.dtype)


def _left_matmul(mat, x, *, out_dtype, tn):
    r, k = mat.shape
    n = x.shape[1]
    assert n % tn == 0
    return pl.pallas_call(
        _lmm_kernel,
        grid=(n // tn,),
        in_specs=[
            pl.BlockSpec((r, k), lambda j: (0, 0)),
            pl.BlockSpec((k, tn), lambda j: (0, j)),
        ],
        out_specs=pl.BlockSpec((r, tn), lambda j: (0, j)),
        out_shape=jax.ShapeDtypeStruct((r, n), out_dtype),
        compiler_params=_params(1),
        name="left_matmul",
    )(mat, x)


def _mod_kernel(cond_ref, wd_ref, wu_ref, b_ref, o_ref):
    cond = cond_ref[...]
    t = _dot(cond * jax.nn.sigmoid(cond), wd_ref[...], precision=lax.Precision.HIGHEST)
    o_ref[...] = _dot(t, wu_ref[...], precision=lax.Precision.HIGHEST) + b_ref[...]


def _ada_mod_all(cond8, w_mod_down, w_mod_up, b_mod):
    depth, d, rank = w_mod_down.shape
    out = pl.pallas_call(
        _mod_kernel,
        grid=(depth, N_MOD),
        in_specs=[
            pl.BlockSpec((8, d), lambda l, j: (0, 0)),
            pl.BlockSpec((None, d, rank), lambda l, j: (l, 0, 0)),
            pl.BlockSpec((None, rank, d), lambda l, j: (l, 0, j)),
            pl.BlockSpec((None, None, 1, d), lambda l, j: (l, j, 0, 0)),
        ],
        out_specs=pl.BlockSpec((None, None, 8, d), lambda l, j: (l, j, 0, 0)),
        out_shape=jax.ShapeDtypeStruct((depth, N_MOD, 8, d), F32),
        compiler_params=_params(2),
        name="ada_mod",
    )(cond8, w_mod_down, w_mod_up, b_mod.reshape(depth, N_MOD, 1, d))
    return jnp.transpose(out[:, :, :2], (0, 2, 1, 3))[:, :, :, None, :]


def _mod_spec(slot, n_lat_blocks, d):
    return pl.BlockSpec((None, None, 1, d), lambda i: (i // n_lat_blocks, slot, 0, 0))


def _modulate_kernel(x_ref, sc_ref, sh_ref, h_ref):
    h_ref[...] = (x_ref[...] * (1.0 + sc_ref[...]) + sh_ref[...]).astype(h_ref.dtype)


def _modulate(x, mod, scale_slot, shift_slot, n_lat, out_dtype):
    t, d = x.shape
    tm = ROW_TILE
    nlb = n_lat // tm
    return pl.pallas_call(
        _modulate_kernel,
        grid=(t // tm,),
        in_specs=[
            pl.BlockSpec((tm, d), lambda i: (i, 0)),
            _mod_spec(scale_slot, nlb, d),
            _mod_spec(shift_slot, nlb, d),
        ],
        out_specs=pl.BlockSpec((tm, d), lambda i: (i, 0)),
        out_shape=jax.ShapeDtypeStruct((t, d), out_dtype),
        compiler_params=_params(1),
        name="modulate",
    )(x, mod, mod)


def _layer_norm_rows(z, g, b):
    mu = jnp.mean(z, axis=-1, keepdims=True)
    zc = z - mu
    var = jnp.mean(zc * zc, axis=-1, keepdims=True)
    return zc * lax.rsqrt(var + LN_EPS) * g + b


def _pack_halves(x):
    c = x.shape[1] // 2
    lo = lax.bitcast_convert_type(x[:, :c].astype(BF16).astype(F32), jnp.uint32)
    hi = lax.bitcast_convert_type(x[:, c:].astype(BF16).astype(F32), jnp.uint32)
    return hi | lax.shift_right_logical(lo, jnp.uint32(16))


def _unpack_low(u):
    return lax.bitcast_convert_type(lax.shift_left(u, jnp.uint32(16)), F32)


def _unpack_high(u):
    return lax.bitcast_convert_type(u & jnp.uint32(0xFFFF0000), F32)


def _unpack_halves(u):
    return jnp.concatenate([_unpack_low(u), _unpack_high(u)], axis=1)


def _resln_kernel(*refs, with_h, with_router):
    x_ref, o_ref, gate_ref, g_ref, b_ref = refs[:5]
    pos = 5
    if with_h:
        sc_ref, sh_ref = refs[pos:pos + 2]
        pos += 2
    if with_router:
        rw_ref = refs[pos]
        pos += 1
    xo_ref = refs[pos]
    pos += 1
    z = DEEPNORM_ALPHA * x_ref[...] + gate_ref[...] * o_ref[...].astype(F32)
    xn = _layer_norm_rows(z, g_ref[...], b_ref[...])
    xo_ref[...] = xn
    if with_h:
        h_ref = refs[pos]
        pos += 1
        h = xn * (1.0 + sc_ref[...]) + sh_ref[...]
        if h_ref.dtype == jnp.uint32:
            h_ref[...] = _pack_halves(h)
        else:
            h_ref[...] = h.astype(h_ref.dtype)
        if with_router:
            lg_ref = refs[pos]
            lg_ref[...] = _dot(h, rw_ref[...], precision=lax.Precision.HIGHEST)


def _resln(x, o, mod, gate_slot, ln_g, ln_b, n_lat, *, next_mod=None, scale_slot=None,
           shift_slot=None, h_dtype=None, router_w=None, n_out_rows=None):
    t, d = x.shape
    n_out_rows = t if n_out_rows is None else n_out_rows
    tm = ROW_TILE
    nlb = n_lat // tm
    with_h = next_mod is not None
    with_router = router_w is not None
    row = pl.BlockSpec((tm, d), lambda i: (i, 0))
    vec = pl.BlockSpec((1, d), lambda i: (0, 0))
    in_specs = [row, row, _mod_spec(gate_slot, nlb, d), vec, vec]
    args = [x, o, mod, ln_g.reshape(1, d), ln_b.reshape(1, d)]
    out_specs = [row]
    out_shape = [jax.ShapeDtypeStruct((n_out_rows, d), F32)]
    if with_h:
        in_specs += [_mod_spec(scale_slot, nlb, d), _mod_spec(shift_slot, nlb, d)]
        args += [next_mod, next_mod]
        h_cols = d // 2 if h_dtype == jnp.uint32 else d
        out_specs.append(pl.BlockSpec((tm, h_cols), lambda i: (i, 0)))
        out_shape.append(jax.ShapeDtypeStruct((n_out_rows, h_cols), h_dtype))
    if with_router:
        ne = router_w.shape[1]
        in_specs.append(pl.BlockSpec((d, ne), lambda i: (0, 0)))
        args.append(router_w)
        out_specs.append(pl.BlockSpec((tm, ne), lambda i: (i, 0)))
        out_shape.append(jax.ShapeDtypeStruct((n_out_rows, ne), F32))
    return pl.pallas_call(
        functools.partial(_resln_kernel, with_h=with_h, with_router=with_router),
        grid=(n_out_rows // tm,),
        in_specs=in_specs,
        out_specs=out_specs,
        out_shape=out_shape,
        compiler_params=_params(1),
        name="residual_layernorm",
    )(*args)


def _gmlp_gate_kernel(u_ref, v_ref, vg_ref, vb_ref, ws_ref, bs_ref, o_ref, *, n_groups, gw):
    vn = _layer_norm_rows(v_ref[...].astype(F32), vg_ref[...], vb_ref[...]).astype(BF16)
    for g in range(n_groups):
        cols = slice(g * gw, (g + 1) * gw)
        sv = _dot(ws_ref[g], vn[:, cols]) + bs_ref[:, cols]
        o_ref[:, cols] = (u_ref[:, cols].astype(F32) * sv).astype(o_ref.dtype)


def _gmlp_gate(uv, v_g, v_b, w_s, b_s):
    t, two_w = uv.shape
    width = two_w // 2
    n_groups, chunk, _ = w_s.shape
    gw = width // n_groups
    bias_full = jnp.repeat(b_s.T, gw, axis=1).astype(F32)
    return pl.pallas_call(
        functools.partial(_gmlp_gate_kernel, n_groups=n_groups, gw=gw),
        grid=(t // chunk,),
        in_specs=[
            pl.BlockSpec((chunk, width), lambda i: (i, 0)),
            pl.BlockSpec((chunk, width), lambda i: (i, 1)),
            pl.BlockSpec((1, width), lambda i: (0, 0)),
            pl.BlockSpec((1, width), lambda i: (0, 0)),
            pl.BlockSpec((n_groups, chunk, chunk), lambda i: (0, 0, 0)),
            pl.BlockSpec((chunk, width), lambda i: (0, 0)),
        ],
        out_specs=pl.BlockSpec((chunk, width), lambda i: (i, 0)),
        out_shape=jax.ShapeDtypeStruct((t, width), BF16),
        compiler_params=_params(1),
        name="gmlp_gate",
    )(uv, uv, v_g.reshape(1, width), v_b.reshape(1, width), w_s.astype(BF16), bias_full)


def _na_bias_table(rpb, rows):
    nb = rows // NA_Q_ROWS
    h = rpb.shape[0]
    n_dr = 2 * NA_WIN_H - 1
    c = np.arange(GRID_W)
    dc = c[None, :] - c[:, None] + NA_WIN_W - 1
    cs = np.clip(c - NA_WIN_W // 2, 0, GRID_W - NA_WIN_W)
    ok_c = (c[None, :] >= cs[:, None]) & (c[None, :] < cs[:, None] + NA_WIN_W)
    sel = ((np.arange(2 * NA_WIN_W - 1)[:, None, None] == dc[None]) & ok_c[None]).astype(np.float32)
    planes = jnp.einsum("hrd,dqk->hrqk", rpb.astype(F32), jnp.asarray(sel), precision=lax.Precision.HIGHEST)
    planes = jnp.where(ok_c[None, None], planes, MASK_VALUE)
    planes = jnp.concatenate([planes, jnp.full((h, 1, GRID_W, GRID_W), MASK_VALUE, F32)], axis=1)
    tables = []
    for b in (0, 1, nb - 1):
        r0 = b * NA_Q_ROWS
        kw0 = int(np.clip(r0 - NA_WIN_H // 2, 0, rows - NA_KEY_ROWS))
        r = r0 + np.arange(NA_Q_ROWS)[:, None]
        kr = kw0 + np.arange(NA_KEY_ROWS)[None, :]
        rs = np.clip(r - NA_WIN_H // 2, 0, rows - NA_WIN_H)
        ok_r = (kr >= rs) & (kr < rs + NA_WIN_H)
        dr = np.where(ok_r, kr - r + NA_WIN_H - 1, n_dr)
        tbl = jnp.stack([planes[:, int(d)] for d in dr.reshape(-1)], axis=1)
        tbl = tbl.reshape(h, NA_Q_ROWS, NA_KEY_ROWS, GRID_W, GRID_W).transpose(0, 1, 3, 2, 4)
        tables.append(tbl.reshape(h, NA_Q_ROWS * GRID_W, NA_KEY_ROWS * GRID_W))
    return jnp.stack(tables, axis=0)


def _na_kernel(q_ref, k_ref, v_ref, bias_ref, o_ref, *, rows, n_lat, n_ctx):
    b = pl.program_id(1)
    kw0 = jnp.clip(b * NA_Q_ROWS - NA_WIN_H // 2, 0, rows - NA_KEY_ROWS)
    start = pl.multiple_of(kw0 * GRID_W, GRID_W)
    nk = NA_KEY_ROWS * GRID_W
    scale = HEAD_DIM ** -0.5
    for j in range(NA_HEADS_PER_STEP):
        cols = slice(j * HEAD_DIM, (j + 1) * HEAD_DIM)
        q = q_ref[:, cols]
        kw = k_ref[pl.ds(start, nk), cols]
        vw = v_ref[pl.ds(start, nk), cols]
        kc = k_ref[pl.ds(n_lat, n_ctx), cols]
        vc = v_ref[pl.ds(n_lat, n_ctx), cols]
        s_loc = _dot_nt(q, kw) * scale + bias_ref[j]
        s_ctx = _dot_nt(q, kc) * scale
        m = jnp.maximum(jnp.max(s_loc, axis=-1, keepdims=True), jnp.max(s_ctx, axis=-1, keepdims=True))
        p_loc = jnp.exp(s_loc - m)
        p_ctx = jnp.exp(s_ctx - m)
        l = jnp.sum(p_loc, axis=-1, keepdims=True) + jnp.sum(p_ctx, axis=-1, keepdims=True)
        o = _dot(p_loc.astype(BF16), vw) + _dot(p_ctx.astype(BF16), vc)
        o_ref[:, cols] = (o / l).astype(o_ref.dtype)


def _na_attention(qkv, bias_table, n_lat, n_ctx):
    t = qkv.shape[0]
    rows = n_lat // GRID_W
    nb = rows // NA_Q_ROWS
    tq = NA_Q_ROWS * GRID_W
    nk = NA_KEY_ROWS * GRID_W
    h = N_HEADS
    hs = NA_HEADS_PER_STEP
    hg = h // hs
    w = hs * HEAD_DIM
    return pl.pallas_call(
        functools.partial(_na_kernel, rows=rows, n_lat=n_lat, n_ctx=n_ctx),
        grid=(hg, nb),
        in_specs=[
            pl.BlockSpec((tq, w), lambda hh, b: (b, hh)),
            pl.BlockSpec((t, w), lambda hh, b: (0, hg + hh)),
            pl.BlockSpec((t, w), lambda hh, b: (0, 2 * hg + hh)),
            pl.BlockSpec((None, hs, tq, nk),
                         lambda hh, b: (jnp.where(b == 0, 0, jnp.where(b == nb - 1, 2, 1)), hh, 0, 0)),
        ],
        out_specs=pl.BlockSpec((tq, w), lambda hh, b: (b, hh)),
        out_shape=jax.ShapeDtypeStruct((n_lat, h * HEAD_DIM), BF16),
        compiler_params=_params(2),
        name="neighbourhood_attention",
    )(qkv, qkv, qkv, bias_table)


def _ctx_attn_kernel(q_ref, k_ref, v_ref, o_ref):
    s = _dot_nt(q_ref[...], k_ref[...]) * (HEAD_DIM ** -0.5)
    m = jnp.max(s, axis=-1, keepdims=True)
    p = jnp.exp(s - m)
    l = jnp.sum(p, axis=-1, keepdims=True)
    o_ref[...] = (_dot(p.astype(BF16), v_ref[...]) / l).astype(o_ref.dtype)


def _ctx_attention(qkv, n_lat, n_ctx):
    h = N_HEADS
    rb = n_lat // n_ctx
    return pl.pallas_call(
        _ctx_attn_kernel,
        grid=(h,),
        in_specs=[
            pl.BlockSpec((n_ctx, HEAD_DIM), lambda hh: (rb, hh)),
            pl.BlockSpec((n_ctx, HEAD_DIM), lambda hh: (rb, h + hh)),
            pl.BlockSpec((n_ctx, HEAD_DIM), lambda hh: (rb, 2 * h + hh)),
        ],
        out_specs=pl.BlockSpec((n_ctx, HEAD_DIM), lambda hh: (0, hh)),
        out_shape=jax.ShapeDtypeStruct((n_ctx, h * HEAD_DIM), BF16),
        compiler_params=_params(1),
        name="context_attention",
    )(qkv, qkv, qkv)


def _dft_cos_sin(k_times_n, period):
    ang = (2.0 * math.pi / period) * (k_times_n % period).astype(F32)
    return jnp.cos(ang), -jnp.sin(ang)


def _ft_stage2_kernel(g_ref, y_ref, o_ref, *, nb):
    for i in range(nb):
        rhs = jnp.concatenate([y_ref[0, i], y_ref[1, i]], axis=0)
        res = _dot(g_ref[i], rhs)
        half = res.shape[0] // 2
        o_ref[0, :, i, :] = res[:half].astype(o_ref.dtype)
        o_ref[1, :, i, :] = res[half:].astype(o_ref.dtype)


def _ft_positions_latent(h_lat):
    n, d = h_lat.shape
    l1 = l2 = int(round(math.sqrt(n)))
    assert l1 * l2 == n
    idx = jnp.arange(l1, dtype=jnp.int32)
    re, im = _dft_cos_sin(idx[:, None] * idx[None, :], l1)
    m1 = (jnp.concatenate([re, im], axis=0) * (l1 ** -0.5)).astype(BF16)
    y = _left_matmul(m1, h_lat.reshape(l1, l2 * d), out_dtype=BF16, tn=8192)
    y = y.reshape(2, l1, l2, d)
    k1 = idx[:, None, None]
    k2 = idx[None, :, None]
    n2 = idx[None, None, :]
    gr, gi = _dft_cos_sin(n2 * k2 * l1 + n2 * k1, n)
    g = jnp.concatenate(
        [jnp.concatenate([gr, -gi], axis=2), jnp.concatenate([gi, gr], axis=2)], axis=1
    )
    g = (g * (l2 ** -0.5)).astype(BF16)
    nb, cb = 8, min(1024, d)
    p = pl.pallas_call(
        functools.partial(_ft_stage2_kernel, nb=nb),
        grid=(l1 // nb, d // cb),
        in_specs=[
            pl.BlockSpec((nb, 2 * l2, 2 * l2), lambda a, c: (a, 0, 0)),
            pl.BlockSpec((2, nb, l2, cb), lambda a, c: (0, a, 0, c)),
        ],
        out_specs=pl.BlockSpec((2, l2, nb, cb), lambda a, c: (0, 0, a, c)),
        out_shape=jax.ShapeDtypeStruct((2, l2, l1, d), BF16),
        compiler_params=_params(2),
        name="ft_stage2",
    )(g, y)
    return p.reshape(2, n, d)


def _ft_positions_dense(h_ctx):
    n, d = h_ctx.shape
    idx = jnp.arange(n, dtype=jnp.int32)
    re, im = _dft_cos_sin(idx[:, None] * idx[None, :], n)
    m = (jnp.concatenate([re, im], axis=0) * (n ** -0.5)).astype(BF16)
    return _left_matmul(m, h_ctx, out_dtype=BF16, tn=min(d, 2048)).reshape(2, n, d)


def _ft_channel_kernel(pr_ref, pi_ref, c_ref, s_ref, o_ref):
    o_ref[...] = (_dot(pr_ref[...], c_ref[...]) + _dot(pi_ref[...], s_ref[...])).astype(o_ref.dtype)


def _ft_channels(p):
    _, t, d = p.shape
    gc = d // FT_GROUPS
    idx = jnp.arange(gc, dtype=jnp.int32)
    re, im = _dft_cos_sin(idx[:, None] * idx[None, :], gc)
    cc = (re * (gc ** -0.5)).astype(BF16)
    sc = (-im * (gc ** -0.5)).astype(BF16)
    tm = MM_TILE_M if t % MM_TILE_M == 0 else t
    return pl.pallas_call(
        _ft_channel_kernel,
        grid=(t // tm, FT_GROUPS),
        in_specs=[
            pl.BlockSpec((None, tm, gc), lambda i, g: (0, i, g)),
            pl.BlockSpec((None, tm, gc), lambda i, g: (1, i, g)),
            pl.BlockSpec((gc, gc), lambda i, g: (0, 0)),
            pl.BlockSpec((gc, gc), lambda i, g: (0, 0)),
        ],
        out_specs=pl.BlockSpec((tm, gc), lambda i, g: (i, g)),
        out_shape=jax.ShapeDtypeStruct((t, d), BF16),
        compiler_params=_params(2),
        name="ft_channels",
    )(p, p, cc, sc)


def _rope_tables(n_lat, n_ctx):
    half = HEAD_DIM // 2
    t = jnp.arange(n_lat)
    inv = ROPE_THETA ** (-jnp.arange(0, half, 2, dtype=F32) / half)
    ang_r = (t // GRID_W).astype(F32)[:, None] * inv[None, :]
    ang_c = (t % GRID_W).astype(F32)[:, None] * inv[None, :]
    cos = jnp.concatenate([jnp.cos(ang_r)] * 2 + [jnp.cos(ang_c)] * 2, axis=-1)
    sin = jnp.concatenate([-jnp.sin(ang_r), jnp.sin(ang_r), -jnp.sin(ang_c), jnp.sin(ang_c)], axis=-1)
    cos = jnp.concatenate([cos, jnp.ones((n_ctx, HEAD_DIM), F32)], axis=0)
    sin = jnp.concatenate([sin, jnp.zeros((n_ctx, HEAD_DIM), F32)], axis=0)
    return cos, sin


def _norm_rope_kernel(x_ref, g_ref, cos_ref, sin_ref, o_ref, *, n_heads, scale):
    quarter = HEAD_DIM // 4
    lane = lax.broadcasted_iota(jnp.int32, (x_ref.shape[0], HEAD_DIM), 1)
    first = (lane % (2 * quarter)) < quarter
    cos = cos_ref[...]
    sin = sin_ref[...]
    g = g_ref[...]
    for hh in range(n_heads):
        cols = slice(hh * HEAD_DIM, (hh + 1) * HEAD_DIM)
        x = x_ref[:, cols].astype(F32)
        xn = x * lax.rsqrt(jnp.mean(x * x, axis=-1, keepdims=True) + RMS_EPS) * g
        partner = jnp.where(first, pltpu.roll(xn, HEAD_DIM - quarter, 1), pltpu.roll(xn, quarter, 1))
        o_ref[:, cols] = ((xn * cos + partner * sin) * scale).astype(o_ref.dtype)


def _norm_rope(qkv, col_block, n_heads, gain, cos, sin, scale):
    t = qkv.shape[0]
    tm = ROW_TILE
    w = n_heads * HEAD_DIM
    return pl.pallas_call(
        functools.partial(_norm_rope_kernel, n_heads=n_heads, scale=scale),
        grid=(t // tm,),
        in_specs=[
            pl.BlockSpec((tm, w), lambda i: (i, col_block)),
            pl.BlockSpec((1, HEAD_DIM), lambda i: (0, 0)),
            pl.BlockSpec((tm, HEAD_DIM), lambda i: (i, 0)),
            pl.BlockSpec((tm, HEAD_DIM), lambda i: (i, 0)),
        ],
        out_specs=pl.BlockSpec((tm, w), lambda i: (i, 0)),
        out_shape=jax.ShapeDtypeStruct((t, w), BF16),
        compiler_params=_params(1),
        name="norm_rope",
    )(qkv, gain.reshape(1, HEAD_DIM), cos, sin)


def _lane_tiles(x):
    return [x[:, j * LANES:(j + 1) * LANES] for j in range(x.shape[1] // LANES)]


FLASH_SUB_ROWS = 64


def _flash_scores(q_sc, kt, s_sc, slot):
    s_sc[slot, :, :kt.shape[1]] = _dot(q_sc[...], kt)


def _flash_softmax(s_sc, p_sc, alpha_sc, m_sc, slot, n):
    for b in range(s_sc.shape[1] // FLASH_SUB_ROWS):
        rows = slice(b * FLASH_SUB_ROWS, (b + 1) * FLASH_SUB_ROWS)
        tiles = _lane_tiles(s_sc[slot, rows, :n])
        m_prev = m_sc[rows]
        m_new = jnp.maximum(m_prev, jnp.max(functools.reduce(jnp.maximum, tiles), axis=-1, keepdims=True))
        alpha_sc[slot, rows] = jnp.exp2(m_prev - m_new)
        p_sc[slot, rows, :n] = jnp.concatenate([jnp.exp2(t - m_new).astype(BF16) for t in tiles], axis=1)
        m_sc[rows] = m_new


def _flash_accumulate(p_sc, alpha_sc, acc_sc, slot, vs):
    n = vs.shape[0]
    v_ext = jnp.concatenate([vs, jnp.ones_like(vs)], axis=1)
    alpha = alpha_sc[slot]
    acc_sc[...] = jnp.concatenate([alpha, alpha], axis=1) * acc_sc[...] + _dot(p_sc[slot, :, :n], v_ext)


def _flash_kernel(q_ref, k_ref, v_ref, o_ref, q_sc, s_sc, p_sc, alpha_sc, m_sc, acc_sc, *,
                  rep, tq, tk, n_main, n_tail):
    for r in range(rep):
        q_sc[r * tq:(r + 1) * tq, :] = q_ref[:, r * HEAD_DIM:(r + 1) * HEAD_DIM]
    m_sc[...] = jnp.full(m_sc.shape, -jnp.inf, F32)
    acc_sc[...] = jnp.zeros(acc_sc.shape, F32)
    n_pairs = n_main // (2 * tk)

    def body(c, carry):
        base = pl.multiple_of(c * 2 * tk, 2 * tk)
        nxt = pl.multiple_of(jnp.minimum(c + 1, n_pairs - 1) * 2 * tk, 2 * tk)
        _flash_scores(q_sc, k_ref[:, pl.ds(base + tk, tk)], s_sc, 1)
        _flash_softmax(s_sc, p_sc, alpha_sc, m_sc, 0, tk)
        _flash_accumulate(p_sc, alpha_sc, acc_sc, 0, v_ref[pl.ds(base, tk), :])
        _flash_scores(q_sc, k_ref[:, pl.ds(nxt, tk)], s_sc, 0)
        _flash_softmax(s_sc, p_sc, alpha_sc, m_sc, 1, tk)
        _flash_accumulate(p_sc, alpha_sc, acc_sc, 1, v_ref[pl.ds(base + tk, tk), :])
        return carry

    if n_tail:
        _flash_scores(q_sc, k_ref[:, pl.ds(n_main, n_tail)], s_sc, 1)
    _flash_scores(q_sc, k_ref[:, pl.ds(0, tk)], s_sc, 0)
    if n_tail:
        _flash_softmax(s_sc, p_sc, alpha_sc, m_sc, 1, n_tail)
        _flash_accumulate(p_sc, alpha_sc, acc_sc, 1, v_ref[pl.ds(n_main, n_tail), :])
    lax.fori_loop(0, n_pairs, body, 0)
    for r in range(rep):
        rows = slice(r * tq, (r + 1) * tq)
        o_ref[:, r * HEAD_DIM:(r + 1) * HEAD_DIM] = (
            acc_sc[rows, :HEAD_DIM] / acc_sc[rows, HEAD_DIM:]).astype(o_ref.dtype)


def _gqa_attention(q, k_t, qkv, v_col_block0, n_q):
    t = k_t.shape[1]
    rep = N_HEADS // N_KV_HEADS
    tq, tk = GQA_TQ, GQA_TK
    n_main = t // (2 * tk) * (2 * tk)
    n_tail = t - n_main
    assert n_q % tq == 0 and n_tail % LANES == 0 and n_tail <= tk
    w = rep * HEAD_DIM
    return pl.pallas_call(
        functools.partial(_flash_kernel, rep=rep, tq=tq, tk=tk, n_main=n_main, n_tail=n_tail),
        grid=(N_KV_HEADS, n_q // tq),
        in_specs=[
            pl.BlockSpec((tq, w), lambda g, i: (i, g)),
            pl.BlockSpec((HEAD_DIM, t), lambda g, i: (g, 0)),
            pl.BlockSpec((t, HEAD_DIM), lambda g, i: (0, v_col_block0 + g)),
        ],
        out_specs=pl.BlockSpec((tq, w), lambda g, i: (i, g)),
        out_shape=jax.ShapeDtypeStruct((n_q, N_HEADS * HEAD_DIM), BF16),
        scratch_shapes=[
            pltpu.VMEM((rep * tq, HEAD_DIM), BF16),
            pltpu.VMEM((2, rep * tq, tk), F32),
            pltpu.VMEM((2, rep * tq, tk), BF16),
            pltpu.VMEM((2, rep * tq, HEAD_DIM), F32),
            pltpu.VMEM((rep * tq, HEAD_DIM), F32),
            pltpu.VMEM((rep * tq, 2 * HEAD_DIM), F32),
        ],
        compiler_params=_params(2),
        name="gqa_flash_attention",
    )(q, k_t, qkv)


def _route(logits, router_bias, block_rows):
    n_tok, n_e = logits.shape
    scores = jax.nn.sigmoid(logits)
    biased = scores + router_bias.astype(F32)
    per_g = n_e // N_EXPERT_GROUPS
    grp = biased.reshape(n_tok, N_EXPERT_GROUPS, per_g)
    g_first, g_mask1 = _first_max(grp)
    g_second, _ = _first_max(jnp.where(g_mask1, -jnp.inf, grp))
    g_score = g_first + g_second
    g_sel = jnp.zeros(g_score.shape, bool)
    for _ in range(TOP_K_GROUPS):
        _, pick = _first_max(jnp.where(g_sel, -jnp.inf, g_score))
        g_sel = g_sel | pick
    e_mask = jnp.repeat(g_sel, per_g, axis=1)
    cand = jnp.where(e_mask, biased, -jnp.inf)
    picks = []
    for _ in range(TOP_K):
        _, pick = _first_max(cand)
        picks.append(pick)
        cand = jnp.where(pick, -jnp.inf, cand)
    onehot = jnp.stack(picks, axis=1)
    w = jnp.sum(jnp.where(onehot, scores[:, None, :], 0.0), axis=-1)
    w = w / jnp.sum(w, axis=-1, keepdims=True) * ROUTED_SCALE

    n_asg = n_tok * TOP_K
    oh = onehot.reshape(n_asg, n_e).astype(jnp.int32)
    csum = jnp.cumsum(oh, axis=0)
    counts = csum[-1]
    padded = (counts + block_rows - 1) // block_rows * block_rows
    pad_end = jnp.cumsum(padded)
    pad_start = pad_end - padded
    dest = jnp.sum(oh * (csum - 1 + pad_start[None, :]), axis=1).astype(jnp.int32)
    n_blocks = -(-(n_asg + n_e * (block_rows - 1)) // block_rows)
    blk_row0 = jnp.arange(n_blocks, dtype=jnp.int32) * block_rows
    block_e = jnp.minimum(jnp.sum((pad_end[None, :] <= blk_row0[:, None]).astype(jnp.int32), axis=1), n_e - 1)
    be_oh = (block_e[:, None] == jnp.arange(n_e, dtype=jnp.int32)[None, :]).astype(jnp.int32)
    seg_left = jnp.sum(be_oh * (counts + pad_start)[None, :], axis=1) - blk_row0
    n_valid = jnp.clip(seg_left, 0, block_rows).astype(jnp.int32)
    codes = _plan_rows(dest, n_blocks * block_rows, block_rows)
    return w, block_e.astype(jnp.int32), n_valid, codes


def _first_max(x):
    m = jnp.max(x, axis=-1, keepdims=True)
    idx = lax.broadcasted_iota(jnp.int32, x.shape, x.ndim - 1)
    first = jnp.min(jnp.where(x == m, idx, x.shape[-1]), axis=-1, keepdims=True)
    return m[..., 0], idx == first


CODE_SLOT_SHIFT = 16
EXPERT_K_CHUNK = 512


def _plan_kernel(dest_hbm, pad_hbm, codes_ref, dest_smem, sem, *, chunk):
    j = pl.program_id(0)
    log_k = TOP_K.bit_length() - 1

    @pl.when(j == 0)
    def _():
        init = pltpu.make_async_copy(pad_hbm, codes_ref, sem)
        init.start()
        init.wait()

    cp = pltpu.make_async_copy(dest_hbm.at[j], dest_smem, sem)
    cp.start()
    cp.wait()

    def place(a, c):
        g = j * chunk + a
        codes_ref[dest_smem[a]] = lax.shift_right_logical(g, log_k) | ((g & (TOP_K - 1)) << CODE_SLOT_SHIFT)
        return c

    lax.fori_loop(0, chunk, place, 0, unroll=16)


def _plan_rows(dest, n_rows, block_rows):
    n_asg = dest.shape[0]
    assert block_rows & (block_rows - 1) == 0 and TOP_K & (TOP_K - 1) == 0
    chunk = max(c for c in range(LANES, 8192 + 1, LANES) if n_asg % c == 0)
    n_chunks = n_asg // chunk
    r = jnp.arange(n_rows, dtype=jnp.int32)
    pad_codes = ((r % block_rows) + ((r // block_rows) % 2) * block_rows) | (TOP_K << CODE_SLOT_SHIFT)
    return pl.pallas_call(
        functools.partial(_plan_kernel, chunk=chunk),
        grid=(n_chunks,),
        in_specs=[pl.BlockSpec(memory_space=pl.ANY), pl.BlockSpec(memory_space=pl.ANY)],
        out_specs=pl.BlockSpec(memory_space=pltpu.SMEM),
        out_shape=jax.ShapeDtypeStruct((n_rows,), jnp.int32),
        scratch_shapes=[pltpu.SMEM((chunk,), jnp.int32), pltpu.SemaphoreType.DMA(())],
        compiler_params=_params(1),
        name="plan_rows",
    )(dest.reshape(n_chunks, chunk), pad_codes)


def _swiglu_rows(x, wgu_ref, wd_ref):
    hgu = _dot(x, wgu_ref[...])
    de = hgu.shape[1] // 2
    gate = hgu[:, :de]
    hb = (gate * jax.nn.sigmoid(gate) * hgu[:, de:]).astype(BF16)
    return _dot(hb, wd_ref[...])


def _expert_kernel(be_ref, nv_ref, codes_ref, t_hbm, wg_ref, wu_ref, wd_ref, y_hbm,
                   xbuf, ybuf, sem_in, sem_out, *, n_tok, br):
    i = pl.program_id(0)
    last_block = pl.num_programs(0) - 2
    slot = i % 2
    active = nv_ref[i] > 0
    prev_active = (i > 0) & (nv_ref[jnp.maximum(i - 1, 0)] > 0)
    token_mask = (1 << CODE_SLOT_SHIFT) - 1
    half = xbuf.shape[2]
    n_chunks = 2 * half // EXPERT_K_CHUNK
    group = br // n_chunks

    def gather(block, buf_slot, lo, hi):
        for r in range(lo, hi):
            tok = codes_ref[block * br + r] & token_mask
            pltpu.make_async_copy(t_hbm.at[pl.ds(tok, 1)], xbuf.at[buf_slot, pl.ds(r, 1)],
                                  sem_in.at[buf_slot]).start(priority=r % 2)

    def gather_wait(buf_slot):
        pltpu.make_async_copy(t_hbm.at[pl.ds(0, br)], xbuf.at[buf_slot], sem_in.at[buf_slot]).wait()

    def scatter(block, buf_slot, lo, hi):
        for r in range(lo, hi):
            code = codes_ref[block * br + r]
            dst = lax.shift_right_logical(code, CODE_SLOT_SHIFT) * n_tok + (code & token_mask)
            pltpu.make_async_copy(ybuf.at[buf_slot, pl.ds(r, 1)], y_hbm.at[pl.ds(dst, 1)],
                                  sem_out.at[buf_slot]).start(priority=r % 2)

    def scatter_wait(buf_slot):
        pltpu.make_async_copy(ybuf.at[buf_slot], y_hbm.at[pl.ds(0, br)], sem_out.at[buf_slot]).wait()

    def compute(with_scatter):
        nxt = jnp.minimum(i + 1, last_block)
        gather_wait(slot)
        hgu = None
        for c in range(n_chunks):
            cols = slice(c * EXPERT_K_CHUNK, (c + 1) * EXPERT_K_CHUNK)
            lo = c * EXPERT_K_CHUNK % half
            words = xbuf[slot, :, lo:lo + EXPERT_K_CHUNK]
            xc = _unpack_low(words) if c * EXPERT_K_CHUNK < half else _unpack_high(words)
            w_chunk = jnp.concatenate([wg_ref[cols, :].astype(BF16), wu_ref[cols, :].astype(BF16)], axis=1)
            part = _dot(xc.astype(BF16), w_chunk)
            hgu = part if hgu is None else hgu + part
            gather(nxt, 1 - slot, c * group, (c + 1) * group)
            if with_scatter:
                scatter(i - 1, 1 - slot, c * group, (c + 1) * group)
        de = hgu.shape[1] // 2
        gate = hgu[:, :de]
        hb = (gate * jax.nn.sigmoid(gate) * hgu[:, de:]).astype(BF16)
        ybuf[slot] = _pack_halves(_dot(hb, wd_ref[...].astype(BF16)))
        if with_scatter:
            scatter_wait(1 - slot)

    @pl.when(i == 0)
    def _():
        gather(0, 0, 0, br)

    @pl.when(active & (i == 0))
    def _():
        compute(False)

    @pl.when(active & (i > 0))
    def _():
        compute(True)

    @pl.when(jnp.logical_not(active) & ((i == 0) | prev_active))
    def _():
        gather_wait(slot)

        @pl.when(prev_active)
        def _():
            scatter(i - 1, 1 - slot, 0, br)
            scatter_wait(1 - slot)


def _routed_experts(tok_packed, block_e, n_valid, codes, w_gate, w_up, w_down, layer):
    n_tok, half = tok_packed.shape
    d = 2 * half
    n_blocks = block_e.shape[0]
    br = codes.shape[0] // n_blocks
    de = w_gate.shape[3]
    assert half % EXPERT_K_CHUNK == 0 and br % (d // EXPERT_K_CHUNK) == 0
    block_e = jnp.concatenate([block_e, block_e[-1:]])
    n_valid = jnp.concatenate([n_valid, jnp.zeros((1,), jnp.int32)])
    grid_spec = pltpu.PrefetchScalarGridSpec(
        num_scalar_prefetch=3,
        grid=(n_blocks + 1,),
        in_specs=[
            pl.BlockSpec(memory_space=pl.ANY),
            pl.BlockSpec((None, None, d, de), lambda i, be, nv, cd: (layer, be[i], 0, 0)),
            pl.BlockSpec((None, None, d, de), lambda i, be, nv, cd: (layer, be[i], 0, 0)),
            pl.BlockSpec((None, None, de, d), lambda i, be, nv, cd: (layer, be[i], 0, 0)),
        ],
        out_specs=pl.BlockSpec(memory_space=pl.ANY),
        scratch_shapes=[
            pltpu.VMEM((2, br, half), jnp.uint32),
            pltpu.VMEM((2, br, half), jnp.uint32),
            pltpu.SemaphoreType.DMA((2,)),
            pltpu.SemaphoreType.DMA((2,)),
        ],
    )
    return pl.pallas_call(
        functools.partial(_expert_kernel, n_tok=n_tok, br=br),
        grid_spec=grid_spec,
        out_shape=jax.ShapeDtypeStruct((TOP_K * n_tok + 2 * br, half), jnp.uint32),
        compiler_params=_params(1),
        name="routed_experts",
    )(block_e, n_valid, codes, tok_packed, w_gate, w_up, w_down)


def _combine_kernel(*refs, with_h):
    (x_ref, tok_ref, sgu_ref, sd_ref, y0_ref, y1_ref, y2_ref, y3_ref, w_ref, gate_ref, g_ref,
     b_ref) = refs[:12]
    w = w_ref[...]
    y = _swiglu_rows(_unpack_halves(tok_ref[...]).astype(BF16), sgu_ref, sd_ref)
    for k, y_ref in enumerate((y0_ref, y1_ref, y2_ref, y3_ref)):
        y = y + w[:, k:k + 1] * _unpack_halves(y_ref[...])
    z = DEEPNORM_ALPHA * x_ref[...] + gate_ref[...] * y
    xn = _layer_norm_rows(z, g_ref[...], b_ref[...])
    if with_h:
        sc_ref, shf_ref, xo_ref, h_ref = refs[12:16]
        xo_ref[...] = xn
        h_ref[...] = (xn * (1.0 + sc_ref[...]) + shf_ref[...]).astype(h_ref.dtype)
    else:
        xo_ref = refs[12]
        xo_ref[...] = xn


def _moe_combine(x, tok_packed, s_gu, s_down, y_slots, w, mod, gate_slot, ln_g, ln_b, n_lat, *,
                 next_mod=None, n_out_rows=None):
    t, d = x.shape
    n_out_rows = t if n_out_rows is None else n_out_rows
    tm = COMBINE_ROWS
    nlb = n_lat // tm
    with_h = next_mod is not None
    y3 = y_slots
    de2 = s_gu.shape[1]
    row = pl.BlockSpec((tm, d), lambda i: (i, 0))
    packed_row = pl.BlockSpec((tm, d // 2), lambda i: (i, 0))
    vec = pl.BlockSpec((1, d), lambda i: (0, 0))

    def slot_spec(k):
        return pl.BlockSpec((tm, d // 2), lambda i: (k * (t // tm) + i, 0))

    in_specs = [
        row, packed_row,
        pl.BlockSpec((d, de2), lambda i: (0, 0)),
        pl.BlockSpec((de2 // 2, d), lambda i: (0, 0)),
    ] + [slot_spec(k) for k in range(TOP_K)] + [
        pl.BlockSpec((tm, TOP_K), lambda i: (i, 0)),
        _mod_spec(gate_slot, nlb, d), vec, vec,
    ]
    args = [x, tok_packed, s_gu, s_down, y3, y3, y3, y3, w, mod, ln_g.reshape(1, d), ln_b.reshape(1, d)]
    out_specs = [row]
    out_shape = [jax.ShapeDtypeStruct((n_out_rows, d), F32)]
    if with_h:
        in_specs += [_mod_spec(1, nlb, d), _mod_spec(0, nlb, d)]
        args += [next_mod, next_mod]
        out_specs.append(row)
        out_shape.append(jax.ShapeDtypeStruct((n_out_rows, d), BF16))
    return pl.pallas_call(
        functools.partial(_combine_kernel, with_h=with_h),
        grid=(n_out_rows // tm,),
        in_specs=in_specs,
        out_specs=out_specs,
        out_shape=out_shape,
        compiler_params=_params(1),
        name="moe_combine",
    )(*args)


def kernel(x, c, ctx, c_ctx, w_mod_down, w_mod_up, b_mod, ln_g, ln_b, gm_w_in, gm_v_g, gm_v_b, gm_w_s, gm_b_s, gm_w_out, na_w_qkv, na_rpb, na_w_out, ft_w_out, ga_w_qkv, ga_q_g, ga_k_g, ga_w_out, router_w, router_bias, exp_w_gate, exp_w_up, exp_w_down, sh_w_gate, sh_w_up, sh_w_down):
    b, n, d = x.shape
    nc = ctx.shape[1]
    assert b == 1 and c.shape[0] == 1
    n_tok = n + nc
    rows = n // GRID_W

    xs = jnp.concatenate([x[0], ctx[0]], axis=0)
    cond8 = jnp.zeros((8, d), F32).at[0].set(c[0]).at[1].set(c_ctx)
    mods = _ada_mod_all(cond8, w_mod_down, w_mod_up, b_mod)
    router_w_pad = jnp.pad(router_w, ((0, 0), (0, 0), (0, LANES - N_EXPERTS)))

    h = _modulate(xs, mods[0], 1, 0, n, BF16)
    for i in range(DEPTH):
        kind = i % 4
        mod = mods[i]
        last = i == DEPTH - 1

        if kind == 0:
            uv = _matmul(h, gm_w_in[0].astype(BF16), out_dtype=BF16, act="gelu")
            gated = _gmlp_gate(uv, gm_v_g[0], gm_v_b[0], gm_w_s[0], gm_b_s[0])
            o = _matmul(gated, gm_w_out[0].astype(BF16), out_dtype=BF16)
        elif kind == 1:
            qkv = _matmul(h, na_w_qkv[0].astype(BF16), out_dtype=BF16)
            a_lat = _na_attention(qkv, _na_bias_table(na_rpb[0], rows), n, nc)
            a_ctx = _ctx_attention(qkv, n, nc)
            o = _matmul(jnp.concatenate([a_lat, a_ctx], axis=0), na_w_out[0].astype(BF16), out_dtype=BF16)
        elif kind == 2:
            p = jnp.concatenate([_ft_positions_latent(h[:n]), _ft_positions_dense(h[n:])], axis=1)
            o = _matmul(_ft_channels(p), ft_w_out[0].astype(BF16), out_dtype=BF16)
        else:
            qkv = _matmul(h, ga_w_qkv[0].astype(BF16), out_dtype=BF16, tn=1024)
            cos, sin = _rope_tables(n, nc)
            qn = _norm_rope(qkv, 0, N_HEADS, ga_q_g[0], cos, sin, HEAD_DIM ** -0.5 * math.log2(math.e))
            kn = _norm_rope(qkv, N_HEADS // N_KV_HEADS, N_KV_HEADS, ga_k_g[0], cos, sin, 1.0)
            att = _gqa_attention(qn, kn.T, qkv, (N_HEADS + N_KV_HEADS), n)
            att = jnp.concatenate([att, jnp.zeros((nc, d), BF16)], axis=0)
            o = _matmul(att, ga_w_out[0].astype(BF16), out_dtype=BF16)

        xs, h2, logits = _resln(xs, o, mod, 2, ln_g[i, 0], ln_b[i, 0], n, next_mod=mod,
                                scale_slot=4, shift_slot=3, h_dtype=jnp.uint32, router_w=router_w_pad[i])

        w, block_e, n_valid, codes = _route(logits[:, :N_EXPERTS], router_bias[i], MOE_ROWS)
        y_slots = _routed_experts(h2, block_e, n_valid, codes, exp_w_gate, exp_w_up, exp_w_down, i)
        s_gu = jnp.concatenate([sh_w_gate[i], sh_w_up[i]], axis=-1).astype(BF16)
        s_down = sh_w_down[i].astype(BF16)
        if last:
            (xs,) = _moe_combine(xs, h2, s_gu, s_down, y_slots, w, mod, 5, ln_g[i, 1], ln_b[i, 1], n,
                                 n_out_rows=n)
        else:
            xs, h = _moe_combine(xs, h2, s_gu, s_down, y_slots, w, mod, 5, ln_g[i, 1], ln_b[i, 1], n,
                                 next_mod=mods[i + 1])
    return xs.reshape(b, n, d)
```
